```python
import math
import jax, jax.numpy as jnp
from jax import lax
import numpy as np

D_MODEL = 1024
BATCH = 2
SEQ = 8192
DEPTH = 1

GRID_W = 64
PLE_DIM = 256
ATT_HEADS = 8
ATT_KV_HEADS = 2
HEAD_DIM = 128
Q_BLOCK = 128
ROPE_THETA = 10000.0
DN_HEADS = 8
DN_DK = 128
DN_DV = 128
CONV_K = 5
CHUNK = 64
EPS = 1e-6

ATT_W = ATT_HEADS * HEAD_DIM
KV_W = ATT_KV_HEADS * HEAD_DIM
DN_KW = DN_HEADS * DN_DK
DN_VW = DN_HEADS * DN_DV
IN_SIZES = (ATT_W, KV_W, KV_W, ATT_W,
            DN_KW, DN_KW, DN_VW, 2 * DN_HEADS, 2 * DN_HEADS, DN_VW,
            D_MODEL, D_MODEL)
IN_W = ATT_W + 2 * KV_W + ATT_W + 2 * DN_KW + DN_VW + 4 * DN_HEADS + DN_VW + 2 * D_MODEL

kernel_name = 'hybrid_gqa_axialrope_gated_deltanet_bidir_block'


def rms_norm(x, w):
    xf = x.astype(jnp.float32)
    y = xf * lax.rsqrt(jnp.mean(xf * xf, axis=-1, keepdims=True) + EPS)
    return (y * w.astype(jnp.float32)).astype(x.dtype)


def l2_norm(x):
    xf = x.astype(jnp.float32)
    return xf * lax.rsqrt(jnp.sum(xf * xf, axis=-1, keepdims=True) + EPS)


def split_cols(t):
    outs, start = [], 0
    for size in IN_SIZES:
        outs.append(t[..., start:start + size])
        start += size
    return outs


def axial_rope_angles(seq_len):
    rows = seq_len // GRID_W
    row = jnp.broadcast_to(jnp.arange(rows)[:, None], (rows, GRID_W)).reshape(seq_len)
    col = jnp.broadcast_to(jnp.arange(GRID_W)[None, :], (rows, GRID_W)).reshape(seq_len)
    n_freq = HEAD_DIM // 4
    inv_freq = ROPE_THETA ** (-jnp.arange(n_freq, dtype=jnp.float32) / n_freq)
    ang = jnp.concatenate([row.astype(jnp.float32)[:, None] * inv_freq,
                           col.astype(jnp.float32)[:, None] * inv_freq], axis=-1)
    return jnp.cos(ang), jnp.sin(ang)


def apply_rope(x, cos, sin):
    xf = x.astype(jnp.float32).reshape(*x.shape[:-1], HEAD_DIM // 2, 2)
    x0, x1 = xf[..., 0], xf[..., 1]
    out = jnp.stack([x0 * cos - x1 * sin, x0 * sin + x1 * cos], axis=-1)
    return out.reshape(x.shape).astype(x.dtype)


def gqa_attention(q, k, v):
    B, Hq, S, D = q.shape
    G = Hq // ATT_KV_HEADS
    nb = S // Q_BLOCK
    qb = q.reshape(B, ATT_KV_HEADS, G, nb, Q_BLOCK, D).transpose(3, 0, 1, 2, 4, 5)
    scale = D ** -0.5

    def block(qi):
        s = jnp.einsum('bkgqd,bksd->bkgqs', qi, k).astype(jnp.float32) * scale
        pr = jax.nn.softmax(s, axis=-1).astype(v.dtype)
        return jnp.einsum('bkgqs,bksd->bkgqd', pr, v)

    o = lax.map(block, qb)
    return o.transpose(1, 2, 3, 0, 4, 5).reshape(B, Hq, S, D)


def short_conv(x, w):
    C = x.shape[-1]
    return lax.conv_general_dilated(
        x, w[:, None, :].astype(x.dtype), window_strides=(1,),
        padding=((CONV_K // 2, CONV_K // 2),),
        dimension_numbers=('NWC', 'WIO', 'NWC'), feature_group_count=C)


def chunk_gated_delta_rule(q, k, v, g, beta):
    B, H, S, DK = q.shape
    DV = v.shape[-1]
    n = S // CHUNK
    q = q.reshape(B, H, n, CHUNK, DK)
    k = k.reshape(B, H, n, CHUNK, DK)
    v = v.reshape(B, H, n, CHUNK, DV)
    beta = beta.reshape(B, H, n, CHUNK)
    g = jnp.cumsum(g.reshape(B, H, n, CHUNK), axis=-1)
    tril = jnp.tril(jnp.ones((CHUNK, CHUNK), dtype=bool))
    eye = jnp.eye(CHUNK, dtype=jnp.float32)
    decay = jnp.exp(jnp.where(tril, g[..., :, None] - g[..., None, :], -jnp.inf))
    k_beta = k * beta[..., None]
    v_beta = v * beta[..., None]
    L = jnp.tril(jnp.einsum('bhncd,bhnjd->bhncj', k_beta, k) * decay, -1)
    T = lax.linalg.triangular_solve(eye + L, jnp.broadcast_to(eye, L.shape),
                                    left_side=True, lower=True, unit_diagonal=True)
    u = jnp.einsum('bhncj,bhnjv->bhncv', T, v_beta)
    w = jnp.einsum('bhncj,bhnjd->bhncd', T, k_beta * jnp.exp(g)[..., None])
    a_intra = jnp.einsum('bhncd,bhnjd->bhncj', q, k) * decay

    def step(state, xs):
        q_c, k_c, u_c, w_c, g_c, a_c = xs
        g_last = g_c[..., -1]
        v_new = u_c - jnp.einsum('bhcd,bhdv->bhcv', w_c, state)
        o = (jnp.einsum('bhcd,bhdv->bhcv', q_c * jnp.exp(g_c)[..., None], state)
             + jnp.einsum('bhcj,bhjv->bhcv', a_c, v_new))
        k_dec = k_c * jnp.exp(g_last[..., None] - g_c)[..., None]
        state = state * jnp.exp(g_last)[..., None, None] + jnp.einsum('bhcd,bhcv->bhdv', k_dec, v_new)
        return state, o

    front = lambda t: jnp.moveaxis(t, 2, 0)
    xs = (front(q), front(k), front(u), front(w), front(g), front(a_intra))
    state0 = jnp.zeros((B, H, DK, DV), jnp.float32)
    _, o = lax.scan(step, state0, xs)
    return jnp.moveaxis(o, 0, 2).reshape(B, H, S, DV)


def bidir_gated_deltanet(q, k, v, g, beta):
    flip = lambda t: jnp.flip(t, axis=2)
    o_f = chunk_gated_delta_rule(q, k, v, g[0], beta[0])
    o_b = flip(chunk_gated_delta_rule(flip(q), flip(k), flip(v), flip(g[1]), flip(beta[1])))
    return o_f + o_b


def setup_inputs(seed: int = 0) -> dict:
    key = jax.random.key(seed)
    ks = jax.random.split(key, 20)
    nrm = lambda k, shape, scale: jax.random.normal(k, shape, jnp.float32) * scale
    gain = lambda k, shape: 1.0 + 0.02 * jax.random.normal(k, shape, jnp.float32)
    dt = jnp.exp(jax.random.uniform(ks[8], (DEPTH, 2, DN_HEADS), jnp.float32,
                                    math.log(1e-3), math.log(1e-1)))
    return {
        'x': nrm(ks[0], (BATCH, SEQ, D_MODEL), 1.0),
        'p': nrm(ks[1], (DEPTH, BATCH, SEQ, PLE_DIM), 1.0),
        'norm_pre': gain(ks[2], (DEPTH, D_MODEL)),
        'w_in': nrm(ks[3], (DEPTH, D_MODEL, IN_W), D_MODEL ** -0.5),
        'q_norm': gain(ks[4], (DEPTH, HEAD_DIM)),
        'k_norm': gain(ks[5], (DEPTH, HEAD_DIM)),
        'conv_w': nrm(ks[6], (DEPTH, CONV_K, 2 * DN_KW + DN_VW), CONV_K ** -0.5),
        'a_log': jnp.log(jax.random.uniform(ks[7], (DEPTH, 2, DN_HEADS), jnp.float32, 1.0, 16.0)),
        'dt_bias': dt + jnp.log(-jnp.expm1(-dt)),
        'dn_norm': gain(ks[9], (DEPTH, DN_DV)),
        'w_br_att': nrm(ks[10], (DEPTH, ATT_W, D_MODEL), ATT_W ** -0.5),
        'w_br_dn': nrm(ks[11], (DEPTH, DN_VW, D_MODEL), DN_VW ** -0.5),
        'w_out': nrm(ks[12], (DEPTH, D_MODEL, D_MODEL), D_MODEL ** -0.5),
        'norm_post': gain(ks[13], (DEPTH, D_MODEL)),
        'w_ple_proj': nrm(ks[14], (DEPTH, PLE_DIM, D_MODEL), PLE_DIM ** -0.5),
        'w_ple_gate': nrm(ks[15], (DEPTH, D_MODEL, D_MODEL), D_MODEL ** -0.5),
        'ple_norm': gain(ks[16], (DEPTH, D_MODEL)),
    }


def reference(x, p, norm_pre, w_in, q_norm, k_norm, conv_w, a_log, dt_bias, dn_norm,
              w_br_att, w_br_dn, w_out, norm_post, w_ple_proj, w_ple_gate, ple_norm):
    B, S, _ = x.shape
    cos, sin = axial_rope_angles(S)
    heads = lambda t, nh, hd: t.reshape(B, S, nh, hd).transpose(0, 2, 1, 3)
    for i in range(DEPTH):
        h = rms_norm(x, norm_pre[i])
        (aq, ak, av, az, dq, dk, dv, db, da, dz, gate_att, gate_dn) = split_cols(h @ w_in[i])

        q = apply_rope(rms_norm(heads(aq, ATT_HEADS, HEAD_DIM), q_norm[i]), cos, sin)
        k = apply_rope(rms_norm(heads(ak, ATT_KV_HEADS, HEAD_DIM), k_norm[i]), cos, sin)
        v = heads(av, ATT_KV_HEADS, HEAD_DIM)
        o_att = gqa_attention(q, k, v).transpose(0, 2, 1, 3).reshape(B, S, ATT_W)
        y_att = (o_att * jax.nn.silu(az)) @ w_br_att[i]

        qkv = jax.nn.silu(short_conv(jnp.concatenate([dq, dk, dv], axis=-1), conv_w[i]))
        cq, ck, cv = qkv[..., :DN_KW], qkv[..., DN_KW:2 * DN_KW], qkv[..., 2 * DN_KW:]
        qd = l2_norm(heads(cq, DN_HEADS, DN_DK)) * (DN_DK ** -0.5)
        kd = l2_norm(heads(ck, DN_HEADS, DN_DK))
        vd = heads(cv, DN_HEADS, DN_DV).astype(jnp.float32)
        da4 = da.astype(jnp.float32).reshape(B, S, 2, DN_HEADS)
        db4 = db.astype(jnp.float32).reshape(B, S, 2, DN_HEADS)
        g = -jnp.exp(a_log[i].astype(jnp.float32)) * jax.nn.softplus(da4 + dt_bias[i].astype(jnp.float32))
        beta = jax.nn.sigmoid(db4)
        g = g.transpose(2, 0, 3, 1)
        beta = beta.transpose(2, 0, 3, 1)
        o_dn = bidir_gated_deltanet(qd, kd, vd, g, beta).astype(x.dtype)
        o_dn = rms_norm(o_dn.transpose(0, 2, 1, 3), dn_norm[i]).reshape(B, S, DN_VW)
        y_dn = (o_dn * jax.nn.silu(dz)) @ w_br_dn[i]

        mix = (jax.nn.sigmoid(gate_att) * y_att + jax.nn.sigmoid(gate_dn) * y_dn) @ w_out[i]
        x = x + rms_norm(mix, norm_post[i])

        e = p[i] @ w_ple_proj[i]
        x = x + rms_norm(jax.nn.sigmoid(x @ w_ple_gate[i]) * e, ple_norm[i])
    return x
```

```python
import functools
import math

import numpy as np
import jax
import jax.numpy as jnp
from jax import lax
from jax.experimental import pallas as pl
from jax.experimental.pallas import tpu as pltpu

F32 = jnp.float32
BF16 = jnp.bfloat16

GRID_W = 64
ATT_HEADS = 8
ATT_KV_HEADS = 2
HEAD_DIM = 128
ROPE_THETA = 10000.0
DN_HEADS = 8
DN_DK = 128
DN_DV = 128
CONV_K = 5
CHUNK = 64
EPS = 1e-6

ATT_W = ATT_HEADS * HEAD_DIM
KV_W = ATT_KV_HEADS * HEAD_DIM
DN_KW = DN_HEADS * DN_DK
DN_VW = DN_HEADS * DN_DV
GROUP = ATT_HEADS // ATT_KV_HEADS
LANES = 128

OFF_AQ = 0
OFF_AZ = OFF_AQ + ATT_W
OFF_GA = OFF_AZ + ATT_W
OFF_GD = OFF_GA + ATT_W


def _deinterleave(n_heads):
    idx = []
    half = HEAD_DIM // 2
    for h in range(n_heads):
        base = h * HEAD_DIM
        idx += [base + 2 * l for l in range(half)] + [base + 2 * l + 1 for l in range(half)]
    return np.asarray(idx, np.int32)


def _column_order(d_model):
    sizes = (ATT_W, KV_W, KV_W, ATT_W, DN_KW, DN_KW, DN_VW, 2 * DN_HEADS, 2 * DN_HEADS, DN_VW,
             d_model, d_model)
    starts = np.concatenate([[0], np.cumsum(sizes)])[:-1]
    aq, ak, av, az, dq, dk, dv, db, da, dz, ga, gd = [np.arange(s, s + n, dtype=np.int32)
                                                       for s, n in zip(starts, sizes)]
    aq = aq[_deinterleave(ATT_HEADS)]
    ak = ak[_deinterleave(ATT_KV_HEADS)]
    groups = [("aq", aq), ("az", az), ("ga", ga), ("gd", gd), ("dz", dz), ("dq", dq), ("dk", dk),
              ("dv", dv), ("ak", ak), ("av", av)]
    offsets, off = {}, 0
    for name, g in groups:
        offsets[name] = off
        off += len(g)
    main = np.concatenate([g for _, g in groups])
    small = np.concatenate([db, da])
    return main, small, offsets


def _proj_kernel(x_ref, g_ref, w_ref, ws_ref, o_ref, os_ref, h_ref):
    @pl.when(pl.program_id(1) == 0)
    def _():
        x = x_ref[...]
        r = lax.rsqrt(jnp.mean(x * x, axis=-1, keepdims=True) + EPS)
        h = (x * r * g_ref[...]).astype(BF16)
        h_ref[...] = h
        os_ref[...] = jnp.dot(h, ws_ref[...], preferred_element_type=F32)

    o_ref[...] = jnp.dot(h_ref[...], w_ref[...], preferred_element_type=F32).astype(BF16)


def _proj(x2, g, w_main, w_small, tm=1024, tn=512):
    n, d = x2.shape
    main_w = w_main.shape[1]
    tm = min(tm, n)
    return pl.pallas_call(
        _proj_kernel,
        grid=(n // tm, main_w // tn),
        in_specs=[
            pl.BlockSpec((tm, d), lambda i, j: (i, 0)),
            pl.BlockSpec((1, d), lambda i, j: (0, 0)),
            pl.BlockSpec((d, tn), lambda i, j: (0, j)),
            pl.BlockSpec((d, LANES), lambda i, j: (0, 0)),
        ],
        out_specs=[
            pl.BlockSpec((tm, tn), lambda i, j: (i, j)),
            pl.BlockSpec((tm, LANES), lambda i, j: (i, 0)),
        ],
        out_shape=[
            jax.ShapeDtypeStruct((n, main_w), BF16),
            jax.ShapeDtypeStruct((n, LANES), F32),
        ],
        scratch_shapes=[pltpu.VMEM((tm, d), BF16)],
        compiler_params=pltpu.CompilerParams(dimension_semantics=("parallel", "arbitrary")),
        name="proj",
    )(x2, g, w_main, w_small)


def _norm_rope(xh, w, cos, sin):
    r = lax.rsqrt(jnp.mean(xh * xh, axis=-1, keepdims=True) + EPS)
    xn = xh * r * w
    return xn * cos + pltpu.roll(xn, HEAD_DIM // 2, axis=1) * sin


def _kprep_kernel(k_ref, cos_ref, sin_ref, w_ref, o_ref):
    cos, sin, w = cos_ref[...], sin_ref[...], w_ref[...]
    for h in range(ATT_KV_HEADS):
        sl = slice(h * HEAD_DIM, (h + 1) * HEAD_DIM)
        o_ref[:, sl] = _norm_rope(k_ref[:, sl].astype(F32), w, cos, sin).astype(BF16)


def _kprep(proj, cos, sin, w, off_ak, seq, tm=1024):
    n = proj.shape[0]
    tm = min(tm, seq)
    tiles_per_seq = seq // tm
    return pl.pallas_call(
        _kprep_kernel,
        grid=(n // tm,),
        in_specs=[
            pl.BlockSpec((tm, KV_W), lambda i: (i, off_ak // KV_W)),
            pl.BlockSpec((tm, HEAD_DIM), lambda i: (i % tiles_per_seq, 0)),
            pl.BlockSpec((tm, HEAD_DIM), lambda i: (i % tiles_per_seq, 0)),
            pl.BlockSpec((1, HEAD_DIM), lambda i: (0, 0)),
        ],
        out_specs=pl.BlockSpec((tm, KV_W), lambda i: (i, 0)),
        out_shape=jax.ShapeDtypeStruct((n, KV_W), BF16),
        compiler_params=pltpu.CompilerParams(dimension_semantics=("parallel",)),
        name="kprep",
    )(proj, cos, sin, w)


def _attn_kernel(q_ref, z_ref, k_ref, v_ref, cos_ref, sin_ref, w_ref, o_ref, *, tq, tk, seq):
    cos, sin, w = cos_ref[...], sin_ref[...], w_ref[...]
    scale = HEAD_DIM ** -0.5
    qs = []
    for g in range(GROUP):
        sl = slice(g * HEAD_DIM, (g + 1) * HEAD_DIM)
        qs.append((_norm_rope(q_ref[:, sl].astype(F32), w, cos, sin) * scale).astype(BF16))
    q = jnp.concatenate(qs, axis=0)
    rows = GROUP * tq

    def body(c, carry):
        m, l, acc = carry
        start = pl.multiple_of(c * tk, tk)
        kc = k_ref[pl.ds(start, tk), :]
        vc = v_ref[pl.ds(start, tk), :]
        s = lax.dot_general(q, kc, (((1,), (1,)), ((), ())), preferred_element_type=F32)
        m_new = jnp.maximum(m, jnp.max(s, axis=-1, keepdims=True))
        alpha = jnp.exp(m - m_new)
        p = jnp.exp(s - m_new)
        l = alpha * l + jnp.sum(p, axis=-1, keepdims=True)
        acc = alpha * acc + jnp.dot(p.astype(BF16), vc, preferred_element_type=F32)
        return m_new, l, acc

    m0 = jnp.full((rows, 1), -jnp.inf, F32)
    l0 = jnp.zeros((rows, 1), F32)
    a0 = jnp.zeros((rows, HEAD_DIM), F32)
    _, l, acc = lax.fori_loop(0, seq // tk, body, (m0, l0, a0))
    o = acc / l
    for g in range(GROUP):
        sl = slice(g * HEAD_DIM, (g + 1) * HEAD_DIM)
        z = z_ref[:, sl].astype(F32)
        o_ref[:, sl] = (o[g * tq:(g + 1) * tq, :] * (z * jax.nn.sigmoid(z))).astype(BF16)


def _attention(proj, k_rot, cos, sin, w, offs, batch, seq, tq=256, tk=512):
    n = proj.shape[0]
    gw = GROUP * HEAD_DIM
    qt = seq // tq
    kern = functools.partial(_attn_kernel, tq=tq, tk=tk, seq=seq)
    return pl.pallas_call(
        kern,
        grid=(batch, ATT_KV_HEADS, qt),
        in_specs=[
            pl.BlockSpec((tq, gw), lambda b, h, i: (b * qt + i, offs["aq"] // gw + h)),
            pl.BlockSpec((tq, gw), lambda b, h, i: (b * qt + i, offs["az"] // gw + h)),
            pl.BlockSpec((seq, HEAD_DIM), lambda b, h, i: (b, h)),
            pl.BlockSpec((seq, HEAD_DIM), lambda b, h, i: (b, offs["av"] // HEAD_DIM + h)),
            pl.BlockSpec((tq, HEAD_DIM), lambda b, h, i: (i, 0)),
            pl.BlockSpec((tq, HEAD_DIM), lambda b, h, i: (i, 0)),
            pl.BlockSpec((1, HEAD_DIM), lambda b, h, i: (0, 0)),
        ],
        out_specs=pl.BlockSpec((tq, gw), lambda b, h, i: (b * qt + i, h)),
        out_shape=jax.ShapeDtypeStruct((n, ATT_W), BF16),
        compiler_params=pltpu.CompilerParams(
            dimension_semantics=("parallel", "parallel", "arbitrary")),
        name="attn",
    )(proj, proj, k_rot, proj, cos, sin, w)


def _dprep_kernel(x_ref, w_ref, o_ref, xs_ref, *, seq, rows, q_blocks, k_blocks):
    pad = 8
    cb = x_ref.shape[1]
    j = pl.program_id(1)
    xs_ref[0:pad, :] = jnp.zeros((pad, cb), F32)
    xs_ref[pad + seq:pad + seq + pad, :] = jnp.zeros((pad, cb), F32)
    xs_ref[pad:pad + seq, :] = x_ref[...].astype(F32)
    w = w_ref[...]
    is_qk = j < (q_blocks + k_blocks)
    post = jnp.where(j < q_blocks, DN_DK ** -0.5, 1.0).astype(F32)
    for r in range(seq // rows):
        base = pad + r * rows - CONV_K // 2
        acc = jnp.zeros((rows, cb), F32)
        for t in range(CONV_K):
            acc = acc + xs_ref[base + t:base + t + rows, :] * w[t:t + 1, :]
        y = acc * jax.nn.sigmoid(acc)
        for h in range(cb // DN_DK):
            sl = slice(h * DN_DK, (h + 1) * DN_DK)
            yh = y[:, sl]
            inv = lax.rsqrt(jnp.sum(yh * yh, axis=-1, keepdims=True) + EPS) * post
            yh = yh * jnp.where(is_qk, inv, 1.0)
            o_ref[r * rows:(r + 1) * rows, sl] = yh.astype(BF16)


def _dprep(proj, conv_w8, off_dq, batch, seq, cb=256, rows=512):
    n = proj.shape[0]
    width = 2 * DN_KW + DN_VW
    kern = functools.partial(_dprep_kernel, seq=seq, rows=rows, q_blocks=DN_KW // cb,
                             k_blocks=DN_KW // cb)
    return pl.pallas_call(
        kern,
        grid=(batch, width // cb),
        in_specs=[
            pl.BlockSpec((seq, cb), lambda b, j: (b, off_dq // cb + j)),
            pl.BlockSpec((8, cb), lambda b, j: (0, j)),
        ],
        out_specs=pl.BlockSpec((seq, cb), lambda b, j: (b, j)),
        out_shape=jax.ShapeDtypeStruct((n, width), BF16),
        scratch_shapes=[pltpu.VMEM((seq + 16, cb), F32)],
        compiler_params=pltpu.CompilerParams(dimension_semantics=("parallel", "parallel")),
        name="dprep",
    )(proj, conv_w8)


def _split3(x):
    p1 = x.astype(BF16)
    r1 = x - p1.astype(F32)
    p2 = r1.astype(BF16)
    p3 = (r1 - p2.astype(F32)).astype(BF16)
    return p1, p2, p3


def _softplus(x):
    return jnp.maximum(x, 0.0) + jnp.log(1.0 + jnp.exp(-jnp.abs(x)))


def _delta_kernel(xf_ref, xb_ref, cf_ref, cb_ref, rf_ref, rb_ref, pc_ref, pr_ref,
                  of_ref, ob_ref, s_ref):
    C = CHUNK
    nh = DN_HEADS

    @pl.when(pl.program_id(1) == 0)
    def _():
        s_ref[...] = jnp.zeros(s_ref.shape, F32)

    ri = lax.broadcasted_iota(jnp.int32, (C, C), 0)
    ci = lax.broadcasted_iota(jnp.int32, (C, C), 1)
    lower_incl = ri >= ci
    upper_incl = ri <= ci
    eye = (ri == ci).astype(F32)
    tril_b = lower_incl.astype(BF16)
    triu_b = upper_incl.astype(BF16)
    same = {1: ri == ci}
    for lg in range(1, 7):
        same[1 << lg] = (ri >> lg) == (ci >> lg)
    level_masks = [(s, same[s] & jnp.logical_not(same[s // 2])) for s in (4, 8, 16, 32, 64)]
    base_mask = same[2]

    a_col = pc_ref[0:1, :]
    b_col = pc_ref[1:2, :]
    a_row = pr_ref[:, 0:1]
    b_row = pr_ref[:, 1:2]

    for d, (x_ref, c_ref, r_ref, o_ref) in enumerate(
            ((xf_ref, cf_ref, rf_ref, of_ref), (xb_ref, cb_ref, rb_ref, ob_ref))):
        fwd = d == 0
        incl = lower_incl if fwd else upper_incl
        cum_col = tril_b if fwd else triu_b
        cum_row = triu_b if fwd else tril_b
        blk = c_ref[...]
        beta_all = jax.nn.sigmoid(blk)
        g_all = -jnp.exp(a_col) * _softplus(blk + b_col)
        gcum = sum(jnp.dot(cum_col, p, preferred_element_type=F32) for p in _split3(g_all))
        blk_t = r_ref[...]
        g_t = -jnp.exp(a_row) * _softplus(blk_t + b_row)
        gcum_t = sum(jnp.dot(p, cum_row, preferred_element_type=F32) for p in _split3(g_t))

        for h in range(nh):
            bcol = d * nh + h
            gcolumn = 2 * nh + d * nh + h
            beta = jnp.broadcast_to(beta_all[:, bcol:bcol + 1], (C, DN_DK))
            gc = jnp.broadcast_to(gcum[:, gcolumn:gcolumn + 1], (C, DN_DK))
            gr = jnp.broadcast_to(gcum_t[gcolumn:gcolumn + 1, :], (C, C))
            g_last = gc[C - 1:C, :] if fwd else gc[0:1, :]

            qh = x_ref[:, h * DN_DK:(h + 1) * DN_DK]
            kh = x_ref[:, DN_KW + h * DN_DK:DN_KW + (h + 1) * DN_DK]
            vh = x_ref[:, 2 * DN_KW + h * DN_DV:2 * DN_KW + (h + 1) * DN_DV]
            kf = kh.astype(F32)
            qk_kk = lax.dot_general(jnp.concatenate([qh, kh], axis=0), kh,
                                    (((1,), (1,)), ((), ())), preferred_element_type=F32)
            qk, kk = qk_kk[:C], qk_kk[C:]

            decay = jnp.exp(jnp.where(incl, gc[:, :C] - gr, -1e30))
            lmat = jnp.where(ri != ci, beta[:, :C] * kk * decay, 0.0)
            amat = qk * decay

            xinv = eye - jnp.where(base_mask, lmat, 0.0)
            for _, msk in level_masks:
                e = jnp.where(msk, lmat, 0.0).astype(BF16)
                xb = xinv.astype(BF16)
                y = jnp.dot(e, xb, preferred_element_type=F32)
                xinv = xinv - jnp.dot(xb, y.astype(BF16), preferred_element_type=F32)

            eg = jnp.exp(gc)
            rhs = jnp.concatenate([kf * (beta * eg), vh.astype(F32) * beta], axis=1).astype(BF16)
            wu = jnp.dot(xinv.astype(BF16), rhs, preferred_element_type=F32)
            w_, u_ = wu[:, :DN_DK], wu[:, DN_DK:]

            state = s_ref[d * nh + h]
            sb = state.astype(BF16)
            v_new = u_ - jnp.dot(w_.astype(BF16), sb, preferred_element_type=F32)
            vb = v_new.astype(BF16)
            o = (jnp.dot((qh.astype(F32) * eg).astype(BF16), sb, preferred_element_type=F32)
                 + jnp.dot(amat.astype(BF16), vb, preferred_element_type=F32))
            k_dec = (kf * jnp.exp(g_last - gc)).astype(BF16)
            s_ref[d * nh + h] = state * jnp.exp(g_last) + lax.dot_general(
                k_dec, vb, (((0,), (0,)), ((), ())), preferred_element_type=F32)
            o_ref[:, h * DN_DV:(h + 1) * DN_DV] = o.astype(BF16)


def _delta(dqkv, small, small_t, par_col, par_row, batch, seq):
    n = dqkv.shape[0]
    nc = seq // CHUNK
    width = dqkv.shape[1]
    fwd = lambda b, i: (b * nc + i, 0)
    bwd = lambda b, i: (b * nc + nc - 1 - i, 0)
    fwd3 = lambda b, i: (b * nc + i, 0, 0)
    bwd3 = lambda b, i: (b * nc + nc - 1 - i, 0, 0)
    return pl.pallas_call(
        _delta_kernel,
        grid=(batch, nc),
        in_specs=[
            pl.BlockSpec((CHUNK, width), fwd),
            pl.BlockSpec((CHUNK, width), bwd),
            pl.BlockSpec((CHUNK, LANES), fwd),
            pl.BlockSpec((CHUNK, LANES), bwd),
            pl.BlockSpec((None, 4 * DN_HEADS, CHUNK), fwd3),
            pl.BlockSpec((None, 4 * DN_HEADS, CHUNK), bwd3),
            pl.BlockSpec((8, LANES), lambda b, i: (0, 0)),
            pl.BlockSpec((4 * DN_HEADS, LANES), lambda b, i: (0, 0)),
        ],
        out_specs=[
            pl.BlockSpec((CHUNK, DN_VW), fwd),
            pl.BlockSpec((CHUNK, DN_VW), bwd),
        ],
        out_shape=[
            jax.ShapeDtypeStruct((n, DN_VW), BF16),
            jax.ShapeDtypeStruct((n, DN_VW), BF16),
        ],
        scratch_shapes=[pltpu.VMEM((2 * DN_HEADS, DN_DK, DN_DV), F32)],
        compiler_params=pltpu.CompilerParams(dimension_semantics=("parallel", "arbitrary")),
        name="delta",
    )(dqkv, dqkv, small, small, small_t, small_t, par_col, par_row)


def _post_kernel(oa_ref, of_ref, ob_ref, dz_ref, ga_ref, gd_ref, x_ref, p_ref,
                 wa_ref, wd_ref, wo_ref, wg_ref, wp_ref, nd_ref, npost_ref, nple_ref, o_ref):
    def rms(t, w):
        return t * lax.rsqrt(jnp.mean(t * t, axis=-1, keepdims=True) + EPS) * w

    nd = nd_ref[...]
    parts = []
    for h in range(DN_HEADS):
        sl = slice(h * DN_DV, (h + 1) * DN_DV)
        od = of_ref[:, sl].astype(F32) + ob_ref[:, sl].astype(F32)
        z = dz_ref[:, sl].astype(F32)
        parts.append((rms(od, nd) * (z * jax.nn.sigmoid(z))).astype(BF16))
    y_dn = jnp.dot(jnp.concatenate(parts, axis=1), wd_ref[...], preferred_element_type=F32)
    y_att = jnp.dot(oa_ref[...], wa_ref[...], preferred_element_type=F32)
    merged = (jax.nn.sigmoid(ga_ref[...].astype(F32)) * y_att
              + jax.nn.sigmoid(gd_ref[...].astype(F32)) * y_dn)
    mix = jnp.dot(merged.astype(BF16), wo_ref[...], preferred_element_type=F32)
    x1 = x_ref[...] + rms(mix, npost_ref[...])
    gate = jax.nn.sigmoid(jnp.dot(x1.astype(BF16), wg_ref[...], preferred_element_type=F32))
    e = jnp.dot(p_ref[...].astype(BF16), wp_ref[...], preferred_element_type=F32)
    o_ref[...] = x1 + rms(gate * e, nple_ref[...])


def _post(o_att, o_f, o_b, proj, x2, p2, wa, wd, wo, wg, wp, nd, npost, nple, offs, tm=512):
    n, d = x2.shape
    ple = p2.shape[1]
    row = lambda i: (i, 0)
    const = lambda i: (0, 0)
    return pl.pallas_call(
        _post_kernel,
        grid=(n // tm,),
        in_specs=[
            pl.BlockSpec((tm, ATT_W), row),
            pl.BlockSpec((tm, DN_VW), row),
            pl.BlockSpec((tm, DN_VW), row),
            pl.BlockSpec((tm, DN_VW), lambda i: (i, offs["dz"] // DN_VW)),
            pl.BlockSpec((tm, d), lambda i: (i, offs["ga"] // d)),
            pl.BlockSpec((tm, d), lambda i: (i, offs["gd"] // d)),
            pl.BlockSpec((tm, d), row),
            pl.BlockSpec((tm, ple), row),
            pl.BlockSpec((ATT_W, d), const),
            pl.BlockSpec((DN_VW, d), const),
            pl.BlockSpec((d, d), const),
            pl.BlockSpec((d, d), const),
            pl.BlockSpec((ple, d), const),
            pl.BlockSpec((1, DN_DV), const),
            pl.BlockSpec((1, d), const),
            pl.BlockSpec((1, d), const),
        ],
        out_specs=pl.BlockSpec((tm, d), row),
        out_shape=jax.ShapeDtypeStruct((n, d), F32),
        compiler_params=pltpu.CompilerParams(dimension_semantics=("parallel",)),
        name="post",
    )(o_att, o_f, o_b, proj, proj, proj, x2, p2, wa, wd, wo, wg, wp, nd, npost, nple)


def _rope_tables(seq):
    rows = seq // GRID_W
    row = jnp.broadcast_to(jnp.arange(rows)[:, None], (rows, GRID_W)).reshape(seq)
    col = jnp.broadcast_to(jnp.arange(GRID_W)[None, :], (rows, GRID_W)).reshape(seq)
    n_freq = HEAD_DIM // 4
    inv_freq = ROPE_THETA ** (-jnp.arange(n_freq, dtype=F32) / n_freq)
    ang = jnp.concatenate([row.astype(F32)[:, None] * inv_freq,
                           col.astype(F32)[:, None] * inv_freq], axis=-1)
    cos, sin = jnp.cos(ang), jnp.sin(ang)
    return jnp.concatenate([cos, cos], axis=-1), jnp.concatenate([-sin, sin], axis=-1)


def kernel(x, p, norm_pre, w_in, q_norm, k_norm, conv_w, a_log, dt_bias, dn_norm, w_br_att,
           w_br_dn, w_out, norm_post, w_ple_proj, w_ple_gate, ple_norm):
    batch, seq, d = x.shape
    depth = w_in.shape[0]
    n = batch * seq
    assert seq % GRID_W == 0 and seq % CHUNK == 0 and d % LANES == 0
    main_perm, small_perm, offs = _column_order(d)
    cos, sin = _rope_tables(seq)
    lane_perm = _deinterleave(1)
    x2 = x.reshape(n, d)
    for i in range(depth):
        w_main = w_in[i][:, main_perm].astype(BF16)
        w_small = jnp.pad(w_in[i][:, small_perm], ((0, 0), (0, LANES - len(small_perm)))).astype(BF16)
        proj, small = _proj(x2, norm_pre[i][None, :], w_main, w_small)

        k_rot = _kprep(proj, cos, sin, k_norm[i][lane_perm][None, :], offs["ak"], seq)
        o_att = _attention(proj, k_rot, cos, sin, q_norm[i][lane_perm][None, :], offs, batch, seq)

        conv_w8 = jnp.pad(conv_w[i], ((0, 8 - CONV_K), (0, 0)))
        dqkv = _dprep(proj, conv_w8, offs["dq"], batch, seq)
        nsm = 4 * DN_HEADS
        small_t = small[:, :nsm].reshape(n // CHUNK, CHUNK, nsm).transpose(0, 2, 1)
        zeros16 = jnp.zeros((2 * DN_HEADS,), F32)
        a_vec = jnp.concatenate([zeros16, a_log[i].reshape(-1)])
        b_vec = jnp.concatenate([zeros16, dt_bias[i].reshape(-1)])
        par_col = jnp.pad(jnp.stack([a_vec, b_vec]), ((0, 6), (0, LANES - nsm)))
        par_row = jnp.pad(jnp.stack([a_vec, b_vec], axis=1), ((0, 0), (0, LANES - 2)))
        o_f, o_b = _delta(dqkv, small, small_t, par_col, par_row, batch, seq)

        x2 = _post(o_att, o_f, o_b, proj, x2, p[i].reshape(n, -1),
                   w_br_att[i].astype(BF16), w_br_dn[i].astype(BF16), w_out[i].astype(BF16),
                   w_ple_gate[i].astype(BF16), w_ple_proj[i].astype(BF16),
                   dn_norm[i][None, :], norm_post[i][None, :], ple_norm[i][None, :], offs)
    return x2.reshape(batch, seq, d)
```

```python
import functools

import numpy as np
import jax
import jax.numpy as jnp
from jax import lax
from jax.experimental import pallas as pl
from jax.experimental.pallas import tpu as pltpu

F32 = jnp.float32
BF16 = jnp.bfloat16

GRID_W = 64
ATT_HEADS = 8
ATT_KV_HEADS = 2
HEAD_DIM = 128
ROPE_THETA = 10000.0
DN_HEADS = 8
DN_DK = 128
DN_DV = 128
CONV_K = 5
CHUNK = 64
EPS = 1e-6

ATT_W = ATT_HEADS * HEAD_DIM
KV_W = ATT_KV_HEADS * HEAD_DIM
DN_KW = DN_HEADS * DN_DK
DN_VW = DN_HEADS * DN_DV
GROUP = ATT_HEADS // ATT_KV_HEADS
LANES = 128


def _deinterleave(t):
    lead = t.shape[:-1]
    t = t.reshape(lead + (-1, HEAD_DIM // 2, 2))
    return jnp.swapaxes(t, -1, -2).reshape(lead + (-1,))


def _regroup_columns(w):
    d_model = w.shape[0]
    sizes = (ATT_W, KV_W, KV_W, ATT_W, DN_KW, DN_KW, DN_VW, 2 * DN_HEADS, 2 * DN_HEADS, DN_VW,
             d_model, d_model)
    assert w.shape[1] == sum(sizes)
    aq, ak, av, az, dq, dk, dv, db, da, dz, ga, gd = jnp.split(w, np.cumsum(sizes)[:-1], axis=1)
    groups = [("aq", _deinterleave(aq)), ("az", az), ("ga", ga), ("gd", gd), ("dz", dz),
              ("dq", dq), ("dk", dk), ("dv", dv), ("ak", _deinterleave(ak)), ("av", av)]
    offsets, off = {}, 0
    for name, g in groups:
        offsets[name] = off
        off += g.shape[1]
    main = jnp.concatenate([g for _, g in groups], axis=1)
    small = jnp.pad(jnp.concatenate([db, da], axis=1), ((0, 0), (0, LANES - 4 * DN_HEADS)))
    return main, small, offsets


def _proj_kernel(x_ref, g_ref, w_ref, ws_ref, o_ref, os_ref, h_ref):
    @pl.when(pl.program_id(1) == 0)
    def _():
        x = x_ref[...]
        r = lax.rsqrt(jnp.mean(x * x, axis=-1, keepdims=True) + EPS)
        h = (x * r * g_ref[...]).astype(BF16)
        h_ref[...] = h
        os_ref[...] = jnp.dot(h, ws_ref[...], preferred_element_type=F32)

    o_ref[...] = jnp.dot(h_ref[...], w_ref[...], preferred_element_type=F32).astype(BF16)


def _proj(x2, g, w_main, w_small, tm=1024, tn=512):
    n, d = x2.shape
    main_w = w_main.shape[1]
    tm = min(tm, n)
    return pl.pallas_call(
        _proj_kernel,
        grid=(n // tm, main_w // tn),
        in_specs=[
            pl.BlockSpec((tm, d), lambda i, j: (i, 0)),
            pl.BlockSpec((1, d), lambda i, j: (0, 0)),
            pl.BlockSpec((d, tn), lambda i, j: (0, j)),
            pl.BlockSpec((d, LANES), lambda i, j: (0, 0)),
        ],
        out_specs=[
            pl.BlockSpec((tm, tn), lambda i, j: (i, j)),
            pl.BlockSpec((tm, LANES), lambda i, j: (i, 0)),
        ],
        out_shape=[
            jax.ShapeDtypeStruct((n, main_w), BF16),
            jax.ShapeDtypeStruct((n, LANES), F32),
        ],
        scratch_shapes=[pltpu.VMEM((tm, d), BF16)],
        compiler_params=pltpu.CompilerParams(dimension_semantics=("parallel", "arbitrary")),
        name="proj",
    )(x2, g, w_main, w_small)


def _norm_rope(xh, w, cos, sin):
    r = lax.rsqrt(jnp.mean(xh * xh, axis=-1, keepdims=True) + EPS)
    xn = xh * r * w
    return xn * cos + pltpu.roll(xn, HEAD_DIM // 2, axis=1) * sin


ATT_TILE = 512
VT_ROWS = HEAD_DIM + 16


def _aprep_kernel(q_ref, k_ref, v_ref, cos_ref, sin_ref, wq_ref, wk_ref, qt_ref, kr_ref, vt_ref):
    cos, sin = cos_ref[...], sin_ref[...]
    scale = HEAD_DIM ** -0.5
    tm = q_ref.shape[0]
    for h in range(ATT_KV_HEADS):
        sl = slice(h * HEAD_DIM, (h + 1) * HEAD_DIM)
        kr_ref[:, sl] = _norm_rope(k_ref[:, sl].astype(F32), wk_ref[...], cos, sin).astype(BF16)
        vt_ref[h, 0:HEAD_DIM, :] = v_ref[:, sl].astype(F32).T.astype(BF16)
        vt_ref[h, HEAD_DIM:, :] = jnp.ones((VT_ROWS - HEAD_DIM, tm), BF16)
        for g in range(GROUP):
            qs = slice((h * GROUP + g) * HEAD_DIM, (h * GROUP + g + 1) * HEAD_DIM)
            qr = _norm_rope(q_ref[:, qs].astype(F32), wq_ref[...], cos, sin) * scale
            qt_ref[h, :, g * tm:(g + 1) * tm] = qr.T.astype(BF16)


def _aprep(proj, cos, sin, wq, wk, offs, batch, seq, tm):
    n = proj.shape[0]
    t = seq // tm
    return pl.pallas_call(
        _aprep_kernel,
        grid=(batch, t),
        in_specs=[
            pl.BlockSpec((tm, ATT_W), lambda b, i: (b * t + i, offs["aq"] // ATT_W)),
            pl.BlockSpec((tm, KV_W), lambda b, i: (b * t + i, offs["ak"] // KV_W)),
            pl.BlockSpec((tm, KV_W), lambda b, i: (b * t + i, offs["av"] // KV_W)),
            pl.BlockSpec((tm, HEAD_DIM), lambda b, i: (i, 0)),
            pl.BlockSpec((tm, HEAD_DIM), lambda b, i: (i, 0)),
            pl.BlockSpec((1, HEAD_DIM), lambda b, i: (0, 0)),
            pl.BlockSpec((1, HEAD_DIM), lambda b, i: (0, 0)),
        ],
        out_specs=[
            pl.BlockSpec((None, None, ATT_KV_HEADS, HEAD_DIM, GROUP * tm),
                         lambda b, i: (b, i, 0, 0, 0)),
            pl.BlockSpec((tm, KV_W), lambda b, i: (b * t + i, 0)),
            pl.BlockSpec((None, ATT_KV_HEADS, None, VT_ROWS, tm), lambda b, i: (b, 0, i, 0, 0)),
        ],
        out_shape=[
            jax.ShapeDtypeStruct((batch, t, ATT_KV_HEADS, HEAD_DIM, GROUP * tm), BF16),
            jax.ShapeDtypeStruct((n, KV_W), BF16),
            jax.ShapeDtypeStruct((batch, ATT_KV_HEADS, t, VT_ROWS, tm), BF16),
        ],
        compiler_params=pltpu.CompilerParams(dimension_semantics=("parallel", "parallel")),
        name="aprep",
    )(proj, proj, proj, cos, sin, wq, wk)


def _attn_kernel(qt_ref, z_ref, k_ref, vt_ref, o_ref, acc_ref, s0_ref, s1_ref, p0_ref, p1_ref,
                 *, tq, tk, seq):
    cols = GROUP * tq
    n_chunks = seq // tk
    qt = qt_ref[...]

    def scores(c, s_ref):
        start = pl.multiple_of(c * tk, tk)
        s_ref[...] = jnp.dot(k_ref[pl.ds(start, tk), :], qt, preferred_element_type=F32)

    def softmax(s_ref, p_ref, m):
        s = s_ref[...]
        m_new = jnp.maximum(m, jnp.max(s, axis=0, keepdims=True))
        p_ref[...] = jnp.exp((s - m_new).astype(BF16))
        return m_new, jnp.exp(m - m_new)

    def accumulate(c, p_ref, alpha):
        acc_ref[...] = alpha * acc_ref[...] + jnp.dot(vt_ref[c], p_ref[...],
                                                      preferred_element_type=F32)

    def pair(c, carry, last):
        m, alpha_prev = carry
        m, alpha0 = softmax(s0_ref, p0_ref, m)
        scores(c + 1, s1_ref)
        accumulate(jnp.maximum(c - 1, 0), p1_ref, alpha_prev)
        m, alpha1 = softmax(s1_ref, p1_ref, m)
        if not last:
            scores(c + 2, s0_ref)
        accumulate(c, p0_ref, alpha0)
        return m, alpha1

    acc_ref[...] = jnp.zeros(acc_ref.shape, F32)
    p1_ref[...] = jnp.zeros(p1_ref.shape, BF16)
    scores(0, s0_ref)
    carry = (jnp.full((1, cols), -jnp.inf, F32), jnp.ones((1, cols), F32))
    carry = lax.fori_loop(0, n_chunks // 2 - 1, lambda j, cr: pair(2 * j, cr, False), carry)
    _, alpha = pair(n_chunks - 2, carry, True)
    accumulate(n_chunks - 1, p1_ref, alpha)
    acc = acc_ref[...]
    o_t = acc[:HEAD_DIM] / acc[HEAD_DIM:HEAD_DIM + 1]
    for g in range(GROUP):
        sl = slice(g * HEAD_DIM, (g + 1) * HEAD_DIM)
        z = z_ref[:, sl].astype(F32)
        o_ref[:, sl] = (o_t[:, g * tq:(g + 1) * tq].T * (z * jax.nn.sigmoid(z))).astype(BF16)


def _attention(proj, qt, k_rot, vt, offs, batch, seq, tq, tk):
    n = proj.shape[0]
    gw = GROUP * HEAD_DIM
    t = seq // tq
    cols = GROUP * tq
    assert (seq // tk) % 2 == 0
    kern = functools.partial(_attn_kernel, tq=tq, tk=tk, seq=seq)
    return pl.pallas_call(
        kern,
        grid=(batch, ATT_KV_HEADS, t),
        in_specs=[
            pl.BlockSpec((None, None, None, HEAD_DIM, cols), lambda b, h, i: (b, i, h, 0, 0)),
            pl.BlockSpec((tq, gw), lambda b, h, i: (b * t + i, offs["az"] // gw + h)),
            pl.BlockSpec((seq, HEAD_DIM), lambda b, h, i: (b, h)),
            pl.BlockSpec((None, None, seq // tk, VT_ROWS, tk), lambda b, h, i: (b, h, 0, 0, 0)),
        ],
        out_specs=pl.BlockSpec((tq, gw), lambda b, h, i: (b * t + i, h)),
        out_shape=jax.ShapeDtypeStruct((n, ATT_W), BF16),
        scratch_shapes=[pltpu.VMEM((VT_ROWS, cols), F32),
                        pltpu.VMEM((tk, cols), F32), pltpu.VMEM((tk, cols), F32),
                        pltpu.VMEM((tk, cols), BF16), pltpu.VMEM((tk, cols), BF16)],
        compiler_params=pltpu.CompilerParams(
            dimension_semantics=("parallel", "parallel", "arbitrary")),
        name="attn",
    )(qt, proj, k_rot, vt)


def _dprep_kernel(x_ref, w_ref, o_ref, xs_ref, *, seq, rows, q_blocks, k_blocks):
    pad = 8
    cb = x_ref.shape[1]
    j = pl.program_id(1)
    xs_ref[0:pad, :] = jnp.zeros((pad, cb), F32)
    xs_ref[pad + seq:pad + seq + pad, :] = jnp.zeros((pad, cb), F32)
    xs_ref[pad:pad + seq, :] = x_ref[...].astype(F32)
    w = w_ref[...]
    is_qk = j < (q_blocks + k_blocks)
    post = jnp.where(j < q_blocks, DN_DK ** -0.5, 1.0).astype(F32)
    for r in range(seq // rows):
        base = pad + r * rows - CONV_K // 2
        acc = jnp.zeros((rows, cb), F32)
        for t in range(CONV_K):
            acc = acc + xs_ref[base + t:base + t + rows, :] * w[t:t + 1, :]
        y = acc * jax.nn.sigmoid(acc)
        for h in range(cb // DN_DK):
            sl = slice(h * DN_DK, (h + 1) * DN_DK)
            yh = y[:, sl]
            inv = lax.rsqrt(jnp.sum(yh * yh, axis=-1, keepdims=True) + EPS) * post
            yh = yh * jnp.where(is_qk, inv, 1.0)
            o_ref[r * rows:(r + 1) * rows, sl] = yh.astype(BF16)


def _dprep(proj, conv_w8, off_dq, batch, seq, cb=256, rows=512):
    n = proj.shape[0]
    width = 2 * DN_KW + DN_VW
    kern = functools.partial(_dprep_kernel, seq=seq, rows=rows, q_blocks=DN_KW // cb,
                             k_blocks=DN_KW // cb)
    return pl.pallas_call(
        kern,
        grid=(batch, width // cb),
        in_specs=[
            pl.BlockSpec((seq, cb), lambda b, j: (b, off_dq // cb + j)),
            pl.BlockSpec((8, cb), lambda b, j: (0, j)),
        ],
        out_specs=pl.BlockSpec((seq, cb), lambda b, j: (b, j)),
        out_shape=jax.ShapeDtypeStruct((n, width), BF16),
        scratch_shapes=[pltpu.VMEM((seq + 16, cb), F32)],
        compiler_params=pltpu.CompilerParams(dimension_semantics=("parallel", "parallel")),
        name="dprep",
    )(proj, conv_w8)


HEADS_PER_TILE = 4
CAT_W = HEADS_PER_TILE * CHUNK
N_TILES = DN_HEADS // HEADS_PER_TILE


def _split3(x):
    p1 = x.astype(BF16)
    r1 = x - p1.astype(F32)
    p2 = r1.astype(BF16)
    p3 = (r1 - p2.astype(F32)).astype(BF16)
    return p1, p2, p3


def _softplus(x):
    return jnp.maximum(x, 0.0) + jnp.log(1.0 + jnp.exp(-jnp.abs(x)))


def _dot(a, b):
    return jnp.dot(a, b, preferred_element_type=F32)


def _dchunk_kernel(x_ref, c_ref, r_ref, pc_ref, pr_ref, a1_ref, a2_ref, au_ref, ae_ref, *, cpb):
    C, W, T = CHUNK, CAT_W, HEADS_PER_TILE
    rows = cpb * C

    def cum_matrix(n, lower):
        i = lax.broadcasted_iota(jnp.int32, (n, n), 0)
        j = lax.broadcasted_iota(jnp.int32, (n, n), 1)
        same = (i >> 6) == (j >> 6)
        return (same & ((i >= j) if lower else (i <= j))).astype(BF16)

    col_lower, col_upper = cum_matrix(rows, True), cum_matrix(rows, False)
    row_lower, row_upper = cum_matrix(W, True), cum_matrix(W, False)

    blk = c_ref[...]
    beta_all = jax.nn.sigmoid(blk)
    g_parts = _split3(-jnp.exp(pc_ref[0:1, :]) * _softplus(blk + pc_ref[1:2, :]))
    gcol = (sum(_dot(col_lower, p) for p in g_parts),
            sum(_dot(col_upper, p) for p in g_parts))
    blk_r = r_ref[...]
    g_r = (-jnp.exp(pr_ref[0][None]) * _softplus(blk_r + pr_ref[1][None])).reshape(cpb * 8, W)
    gr_parts = _split3(g_r)
    grow = (sum(_dot(p, row_upper) for p in gr_parts),
            sum(_dot(p, row_lower) for p in gr_parts))

    ri = lax.broadcasted_iota(jnp.int32, (C, W), 0)
    li = lax.broadcasted_iota(jnp.int32, (C, W), 1)
    ci = li & (C - 1)
    lb = li >> 6
    incl = (ri >= ci, ri <= ci)
    offdiag = ri != ci
    eye = (ri == ci).astype(F32)
    same = {s: (ri >> lg) == (ci >> lg) for lg, s in ((1, 2), (2, 4), (3, 8), (4, 16), (5, 32))}
    level_masks = [same[4] & ~same[2], same[8] & ~same[4], same[16] & ~same[8],
                   same[32] & ~same[16], ~same[32]]
    tile_sel = [lb == t for t in range(T)]
    head_of_lane = lax.broadcasted_iota(jnp.int32, (C, T * DN_DK), 1) >> 7

    def block_diag(y):
        zero = jnp.zeros_like(y)
        return jnp.concatenate([jnp.where(tile_sel[t], y, zero) for t in range(T)], axis=0)

    def cat_bcast(cols):
        out = jnp.broadcast_to(cols[T - 1], (C, W))
        for t in range(T - 2, -1, -1):
            out = jnp.where(tile_sel[t], jnp.broadcast_to(cols[t], (C, W)), out)
        return out

    pairs = [(c, t) for c in range(cpb) for t in range(N_TILES)]
    units = [(c, t, d) for c in range(cpb) for t in range(N_TILES) for d in range(2)]

    qk, kk, q4s, k4s, v4s = {}, {}, {}, {}, {}
    for c, t in pairs:
        rs = slice(c * C, (c + 1) * C)
        q4 = x_ref[rs, t * T * DN_DK:(t + 1) * T * DN_DK]
        k4 = x_ref[rs, DN_KW + t * T * DN_DK:DN_KW + (t + 1) * T * DN_DK]
        v4 = x_ref[rs, 2 * DN_KW + t * T * DN_DV:2 * DN_KW + (t + 1) * T * DN_DV]
        zero = jnp.zeros_like(k4)
        k_bd = jnp.concatenate([jnp.where(head_of_lane == h, k4, zero) for h in range(T)], axis=0)
        r = lax.dot_general(jnp.concatenate([q4, k4], axis=0), k_bd, (((1,), (1,)), ((), ())),
                            preferred_element_type=F32)
        qk[c, t], kk[c, t] = r[:C], r[C:]
        q4s[c, t], k4s[c, t], v4s[c, t] = q4, k4, v4

    lmat, amat, xinv = {}, {}, {}
    for c, t, d in units:
        rs = slice(c * C, (c + 1) * C)
        base = d * DN_HEADS + t * T
        gc = cat_bcast([gcol[d][rs, 2 * DN_HEADS + base + h:2 * DN_HEADS + base + h + 1]
                        for h in range(T)])
        beta = cat_bcast([beta_all[rs, base + h:base + h + 1] for h in range(T)])
        row = c * 8 + d * N_TILES + t
        gr = jnp.broadcast_to(grow[d][row:row + 1, :], (C, W))
        decay = jnp.exp(jnp.where(incl[d], gc - gr, -1e30))
        l = jnp.where(offdiag, beta * kk[c, t] * decay, 0.0)
        lmat[c, t, d] = l
        amat[c, t, d] = qk[c, t] * decay
        xinv[c, t, d] = eye - jnp.where(same[2], l, 0.0)

    for msk in level_masks:
        ys, xbs = {}, {}
        for u in units:
            e = jnp.where(msk, lmat[u], 0.0).astype(BF16)
            xbs[u] = xinv[u].astype(BF16)
            ys[u] = _dot(e, block_diag(xbs[u]))
        for u in units:
            xinv[u] = xinv[u] - _dot(xbs[u], block_diag(ys[u].astype(BF16)))

    for c, t, d in units:
        rs = slice(c * C, (c + 1) * C)
        kd_rows, rhs_rows = [], []
        for h in range(T):
            head = t * T + h
            col = d * DN_HEADS + head
            hs = slice(h * DN_DK, (h + 1) * DN_DK)
            out_s = slice(head * DN_DK, (head + 1) * DN_DK)
            gc = jnp.broadcast_to(gcol[d][rs, 2 * DN_HEADS + col:2 * DN_HEADS + col + 1],
                                  (C, DN_DK))
            beta = jnp.broadcast_to(beta_all[rs, col:col + 1], (C, DN_DK))
            g_last = gc[C - 1:C, :] if d == 0 else gc[0:1, :]
            eg = jnp.exp(gc)
            kf = k4s[c, t][:, hs].astype(F32)
            rhs = jnp.concatenate([kf * (beta * eg), v4s[c, t][:, hs].astype(F32) * beta],
                                  axis=1).astype(BF16)
            zero = jnp.zeros_like(rhs)
            rhs_rows.append(jnp.concatenate([rhs if hh == h else zero for hh in range(T)], axis=1))
            kd_rows.append(kf * jnp.exp(g_last - gc))
            a1_ref[c, d, C:2 * C, out_s] = (q4s[c, t][:, hs].astype(F32) * eg).astype(BF16)
            ae_ref[c, d, :, out_s] = jnp.broadcast_to(jnp.exp(g_last), (8, DN_DK))
        wu = _dot(xinv[c, t, d].astype(BF16), jnp.concatenate(rhs_rows, axis=0))
        for h in range(T):
            head = t * T + h
            out_s = slice(head * DN_DK, (head + 1) * DN_DK)
            a1_ref[c, d, 0:C, out_s] = wu[:, 2 * h * DN_DK:(2 * h + 1) * DN_DK].astype(BF16)
            au_ref[c, d, :, out_s] = wu[:, (2 * h + 1) * DN_DK:(2 * h + 2) * DN_DK].astype(BF16)
        cat_s = slice(t * W, (t + 1) * W)
        a2_ref[c, d, 0:C, cat_s] = amat[c, t, d].astype(BF16)
        a2_ref[c, d, C:C + DN_DK, cat_s] = jnp.concatenate(kd_rows, axis=0).T.astype(BF16)


def _dchunk(dqkv, small, small_r, par_col, par_row, batch, seq, cpb=4):
    nc = seq // CHUNK
    steps = nc // cpb
    width = dqkv.shape[1]
    kern = functools.partial(_dchunk_kernel, cpb=cpb)
    out5 = lambda b, i: (b, i, 0, 0, 0)
    return pl.pallas_call(
        kern,
        grid=(batch, steps),
        in_specs=[
            pl.BlockSpec((cpb * CHUNK, width), lambda b, i: (b * steps + i, 0)),
            pl.BlockSpec((cpb * CHUNK, LANES), lambda b, i: (b * steps + i, 0)),
            pl.BlockSpec((cpb, 8, CAT_W), lambda b, i: (b * steps + i, 0, 0)),
            pl.BlockSpec((8, LANES), lambda b, i: (0, 0)),
            pl.BlockSpec((2, 8, CAT_W), lambda b, i: (0, 0, 0)),
        ],
        out_specs=[
            pl.BlockSpec((None, cpb, 2, 2 * CHUNK, DN_KW), out5),
            pl.BlockSpec((None, cpb, 2, CHUNK + DN_DK, DN_HEADS * CHUNK), out5),
            pl.BlockSpec((None, cpb, 2, CHUNK, DN_VW), out5),
            pl.BlockSpec((None, cpb, 2, 8, DN_VW), out5),
        ],
        out_shape=[
            jax.ShapeDtypeStruct((batch, nc, 2, 2 * CHUNK, DN_KW), BF16),
            jax.ShapeDtypeStruct((batch, nc, 2, CHUNK + DN_DK, DN_HEADS * CHUNK), BF16),
            jax.ShapeDtypeStruct((batch, nc, 2, CHUNK, DN_VW), BF16),
            jax.ShapeDtypeStruct((batch, nc, 2, 8, DN_VW), F32),
        ],
        compiler_params=pltpu.CompilerParams(dimension_semantics=("parallel", "parallel")),
        name="dchunk",
    )(dqkv, small, small_r, par_col, par_row)


def _dscan_kernel(a1f_ref, a1b_ref, a2f_ref, a2b_ref, auf_ref, aub_ref, aef_ref, aeb_ref,
                  of_ref, ob_ref, s_ref, *, batch):
    C = CHUNK
    PW = 2 * DN_DV
    n_pairs = DN_HEADS // 2

    @pl.when(pl.program_id(0) == 0)
    def _():
        s_ref[...] = jnp.zeros(s_ref.shape, F32)

    first = lax.broadcasted_iota(jnp.int32, (DN_DK, PW), 1) < DN_DV
    first_c = lax.broadcasted_iota(jnp.int32, (C, PW), 1) < DN_DV
    refs = ((a1f_ref, a2f_ref, auf_ref, aef_ref, of_ref), (a1b_ref, a2b_ref, aub_ref, aeb_ref, ob_ref))
    units = [(b, d, p) for b in range(batch) for d in range(2) for p in range(n_pairs)]
    sidx = lambda u: (u[0] * 2 + u[1]) * n_pairs + u[2]

    r1, r2, states = {}, {}, {}
    for u in units:
        b, d, p = u
        states[u] = s_ref[sidx(u)]
        r1[u] = _dot(refs[d][0][b, :, p * PW:(p + 1) * PW], states[u].astype(BF16))
    for u in units:
        b, d, p = u
        vb = (refs[d][2][b, :, p * PW:(p + 1) * PW].astype(F32) - r1[u][:C]).astype(BF16)
        zero = jnp.zeros_like(vb)
        v_bd = jnp.concatenate([jnp.where(first_c, vb, zero), jnp.where(first_c, zero, vb)], axis=0)
        r2[u] = _dot(refs[d][1][b, :, p * 2 * C:(p + 1) * 2 * C], v_bd)
    for u in units:
        b, d, p = u
        refs[d][4][b, :, p * PW:(p + 1) * PW] = (r1[u][C:] + r2[u][:C]).astype(BF16)
        e = refs[d][3][b, 0:1, p * PW:(p + 1) * PW]
        upd = r2[u][C:]
        s = states[u]
        s_ref[sidx(u)] = jnp.concatenate(
            [s[:DN_DK] * e + jnp.where(first, upd, 0.0), s[DN_DK:] * e + jnp.where(first, 0.0, upd)],
            axis=0)


def _dscan(a1, a2, au, ae, batch, seq):
    nc = seq // CHUNK
    kern = functools.partial(_dscan_kernel, batch=batch)
    fwd = lambda i: (0, i, 0, 0, 0)
    bwd = lambda i: (0, nc - 1 - i, 1, 0, 0)

    def spec(arr, imap):
        return pl.BlockSpec((batch, None, None) + arr.shape[3:], imap)

    o_spec = lambda imap: pl.BlockSpec((batch, None, CHUNK, DN_VW), imap)
    o_shape = jax.ShapeDtypeStruct((batch, nc, CHUNK, DN_VW), BF16)
    return pl.pallas_call(
        kern,
        grid=(nc,),
        in_specs=[spec(a1, fwd), spec(a1, bwd), spec(a2, fwd), spec(a2, bwd),
                  spec(au, fwd), spec(au, bwd), spec(ae, fwd), spec(ae, bwd)],
        out_specs=[o_spec(lambda i: (0, i, 0, 0)), o_spec(lambda i: (0, nc - 1 - i, 0, 0))],
        out_shape=[o_shape, o_shape],
        scratch_shapes=[pltpu.VMEM((batch * 2 * (DN_HEADS // 2), 2 * DN_DK, 2 * DN_DV), F32)],
        compiler_params=pltpu.CompilerParams(dimension_semantics=("arbitrary",)),
        name="dscan",
    )(a1, a1, a2, a2, au, au, ae, ae)


def _post_kernel(oa_ref, of_ref, ob_ref, dz_ref, ga_ref, gd_ref, x_ref, p_ref,
                 wa_ref, wd_ref, wo_ref, wg_ref, wp_ref, nd_ref, npost_ref, nple_ref, o_ref):
    def rms(t, w):
        return t * lax.rsqrt(jnp.mean(t * t, axis=-1, keepdims=True) + EPS) * w

    nd = nd_ref[...]
    parts = []
    for h in range(DN_HEADS):
        sl = slice(h * DN_DV, (h + 1) * DN_DV)
        od = of_ref[:, sl].astype(F32) + ob_ref[:, sl].astype(F32)
        z = dz_ref[:, sl].astype(F32)
        parts.append((rms(od, nd) * (z * jax.nn.sigmoid(z))).astype(BF16))
    y_dn = jnp.dot(jnp.concatenate(parts, axis=1), wd_ref[...], preferred_element_type=F32)
    y_att = jnp.dot(oa_ref[...], wa_ref[...], preferred_element_type=F32)
    merged = (jax.nn.sigmoid(ga_ref[...].astype(F32)) * y_att
              + jax.nn.sigmoid(gd_ref[...].astype(F32)) * y_dn)
    mix = jnp.dot(merged.astype(BF16), wo_ref[...], preferred_element_type=F32)
    x1 = x_ref[...] + rms(mix, npost_ref[...])
    gate = jax.nn.sigmoid(jnp.dot(x1.astype(BF16), wg_ref[...], preferred_element_type=F32))
    e = jnp.dot(p_ref[...].astype(BF16), wp_ref[...], preferred_element_type=F32)
    o_ref[...] = x1 + rms(gate * e, nple_ref[...])


def _post(o_att, o_f, o_b, proj, x2, p2, wa, wd, wo, wg, wp, nd, npost, nple, offs, tm=512):
    n, d = x2.shape
    ple = p2.shape[1]
    row = lambda i: (i, 0)
    const = lambda i: (0, 0)
    return pl.pallas_call(
        _post_kernel,
        grid=(n // tm,),
        in_specs=[
            pl.BlockSpec((tm, ATT_W), row),
            pl.BlockSpec((tm, DN_VW), row),
            pl.BlockSpec((tm, DN_VW), row),
            pl.BlockSpec((tm, DN_VW), lambda i: (i, offs["dz"] // DN_VW)),
            pl.BlockSpec((tm, d), lambda i: (i, offs["ga"] // d)),
            pl.BlockSpec((tm, d), lambda i: (i, offs["gd"] // d)),
            pl.BlockSpec((tm, d), row),
            pl.BlockSpec((tm, ple), row),
            pl.BlockSpec((ATT_W, d), const),
            pl.BlockSpec((DN_VW, d), const),
            pl.BlockSpec((d, d), const),
            pl.BlockSpec((d, d), const),
            pl.BlockSpec((ple, d), const),
            pl.BlockSpec((1, DN_DV), const),
            pl.BlockSpec((1, d), const),
            pl.BlockSpec((1, d), const),
        ],
        out_specs=pl.BlockSpec((tm, d), row),
        out_shape=jax.ShapeDtypeStruct((n, d), F32),
        compiler_params=pltpu.CompilerParams(dimension_semantics=("parallel",)),
        name="post",
    )(o_att, o_f, o_b, proj, proj, proj, x2, p2, wa, wd, wo, wg, wp, nd, npost, nple)


def _rope_tables(seq):
    rows = seq // GRID_W
    row = jnp.broadcast_to(jnp.arange(rows)[:, None], (rows, GRID_W)).reshape(seq)
    col = jnp.broadcast_to(jnp.arange(GRID_W)[None, :], (rows, GRID_W)).reshape(seq)
    n_freq = HEAD_DIM // 4
    inv_freq = ROPE_THETA ** (-jnp.arange(n_freq, dtype=F32) / n_freq)
    ang = jnp.concatenate([row.astype(F32)[:, None] * inv_freq,
                           col.astype(F32)[:, None] * inv_freq], axis=-1)
    cos, sin = jnp.cos(ang), jnp.sin(ang)
    return jnp.concatenate([cos, cos], axis=-1), jnp.concatenate([-sin, sin], axis=-1)


def kernel(x, p, norm_pre, w_in, q_norm, k_norm, conv_w, a_log, dt_bias, dn_norm, w_br_att,
           w_br_dn, w_out, norm_post, w_ple_proj, w_ple_gate, ple_norm):
    batch, seq, d = x.shape
    depth = w_in.shape[0]
    n = batch * seq
    assert seq % GRID_W == 0 and seq % CHUNK == 0 and d % LANES == 0
    cos, sin = _rope_tables(seq)
    x2 = x.reshape(n, d)
    for i in range(depth):
        w_main, w_small, offs = _regroup_columns(w_in[i].astype(BF16))
        proj, small = _proj(x2, norm_pre[i][None, :], w_main, w_small)

        att_tile = min(ATT_TILE, seq // 2)
        qt, k_rot, vt = _aprep(proj, cos, sin, _deinterleave(q_norm[i])[None, :],
                               _deinterleave(k_norm[i])[None, :], offs, batch, seq, att_tile)
        o_att = _attention(proj, qt, k_rot, vt, offs, batch, seq, att_tile, att_tile)

        conv_w8 = jnp.pad(conv_w[i], ((0, 8 - CONV_K), (0, 0)))
        dqkv = _dprep(proj, conv_w8, offs["dq"], batch, seq)
        nsm = 4 * DN_HEADS
        nck = n // CHUNK
        small_r = small[:, 2 * DN_HEADS:nsm].reshape(nck, CHUNK, 2, N_TILES, HEADS_PER_TILE)
        small_r = small_r.transpose(0, 2, 3, 4, 1).reshape(nck, 2 * N_TILES, CAT_W)
        small_r = jnp.pad(small_r, ((0, 0), (0, 8 - 2 * N_TILES), (0, 0)))
        par = jnp.stack([a_log[i].reshape(-1), dt_bias[i].reshape(-1)])
        par_col = jnp.pad(jnp.concatenate([jnp.zeros_like(par), par], axis=1),
                          ((0, 6), (0, LANES - nsm)))
        par_row = jnp.repeat(par.reshape(2, 2 * N_TILES, HEADS_PER_TILE), CHUNK, axis=2)
        par_row = jnp.pad(par_row, ((0, 0), (0, 8 - 2 * N_TILES), (0, 0)))
        a1, a2, au, ae = _dchunk(dqkv, small, small_r, par_col, par_row, batch, seq)
        o_f, o_b = _dscan(a1, a2, au, ae, batch, seq)
        o_f, o_b = o_f.reshape(n, DN_VW), o_b.reshape(n, DN_VW)

        x2 = _post(o_att, o_f, o_b, proj, x2, p[i].reshape(n, -1),
                   w_br_att[i].astype(BF16), w_br_dn[i].astype(BF16), w_out[i].astype(BF16),
                   w_ple_gate[i].astype(BF16), w_ple_proj[i].astype(BF16),
                   dn_norm[i][None, :], norm_post[i][None, :], ple_norm[i][None, :], offs)
    return x2.reshape(batch, seq, d)
```

```python
import functools

import numpy as np
import jax
import jax.numpy as jnp
from jax import lax
from jax.experimental import pallas as pl
from jax.experimental.pallas import tpu as pltpu

F32 = jnp.float32
BF16 = jnp.bfloat16

GRID_W = 64
ATT_HEADS = 8
ATT_KV_HEADS = 2
HEAD_DIM = 128
ROPE_THETA = 10000.0
DN_HEADS = 8
DN_DK = 128
DN_DV = 128
CONV_K = 5
CHUNK = 64
EPS = 1e-6

ATT_W = ATT_HEADS * HEAD_DIM
KV_W = ATT_KV_HEADS * HEAD_DIM
DN_KW = DN_HEADS * DN_DK
DN_VW = DN_HEADS * DN_DV
GROUP = ATT_HEADS // ATT_KV_HEADS
LANES = 128


def _deinterleave(t):
    lead = t.shape[:-1]
    t = t.reshape(lead + (-1, HEAD_DIM // 2, 2))
    return jnp.swapaxes(t, -1, -2).reshape(lead + (-1,))


def _regroup_columns(w):
    d_model = w.shape[0]
    sizes = (ATT_W, KV_W, KV_W, ATT_W, DN_KW, DN_KW, DN_VW, 2 * DN_HEADS, 2 * DN_HEADS, DN_VW,
             d_model, d_model)
    assert w.shape[1] == sum(sizes)
    aq, ak, av, az, dq, dk, dv, db, da, dz, ga, gd = jnp.split(w, np.cumsum(sizes)[:-1], axis=1)
    groups = [("aq", _deinterleave(aq)), ("az", az), ("ga", ga), ("gd", gd), ("dz", dz),
              ("dq", dq), ("dk", dk), ("dv", dv), ("ak", _deinterleave(ak)), ("av", av)]
    offsets, off = {}, 0
    for name, g in groups:
        offsets[name] = off
        off += g.shape[1]
    main = jnp.concatenate([g for _, g in groups], axis=1)
    small = jnp.pad(jnp.concatenate([db, da], axis=1), ((0, 0), (0, LANES - 4 * DN_HEADS)))
    return main, small, offsets


def _proj_kernel(x_ref, g_ref, w_ref, ws_ref, o_ref, os_ref, h_ref):
    @pl.when(pl.program_id(1) == 0)
    def _():
        x = x_ref[...]
        r = lax.rsqrt(jnp.mean(x * x, axis=-1, keepdims=True) + EPS)
        h = (x * r * g_ref[...]).astype(BF16)
        h_ref[...] = h
        os_ref[...] = jnp.dot(h, ws_ref[...], preferred_element_type=F32)

    o_ref[...] = jnp.dot(h_ref[...], w_ref[...], preferred_element_type=F32).astype(BF16)


def _proj(x2, g, w_main, w_small, tm=512, tn=4352):
    n, d = x2.shape
    main_w = w_main.shape[1]
    tm = min(tm, n)
    return pl.pallas_call(
        _proj_kernel,
        grid=(n // tm, main_w // tn),
        in_specs=[
            pl.BlockSpec((tm, d), lambda i, j: (i, 0)),
            pl.BlockSpec((1, d), lambda i, j: (0, 0)),
            pl.BlockSpec((d, tn), lambda i, j: (0, j)),
            pl.BlockSpec((d, LANES), lambda i, j: (0, 0)),
        ],
        out_specs=[
            pl.BlockSpec((tm, tn), lambda i, j: (i, j)),
            pl.BlockSpec((tm, LANES), lambda i, j: (i, 0)),
        ],
        out_shape=[
            jax.ShapeDtypeStruct((n, main_w), BF16),
            jax.ShapeDtypeStruct((n, LANES), F32),
        ],
        scratch_shapes=[pltpu.VMEM((tm, d), BF16)],
        compiler_params=pltpu.CompilerParams(dimension_semantics=("parallel", "arbitrary")),
        name="proj",
    )(x2, g, w_main, w_small)


def _norm_rope(xh, w, cos, sin):
    r = lax.rsqrt(jnp.mean(xh * xh, axis=-1, keepdims=True) + EPS)
    xn = xh * r * w
    return xn * cos + pltpu.roll(xn, HEAD_DIM // 2, axis=1) * sin


ATT_TQ = 1024
ATT_TK = 512
VT_ROWS = HEAD_DIM + 16


def _aprep_kernel(q_ref, k_ref, v_ref, cos_ref, sin_ref, wq_ref, wk_ref, qt_ref, kr_ref, vt_ref):
    cos, sin = cos_ref[...], sin_ref[...]
    scale = HEAD_DIM ** -0.5
    tm = q_ref.shape[0]
    tk = vt_ref.shape[-1]
    for h in range(ATT_KV_HEADS):
        sl = slice(h * HEAD_DIM, (h + 1) * HEAD_DIM)
        kr_ref[:, sl] = _norm_rope(k_ref[:, sl].astype(F32), wk_ref[...], cos, sin).astype(BF16)
        for c in range(tm // tk):
            vt_ref[h, c, 0:HEAD_DIM, :] = v_ref[c * tk:(c + 1) * tk, sl].astype(F32).T.astype(BF16)
            vt_ref[h, c, HEAD_DIM:, :] = jnp.ones((VT_ROWS - HEAD_DIM, tk), BF16)
        for g in range(GROUP):
            qs = slice((h * GROUP + g) * HEAD_DIM, (h * GROUP + g + 1) * HEAD_DIM)
            qr = _norm_rope(q_ref[:, qs].astype(F32), wq_ref[...], cos, sin) * scale
            qt_ref[h, :, g * tm:(g + 1) * tm] = qr.T.astype(BF16)


def _aprep(proj, cos, sin, wq, wk, offs, batch, seq, tm, tk):
    n = proj.shape[0]
    t = seq // tm
    cpt = tm // tk
    return pl.pallas_call(
        _aprep_kernel,
        grid=(batch, t),
        in_specs=[
            pl.BlockSpec((tm, ATT_W), lambda b, i: (b * t + i, offs["aq"] // ATT_W)),
            pl.BlockSpec((tm, KV_W), lambda b, i: (b * t + i, offs["ak"] // KV_W)),
            pl.BlockSpec((tm, KV_W), lambda b, i: (b * t + i, offs["av"] // KV_W)),
            pl.BlockSpec((tm, HEAD_DIM), lambda b, i: (i, 0)),
            pl.BlockSpec((tm, HEAD_DIM), lambda b, i: (i, 0)),
            pl.BlockSpec((1, HEAD_DIM), lambda b, i: (0, 0)),
            pl.BlockSpec((1, HEAD_DIM), lambda b, i: (0, 0)),
        ],
        out_specs=[
            pl.BlockSpec((None, None, ATT_KV_HEADS, HEAD_DIM, GROUP * tm),
                         lambda b, i: (b, i, 0, 0, 0)),
            pl.BlockSpec((tm, KV_W), lambda b, i: (b * t + i, 0)),
            pl.BlockSpec((None, ATT_KV_HEADS, cpt, VT_ROWS, tk), lambda b, i: (b, 0, i, 0, 0)),
        ],
        out_shape=[
            jax.ShapeDtypeStruct((batch, t, ATT_KV_HEADS, HEAD_DIM, GROUP * tm), BF16),
            jax.ShapeDtypeStruct((n, KV_W), BF16),
            jax.ShapeDtypeStruct((batch, ATT_KV_HEADS, seq // tk, VT_ROWS, tk), BF16),
        ],
        compiler_params=pltpu.CompilerParams(dimension_semantics=("parallel", "parallel")),
        name="aprep",
    )(proj, proj, proj, cos, sin, wq, wk)


def _attn_kernel(qt_ref, z_ref, k_ref, vt_ref, o_ref, acc_ref, s0_ref, s1_ref, p0_ref, p1_ref,
                 *, tq, tk, seq):
    cols = GROUP * tq
    n_chunks = seq // tk
    qt = qt_ref[...]

    def scores(c, s_ref):
        start = pl.multiple_of(c * tk, tk)
        s_ref[...] = jnp.dot(k_ref[pl.ds(start, tk), :], qt, preferred_element_type=F32)

    def softmax(s_ref, p_ref, m):
        s = s_ref[...]
        m_new = jnp.maximum(m, jnp.max(s, axis=0, keepdims=True))
        p_ref[...] = jnp.exp((s - m_new).astype(BF16))
        return m_new, jnp.exp(m - m_new)

    def accumulate(c, p_ref, alpha):
        acc_ref[...] = alpha * acc_ref[...] + jnp.dot(vt_ref[c], p_ref[...],
                                                      preferred_element_type=F32)

    def pair(c, carry, last):
        m, alpha_prev = carry
        m, alpha0 = softmax(s0_ref, p0_ref, m)
        scores(c + 1, s1_ref)
        accumulate(jnp.maximum(c - 1, 0), p1_ref, alpha_prev)
        m, alpha1 = softmax(s1_ref, p1_ref, m)
        if not last:
            scores(c + 2, s0_ref)
        accumulate(c, p0_ref, alpha0)
        return m, alpha1

    acc_ref[...] = jnp.zeros(acc_ref.shape, F32)
    p1_ref[...] = jnp.zeros(p1_ref.shape, BF16)
    scores(0, s0_ref)
    carry = (jnp.full((1, cols), -jnp.inf, F32), jnp.ones((1, cols), F32))
    carry = lax.fori_loop(0, n_chunks // 2 - 1, lambda j, cr: pair(2 * j, cr, False), carry)
    _, alpha = pair(n_chunks - 2, carry, True)
    accumulate(n_chunks - 1, p1_ref, alpha)
    acc = acc_ref[...]
    o_t = acc[:HEAD_DIM] / acc[HEAD_DIM:HEAD_DIM + 1]
    for g in range(GROUP):
        sl = slice(g * HEAD_DIM, (g + 1) * HEAD_DIM)
        z = z_ref[:, sl].astype(F32)
        o_ref[:, sl] = (o_t[:, g * tq:(g + 1) * tq].T * (z * jax.nn.sigmoid(z))).astype(BF16)


def _attention(proj, qt, k_rot, vt, offs, batch, seq, tq, tk):
    n = proj.shape[0]
    gw = GROUP * HEAD_DIM
    t = seq // tq
    cols = GROUP * tq
    assert (seq // tk) % 2 == 0
    kern = functools.partial(_attn_kernel, tq=tq, tk=tk, seq=seq)
    return pl.pallas_call(
        kern,
        grid=(batch, ATT_KV_HEADS, t),
        in_specs=[
            pl.BlockSpec((None, None, None, HEAD_DIM, cols), lambda b, h, i: (b, i, h, 0, 0)),
            pl.BlockSpec((tq, gw), lambda b, h, i: (b * t + i, offs["az"] // gw + h)),
            pl.BlockSpec((seq, HEAD_DIM), lambda b, h, i: (b, h)),
            pl.BlockSpec((None, None, seq // tk, VT_ROWS, tk), lambda b, h, i: (b, h, 0, 0, 0)),
        ],
        out_specs=pl.BlockSpec((tq, gw), lambda b, h, i: (b * t + i, h)),
        out_shape=jax.ShapeDtypeStruct((n, ATT_W), BF16),
        scratch_shapes=[pltpu.VMEM((VT_ROWS, cols), F32),
                        pltpu.VMEM((tk, cols), F32), pltpu.VMEM((tk, cols), F32),
                        pltpu.VMEM((tk, cols), BF16), pltpu.VMEM((tk, cols), BF16)],
        compiler_params=pltpu.CompilerParams(
            dimension_semantics=("parallel", "parallel", "arbitrary")),
        name="attn",
    )(qt, proj, k_rot, vt)


def _dprep_kernel(x_ref, w_ref, o_ref, xs_ref, *, seq, rows, q_blocks, k_blocks):
    pad = 8
    cb = x_ref.shape[1]
    j = pl.program_id(1)
    xs_ref[0:pad, :] = jnp.zeros((pad, cb), F32)
    xs_ref[pad + seq:pad + seq + pad, :] = jnp.zeros((pad, cb), F32)
    xs_ref[pad:pad + seq, :] = x_ref[...].astype(F32)
    w = w_ref[...]
    is_qk = j < (q_blocks + k_blocks)
    post = jnp.where(j < q_blocks, DN_DK ** -0.5, 1.0).astype(F32)
    half = CONV_K // 2
    for r in range(seq // rows):
        window = xs_ref[r * rows:r * rows + rows + 2 * pad, :]
        acc = window[pad:pad + rows] * w[half:half + 1, :]
        for t in range(CONV_K):
            if t != half:
                shifted = pltpu.roll(window, (half - t) % (rows + 2 * pad), axis=0)
                acc = acc + shifted[pad:pad + rows] * w[t:t + 1, :]
        y = acc * jax.nn.sigmoid(acc)
        for h in range(cb // DN_DK):
            sl = slice(h * DN_DK, (h + 1) * DN_DK)
            yh = y[:, sl]
            inv = lax.rsqrt(jnp.sum(yh * yh, axis=-1, keepdims=True) + EPS) * post
            yh = yh * jnp.where(is_qk, inv, 1.0)
            o_ref[r * rows:(r + 1) * rows, sl] = yh.astype(BF16)


def _dprep(proj, conv_w8, off_dq, batch, seq, cb=256, rows=512):
    n = proj.shape[0]
    width = 2 * DN_KW + DN_VW
    kern = functools.partial(_dprep_kernel, seq=seq, rows=rows, q_blocks=DN_KW // cb,
                             k_blocks=DN_KW // cb)
    return pl.pallas_call(
        kern,
        grid=(batch, width // cb),
        in_specs=[
            pl.BlockSpec((seq, cb), lambda b, j: (b, off_dq // cb + j)),
            pl.BlockSpec((8, cb), lambda b, j: (0, j)),
        ],
        out_specs=pl.BlockSpec((seq, cb), lambda b, j: (b, j)),
        out_shape=jax.ShapeDtypeStruct((n, width), BF16),
        scratch_shapes=[pltpu.VMEM((seq + 16, cb), F32)],
        compiler_params=pltpu.CompilerParams(dimension_semantics=("parallel", "parallel")),
        name="dprep",
    )(proj, conv_w8)


HEADS_PER_TILE = 4
CAT_W = HEADS_PER_TILE * CHUNK
N_TILES = DN_HEADS // HEADS_PER_TILE


def _split3(x):
    p1 = x.astype(BF16)
    r1 = x - p1.astype(F32)
    p2 = r1.astype(BF16)
    p3 = (r1 - p2.astype(F32)).astype(BF16)
    return p1, p2, p3


def _softplus(x):
    return jnp.maximum(x, 0.0) + jnp.log(1.0 + jnp.exp(-jnp.abs(x)))


def _dot(a, b):
    return jnp.dot(a, b, preferred_element_type=F32)


def _dchunk_kernel(x_ref, c_ref, r_ref, pc_ref, pr_ref, a1_ref, a2_ref, au_ref, ae_ref, *, cpb):
    C, W, T = CHUNK, CAT_W, HEADS_PER_TILE
    rows = cpb * C

    def cum_matrix(n, lower):
        i = lax.broadcasted_iota(jnp.int32, (n, n), 0)
        j = lax.broadcasted_iota(jnp.int32, (n, n), 1)
        same = (i >> 6) == (j >> 6)
        return (same & ((i >= j) if lower else (i <= j))).astype(BF16)

    col_lower, col_upper = cum_matrix(rows, True), cum_matrix(rows, False)
    row_lower, row_upper = cum_matrix(W, True), cum_matrix(W, False)

    blk = c_ref[...]
    beta_all = jax.nn.sigmoid(blk)
    g_parts = _split3(-jnp.exp(pc_ref[0:1, :]) * _softplus(blk + pc_ref[1:2, :]))
    gcol = (sum(_dot(col_lower, p) for p in g_parts),
            sum(_dot(col_upper, p) for p in g_parts))
    blk_r = r_ref[...]
    g_r = (-jnp.exp(pr_ref[0][None]) * _softplus(blk_r + pr_ref[1][None])).reshape(cpb * 8, W)
    gr_parts = _split3(g_r)
    grow = (sum(_dot(p, row_upper) for p in gr_parts),
            sum(_dot(p, row_lower) for p in gr_parts))

    ri = lax.broadcasted_iota(jnp.int32, (C, W), 0)
    li = lax.broadcasted_iota(jnp.int32, (C, W), 1)
    ci = li & (C - 1)
    lb = li >> 6
    incl = (ri >= ci, ri <= ci)
    offdiag = ri != ci
    eye = (ri == ci).astype(F32)
    same = {s: (ri >> lg) == (ci >> lg) for lg, s in ((1, 2), (2, 4), (3, 8), (4, 16), (5, 32))}
    level_masks = [same[4] & ~same[2], same[8] & ~same[4], same[16] & ~same[8],
                   same[32] & ~same[16], ~same[32]]
    tile_sel = [lb == t for t in range(T)]
    head_of_lane = lax.broadcasted_iota(jnp.int32, (C, T * DN_DK), 1) >> 7

    def block_diag(y):
        zero = jnp.zeros_like(y)
        return jnp.concatenate([jnp.where(tile_sel[t], y, zero) for t in range(T)], axis=0)

    def cat_bcast(cols):
        out = jnp.broadcast_to(cols[T - 1], (C, W))
        for t in range(T - 2, -1, -1):
            out = jnp.where(tile_sel[t], jnp.broadcast_to(cols[t], (C, W)), out)
        return out

    pairs = [(c, t) for c in range(cpb) for t in range(N_TILES)]
    units = [(c, t, d) for c in range(cpb) for t in range(N_TILES) for d in range(2)]

    qk, kk, q4s, k4s, v4s = {}, {}, {}, {}, {}
    for c, t in pairs:
        rs = slice(c * C, (c + 1) * C)
        q4 = x_ref[rs, t * T * DN_DK:(t + 1) * T * DN_DK]
        k4 = x_ref[rs, DN_KW + t * T * DN_DK:DN_KW + (t + 1) * T * DN_DK]
        v4 = x_ref[rs, 2 * DN_KW + t * T * DN_DV:2 * DN_KW + (t + 1) * T * DN_DV]
        zero = jnp.zeros_like(k4)
        k_bd = jnp.concatenate([jnp.where(head_of_lane == h, k4, zero) for h in range(T)], axis=0)
        r = lax.dot_general(jnp.concatenate([q4, k4], axis=0), k_bd, (((1,), (1,)), ((), ())),
                            preferred_element_type=F32)
        qk[c, t], kk[c, t] = r[:C], r[C:]
        q4s[c, t], k4s[c, t], v4s[c, t] = q4, k4, v4

    lmat, amat, xinv = {}, {}, {}
    for c, t, d in units:
        rs = slice(c * C, (c + 1) * C)
        base = d * DN_HEADS + t * T
        gc = cat_bcast([gcol[d][rs, 2 * DN_HEADS + base + h:2 * DN_HEADS + base + h + 1]
                        for h in range(T)])
        beta = cat_bcast([beta_all[rs, base + h:base + h + 1] for h in range(T)])
        row = c * 8 + d * N_TILES + t
        gr = jnp.broadcast_to(grow[d][row:row + 1, :], (C, W))
        decay = jnp.exp(jnp.where(incl[d], gc - gr, -1e30))
        l = jnp.where(offdiag, beta * kk[c, t] * decay, 0.0)
        lmat[c, t, d] = l
        amat[c, t, d] = qk[c, t] * decay
        xinv[c, t, d] = eye - jnp.where(same[2], l, 0.0)

    for msk in level_masks:
        ys, xbs = {}, {}
        for u in units:
            e = jnp.where(msk, lmat[u], 0.0).astype(BF16)
            xbs[u] = xinv[u].astype(BF16)
            ys[u] = _dot(e, block_diag(xbs[u]))
        for u in units:
            xinv[u] = xinv[u] - _dot(xbs[u], block_diag(ys[u].astype(BF16)))

    for c, t, d in units:
        rs = slice(c * C, (c + 1) * C)
        kd_rows, rhs_rows = [], []
        for h in range(T):
            head = t * T + h
            col = d * DN_HEADS + head
            hs = slice(h * DN_DK, (h + 1) * DN_DK)
            out_s = slice(head * DN_DK, (head + 1) * DN_DK)
            gc = jnp.broadcast_to(gcol[d][rs, 2 * DN_HEADS + col:2 * DN_HEADS + col + 1],
                                  (C, DN_DK))
            beta = jnp.broadcast_to(beta_all[rs, col:col + 1], (C, DN_DK))
            g_last = gc[C - 1:C, :] if d == 0 else gc[0:1, :]
            eg = jnp.exp(gc)
            kf = k4s[c, t][:, hs].astype(F32)
            rhs = jnp.concatenate([kf * (beta * eg), v4s[c, t][:, hs].astype(F32) * beta],
                                  axis=1).astype(BF16)
            zero = jnp.zeros_like(rhs)
            rhs_rows.append(jnp.concatenate([rhs if hh == h else zero for hh in range(T)], axis=1))
            kd_rows.append(kf * jnp.exp(g_last - gc))
            a1_ref[c, d, C:2 * C, out_s] = (q4s[c, t][:, hs].astype(F32) * eg).astype(BF16)
            ae_ref[c, d, :, out_s] = jnp.broadcast_to(jnp.exp(g_last), (8, DN_DK))
        wu = _dot(xinv[c, t, d].astype(BF16), jnp.concatenate(rhs_rows, axis=0))
        for h in range(T):
            head = t * T + h
            out_s = slice(head * DN_DK, (head + 1) * DN_DK)
            a1_ref[c, d, 0:C, out_s] = wu[:, 2 * h * DN_DK:(2 * h + 1) * DN_DK].astype(BF16)
            au_ref[c, d, :, out_s] = wu[:, (2 * h + 1) * DN_DK:(2 * h + 2) * DN_DK].astype(BF16)
        cat_s = slice(t * W, (t + 1) * W)
        a2_ref[c, d, 0:C, cat_s] = amat[c, t, d].astype(BF16)
        a2_ref[c, d, C:C + DN_DK, cat_s] = jnp.concatenate(kd_rows, axis=0).T.astype(BF16)


def _dchunk(dqkv, small, small_r, par_col, par_row, batch, seq, cpb=4):
    nc = seq // CHUNK
    steps = nc // cpb
    width = dqkv.shape[1]
    kern = functools.partial(_dchunk_kernel, cpb=cpb)
    out5 = lambda b, i: (b, i, 0, 0, 0)
    return pl.pallas_call(
        kern,
        grid=(batch, steps),
        in_specs=[
            pl.BlockSpec((cpb * CHUNK, width), lambda b, i: (b * steps + i, 0)),
            pl.BlockSpec((cpb * CHUNK, LANES), lambda b, i: (b * steps + i, 0)),
            pl.BlockSpec((cpb, 8, CAT_W), lambda b, i: (b * steps + i, 0, 0)),
            pl.BlockSpec((8, LANES), lambda b, i: (0, 0)),
            pl.BlockSpec((2, 8, CAT_W), lambda b, i: (0, 0, 0)),
        ],
        out_specs=[
            pl.BlockSpec((None, cpb, 2, 2 * CHUNK, DN_KW), out5),
            pl.BlockSpec((None, cpb, 2, CHUNK + DN_DK, DN_HEADS * CHUNK), out5),
            pl.BlockSpec((None, cpb, 2, CHUNK, DN_VW), out5),
            pl.BlockSpec((None, cpb, 2, 8, DN_VW), out5),
        ],
        out_shape=[
            jax.ShapeDtypeStruct((batch, nc, 2, 2 * CHUNK, DN_KW), BF16),
            jax.ShapeDtypeStruct((batch, nc, 2, CHUNK + DN_DK, DN_HEADS * CHUNK), BF16),
            jax.ShapeDtypeStruct((batch, nc, 2, CHUNK, DN_VW), BF16),
            jax.ShapeDtypeStruct((batch, nc, 2, 8, DN_VW), F32),
        ],
        compiler_params=pltpu.CompilerParams(dimension_semantics=("parallel", "parallel")),
        name="dchunk",
    )(dqkv, small, small_r, par_col, par_row)


def _dscan_kernel(a1f_ref, a1b_ref, a2f_ref, a2b_ref, auf_ref, aub_ref, aef_ref, aeb_ref,
                  of_ref, ob_ref, s_ref, *, batch, cps):
    C = CHUNK
    PW = 2 * DN_DV
    n_pairs = DN_HEADS // 2

    @pl.when(pl.program_id(0) == 0)
    def _():
        s_ref[...] = jnp.zeros(s_ref.shape, F32)

    first = lax.broadcasted_iota(jnp.int32, (DN_DK, PW), 1) < DN_DV
    first_c = lax.broadcasted_iota(jnp.int32, (C, PW), 1) < DN_DV
    refs = ((a1f_ref, a2f_ref, auf_ref, aef_ref, of_ref), (a1b_ref, a2b_ref, aub_ref, aeb_ref, ob_ref))
    units = [(b, d, p) for b in range(batch) for d in range(2) for p in range(n_pairs)]
    sidx = lambda u: (u[0] * 2 + u[1]) * n_pairs + u[2]

    states = {u: s_ref[sidx(u)] for u in units}
    for step in range(cps):
        r1, r2 = {}, {}
        cc = (step, cps - 1 - step)
        for u in units:
            b, d, p = u
            r1[u] = _dot(refs[d][0][b, cc[d], :, p * PW:(p + 1) * PW], states[u].astype(BF16))
        for u in units:
            b, d, p = u
            vb = (refs[d][2][b, cc[d], :, p * PW:(p + 1) * PW].astype(F32) - r1[u][:C]).astype(BF16)
            zero = jnp.zeros_like(vb)
            v_bd = jnp.concatenate([jnp.where(first_c, vb, zero), jnp.where(first_c, zero, vb)],
                                   axis=0)
            r2[u] = _dot(refs[d][1][b, cc[d], :, p * 2 * C:(p + 1) * 2 * C], v_bd)
        for u in units:
            b, d, p = u
            refs[d][4][b, cc[d], :, p * PW:(p + 1) * PW] = (r1[u][C:] + r2[u][:C]).astype(BF16)
            e = refs[d][3][b, cc[d], 0:1, p * PW:(p + 1) * PW]
            upd = r2[u][C:]
            s = states[u]
            states[u] = jnp.concatenate(
                [s[:DN_DK] * e + jnp.where(first, upd, 0.0),
                 s[DN_DK:] * e + jnp.where(first, 0.0, upd)], axis=0)
    for u in units:
        s_ref[sidx(u)] = states[u]


def _dscan(a1, a2, au, ae, batch, seq, cps=2):
    nc = seq // CHUNK
    steps = nc // cps
    kern = functools.partial(_dscan_kernel, batch=batch, cps=cps)
    fwd = lambda i: (0, i, 0, 0, 0)
    bwd = lambda i: (0, steps - 1 - i, 1, 0, 0)

    def spec(arr, imap):
        return pl.BlockSpec((batch, cps, None) + arr.shape[3:], imap)

    o_spec = lambda imap: pl.BlockSpec((batch, cps, CHUNK, DN_VW), imap)
    o_shape = jax.ShapeDtypeStruct((batch, nc, CHUNK, DN_VW), BF16)
    return pl.pallas_call(
        kern,
        grid=(steps,),
        in_specs=[spec(a1, fwd), spec(a1, bwd), spec(a2, fwd), spec(a2, bwd),
                  spec(au, fwd), spec(au, bwd), spec(ae, fwd), spec(ae, bwd)],
        out_specs=[o_spec(lambda i: (0, i, 0, 0)), o_spec(lambda i: (0, steps - 1 - i, 0, 0))],
        out_shape=[o_shape, o_shape],
        scratch_shapes=[pltpu.VMEM((batch * 2 * (DN_HEADS // 2), 2 * DN_DK, 2 * DN_DV), F32)],
        compiler_params=pltpu.CompilerParams(dimension_semantics=("arbitrary",)),
        name="dscan",
    )(a1, a1, a2, a2, au, au, ae, ae)


def _post_kernel(oa_ref, of_ref, ob_ref, dz_ref, ga_ref, gd_ref, x_ref, p_ref,
                 wa_ref, wd_ref, wo_ref, wg_ref, wp_ref, nd_ref, npost_ref, nple_ref, o_ref):
    def rms(t, w):
        return t * lax.rsqrt(jnp.mean(t * t, axis=-1, keepdims=True) + EPS) * w

    nd = nd_ref[...]
    parts = []
    for h in range(DN_HEADS):
        sl = slice(h * DN_DV, (h + 1) * DN_DV)
        od = of_ref[:, sl].astype(F32) + ob_ref[:, sl].astype(F32)
        z = dz_ref[:, sl].astype(F32)
        parts.append((rms(od, nd) * (z * jax.nn.sigmoid(z))).astype(BF16))
    y_dn = jnp.dot(jnp.concatenate(parts, axis=1), wd_ref[...], preferred_element_type=F32)
    y_att = jnp.dot(oa_ref[...], wa_ref[...], preferred_element_type=F32)
    merged = (jax.nn.sigmoid(ga_ref[...].astype(F32)) * y_att
              + jax.nn.sigmoid(gd_ref[...].astype(F32)) * y_dn)
    mix = jnp.dot(merged.astype(BF16), wo_ref[...], preferred_element_type=F32)
    x1 = x_ref[...] + rms(mix, npost_ref[...])
    gate = jax.nn.sigmoid(jnp.dot(x1.astype(BF16), wg_ref[...], preferred_element_type=F32))
    e = jnp.dot(p_ref[...].astype(BF16), wp_ref[...], preferred_element_type=F32)
    o_ref[...] = x1 + rms(gate * e, nple_ref[...])


def _post(o_att, o_f, o_b, proj, x2, p2, wa, wd, wo, wg, wp, nd, npost, nple, offs, tm=512):
    n, d = x2.shape
    ple = p2.shape[1]
    row = lambda i: (i, 0)
    const = lambda i: (0, 0)
    return pl.pallas_call(
        _post_kernel,
        grid=(n // tm,),
        in_specs=[
            pl.BlockSpec((tm, ATT_W), row),
            pl.BlockSpec((tm, DN_VW), row),
            pl.BlockSpec((tm, DN_VW), row),
            pl.BlockSpec((tm, DN_VW), lambda i: (i, offs["dz"] // DN_VW)),
            pl.BlockSpec((tm, d), lambda i: (i, offs["ga"] // d)),
            pl.BlockSpec((tm, d), lambda i: (i, offs["gd"] // d)),
            pl.BlockSpec((tm, d), row),
            pl.BlockSpec((tm, ple), row),
            pl.BlockSpec((ATT_W, d), const),
            pl.BlockSpec((DN_VW, d), const),
            pl.BlockSpec((d, d), const),
            pl.BlockSpec((d, d), const),
            pl.BlockSpec((ple, d), const),
            pl.BlockSpec((1, DN_DV), const),
            pl.BlockSpec((1, d), const),
            pl.BlockSpec((1, d), const),
        ],
        out_specs=pl.BlockSpec((tm, d), row),
        out_shape=jax.ShapeDtypeStruct((n, d), F32),
        compiler_params=pltpu.CompilerParams(dimension_semantics=("parallel",)),
        name="post",
    )(o_att, o_f, o_b, proj, proj, proj, x2, p2, wa, wd, wo, wg, wp, nd, npost, nple)


def _rope_tables(seq):
    rows = seq // GRID_W
    row = jnp.broadcast_to(jnp.arange(rows)[:, None], (rows, GRID_W)).reshape(seq)
    col = jnp.broadcast_to(jnp.arange(GRID_W)[None, :], (rows, GRID_W)).reshape(seq)
    n_freq = HEAD_DIM // 4
    inv_freq = ROPE_THETA ** (-jnp.arange(n_freq, dtype=F32) / n_freq)
    ang = jnp.concatenate([row.astype(F32)[:, None] * inv_freq,
                           col.astype(F32)[:, None] * inv_freq], axis=-1)
    cos, sin = jnp.cos(ang), jnp.sin(ang)
    return jnp.concatenate([cos, cos], axis=-1), jnp.concatenate([-sin, sin], axis=-1)


def kernel(x, p, norm_pre, w_in, q_norm, k_norm, conv_w, a_log, dt_bias, dn_norm, w_br_att,
           w_br_dn, w_out, norm_post, w_ple_proj, w_ple_gate, ple_norm):
    batch, seq, d = x.shape
    depth = w_in.shape[0]
    n = batch * seq
    assert seq % GRID_W == 0 and seq % CHUNK == 0 and d % LANES == 0
    cos, sin = _rope_tables(seq)
    x2 = x.reshape(n, d)
    for i in range(depth):
        w_main, w_small, offs = _regroup_columns(w_in[i].astype(BF16))
        proj, small = _proj(x2, norm_pre[i][None, :], w_main, w_small)

        tq, tk = min(ATT_TQ, seq), min(ATT_TK, seq // 2)
        qt, k_rot, vt = _aprep(proj, cos, sin, _deinterleave(q_norm[i])[None, :],
                               _deinterleave(k_norm[i])[None, :], offs, batch, seq, tq, tk)
        o_att = _attention(proj, qt, k_rot, vt, offs, batch, seq, tq, tk)

        conv_w8 = jnp.pad(conv_w[i], ((0, 8 - CONV_K), (0, 0)))
        dqkv = _dprep(proj, conv_w8, offs["dq"], batch, seq)
        nsm = 4 * DN_HEADS
        nck = n // CHUNK
        small_r = small[:, 2 * DN_HEADS:nsm].reshape(nck, CHUNK, 2, N_TILES, HEADS_PER_TILE)
        small_r = small_r.transpose(0, 2, 3, 4, 1).reshape(nck, 2 * N_TILES, CAT_W)
        small_r = jnp.pad(small_r, ((0, 0), (0, 8 - 2 * N_TILES), (0, 0)))
        par = jnp.stack([a_log[i].reshape(-1), dt_bias[i].reshape(-1)])
        par_col = jnp.pad(jnp.concatenate([jnp.zeros_like(par), par], axis=1),
                          ((0, 6), (0, LANES - nsm)))
        par_row = jnp.repeat(par.reshape(2, 2 * N_TILES, HEADS_PER_TILE), CHUNK, axis=2)
        par_row = jnp.pad(par_row, ((0, 0), (0, 8 - 2 * N_TILES), (0, 0)))
        a1, a2, au, ae = _dchunk(dqkv, small, small_r, par_col, par_row, batch, seq)
        o_f, o_b = _dscan(a1, a2, au, ae, batch, seq)
        o_f, o_b = o_f.reshape(n, DN_VW), o_b.reshape(n, DN_VW)

        x2 = _post(o_att, o_f, o_b, proj, x2, p[i].reshape(n, -1),
                   w_br_att[i].astype(BF16), w_br_dn[i].astype(BF16), w_out[i].astype(BF16),
                   w_ple_gate[i].astype(BF16), w_ple_proj[i].astype(BF16),
                   dn_norm[i][None, :], norm_post[i][None, :], ple_norm[i][None, :], offs)
    return x2.reshape(batch, seq, d)
```

```python
import functools

import numpy as np
import jax
import jax.numpy as jnp
from jax import lax
from jax.experimental import pallas as pl
from jax.experimental.pallas import tpu as pltpu

F32 = jnp.float32
BF16 = jnp.bfloat16

GRID_W = 64
ATT_HEADS = 8
ATT_KV_HEADS = 2
HEAD_DIM = 128
ROPE_THETA = 10000.0
DN_HEADS = 8
DN_DK = 128
DN_DV = 128
CONV_K = 5
CHUNK = 64
EPS = 1e-6

ATT_W = ATT_HEADS * HEAD_DIM
KV_W = ATT_KV_HEADS * HEAD_DIM
DN_KW = DN_HEADS * DN_DK
DN_VW = DN_HEADS * DN_DV
GROUP = ATT_HEADS // ATT_KV_HEADS
LANES = 128


def _deinterleave(t):
    lead = t.shape[:-1]
    t = t.reshape(lead + (-1, HEAD_DIM // 2, 2))
    return jnp.swapaxes(t, -1, -2).reshape(lead + (-1,))


PROJ_TN = 4352
CONV_HALO = 16
CONV_PIECE = 512


def _regroup_columns(w):
    d_model = w.shape[0]
    sizes = (ATT_W, KV_W, KV_W, ATT_W, DN_KW, DN_KW, DN_VW, 2 * DN_HEADS, 2 * DN_HEADS, DN_VW,
             d_model, d_model)
    assert w.shape[1] == sum(sizes)
    aq, ak, av, az, dq, dk, dv, db, da, dz, ga, gd = jnp.split(w, np.cumsum(sizes)[:-1], axis=1)
    aq0, aq1 = jnp.split(_deinterleave(aq), 2, axis=1)
    az0, az1 = jnp.split(az, 2, axis=1)
    dk0, dk1 = jnp.split(dk, 2, axis=1)
    groups = [("ga", ga, "plain", 0), ("gd", gd, "plain", 0), ("dq", dq, "conv_q", 0),
              ("dk0", dk0, "conv_k", DN_KW), ("aq0", aq0, "plain", 0),
              ("ak", _deinterleave(ak), "plain", 0),
              ("av", av, "plain", 0), ("dk1", dk1, "conv_k", DN_KW + DN_KW // 2),
              ("dz", dz, "plain", 0), ("dv", dv, "conv_v", 2 * DN_KW), ("aq1", aq1, "plain", 0),
              ("az0", az0, "plain", 0), ("az1", az1, "plain", 0)]
    offsets, off = {}, 0
    pieces = [[] for _ in range(sum(g.shape[1] for _, g, _, _ in groups) // PROJ_TN)]
    for name, g, kind, ch in groups:
        offsets[name] = off
        width = g.shape[1]
        tile, local = divmod(off, PROJ_TN)
        assert local + width <= PROJ_TN
        if kind == "plain":
            last = pieces[tile][-1] if pieces[tile] else None
            if last is not None and last[2] == "plain" and last[0] + last[1] == local:
                pieces[tile][-1] = (last[0], last[1] + width, "plain", 0)
            else:
                pieces[tile].append((local, width, "plain", 0))
        else:
            for k in range(0, width, CONV_PIECE):
                pieces[tile].append((local + k, CONV_PIECE, kind, ch + k))
        off += width
    main = jnp.concatenate([g for _, g, _, _ in groups], axis=1)
    small = jnp.pad(jnp.concatenate([db, da], axis=1), ((0, 0), (0, LANES - 4 * DN_HEADS)))
    return main, small, offsets, pieces


def _proj_kernel(x_ref, xp_ref, xn_ref, g_ref, w_ref, ws_ref, cw_ref, o_ref, os_ref, h_ref,
                 *, tiles_per_seq, pieces):
    i, j = pl.program_id(0), pl.program_id(1)
    tm = x_ref.shape[0]
    ext = tm + 2 * CONV_HALO
    half = CONV_K // 2

    def normed(x):
        r = lax.rsqrt(jnp.mean(x * x, axis=-1, keepdims=True) + EPS)
        return (x * r * g_ref[...]).astype(BF16)

    @pl.when(j == 0)
    def _():
        t = i % tiles_per_seq
        h = normed(x_ref[...])
        zero = jnp.zeros((CONV_HALO, x_ref.shape[1]), BF16)
        h_ref[0:CONV_HALO, :] = jnp.where(t == 0, zero, normed(xp_ref[...]))
        h_ref[CONV_HALO:CONV_HALO + tm, :] = h
        h_ref[CONV_HALO + tm:, :] = jnp.where(t == tiles_per_seq - 1, zero, normed(xn_ref[...]))
        os_ref[...] = jnp.dot(h, ws_ref[...], preferred_element_type=F32)

    def tile(tile_pieces):
        for start, width, kind, ch in tile_pieces:
            cols = slice(start, start + width)
            if kind == "plain":
                o_ref[:, cols] = jnp.dot(h_ref[CONV_HALO:CONV_HALO + tm, :], w_ref[:, cols],
                                         preferred_element_type=F32).astype(BF16)
                continue
            r = jnp.dot(h_ref[...], w_ref[:, cols], preferred_element_type=F32)
            taps = cw_ref[:, ch:ch + width]
            acc = r[CONV_HALO:CONV_HALO + tm] * taps[half:half + 1]
            for t in range(CONV_K):
                if t != half:
                    shifted = pltpu.roll(r, (half - t) % ext, axis=0)
                    acc = acc + shifted[CONV_HALO:CONV_HALO + tm] * taps[t:t + 1]
            y = acc * jax.nn.sigmoid(acc)
            for hh in range(width // DN_DK):
                yh = y[:, hh * DN_DK:(hh + 1) * DN_DK]
                if kind != "conv_v":
                    inv = lax.rsqrt(jnp.sum(yh * yh, axis=-1, keepdims=True) + EPS)
                    yh = yh * (inv * DN_DK ** -0.5 if kind == "conv_q" else inv)
                o_ref[:, start + hh * DN_DK:start + (hh + 1) * DN_DK] = yh.astype(BF16)

    for jj, tile_pieces in enumerate(pieces):
        pl.when(j == jj)(functools.partial(tile, tile_pieces))


def _proj(x2, g, w_main, w_small, conv_w8, pieces, seq, tm=512):
    n, d = x2.shape
    main_w = w_main.shape[1]
    tm = min(tm, seq)
    hb = tm // CONV_HALO
    last_hb = n // CONV_HALO - 1
    kern = functools.partial(_proj_kernel, tiles_per_seq=seq // tm, pieces=pieces)
    return pl.pallas_call(
        kern,
        grid=(n // tm, main_w // PROJ_TN),
        in_specs=[
            pl.BlockSpec((tm, d), lambda i, j: (i, 0)),
            pl.BlockSpec((CONV_HALO, d), lambda i, j: (jnp.maximum(i * hb - 1, 0), 0)),
            pl.BlockSpec((CONV_HALO, d), lambda i, j: (jnp.minimum((i + 1) * hb, last_hb), 0)),
            pl.BlockSpec((1, d), lambda i, j: (0, 0)),
            pl.BlockSpec((d, PROJ_TN), lambda i, j: (0, j)),
            pl.BlockSpec((d, LANES), lambda i, j: (0, 0)),
            pl.BlockSpec(conv_w8.shape, lambda i, j: (0, 0)),
        ],
        out_specs=[
            pl.BlockSpec((tm, PROJ_TN), lambda i, j: (i, j)),
            pl.BlockSpec((tm, LANES), lambda i, j: (i, 0)),
        ],
        out_shape=[
            jax.ShapeDtypeStruct((n, main_w), BF16),
            jax.ShapeDtypeStruct((n, LANES), F32),
        ],
        scratch_shapes=[pltpu.VMEM((tm + 2 * CONV_HALO, d), BF16)],
        compiler_params=pltpu.CompilerParams(dimension_semantics=("parallel", "arbitrary")),
        name="proj",
    )(x2, x2, x2, g, w_main, w_small, conv_w8)


def _norm_rope(xh, w, cos, sin):
    r = lax.rsqrt(jnp.mean(xh * xh, axis=-1, keepdims=True) + EPS)
    xn = xh * r * w
    return xn * cos + pltpu.roll(xn, HEAD_DIM // 2, axis=1) * sin


ATT_TQ = 1024
ATT_TK = 512
VT_ROWS = HEAD_DIM + 16


def _aprep_kernel(q0_ref, q1_ref, k_ref, v_ref, cos_ref, sin_ref, wq_ref, wk_ref,
                  qt_ref, kr_ref, vt_ref):
    cos, sin = cos_ref[...], sin_ref[...]
    scale = HEAD_DIM ** -0.5
    tm = k_ref.shape[0]
    tk = vt_ref.shape[-1]
    for h, q_ref in enumerate((q0_ref, q1_ref)):
        sl = slice(h * HEAD_DIM, (h + 1) * HEAD_DIM)
        kr_ref[:, sl] = _norm_rope(k_ref[:, sl].astype(F32), wk_ref[...], cos, sin).astype(BF16)
        for c in range(tm // tk):
            vt_ref[h, c, 0:HEAD_DIM, :] = v_ref[c * tk:(c + 1) * tk, sl].astype(F32).T.astype(BF16)
            vt_ref[h, c, HEAD_DIM:, :] = jnp.ones((VT_ROWS - HEAD_DIM, tk), BF16)
        for g in range(GROUP):
            qs = slice(g * HEAD_DIM, (g + 1) * HEAD_DIM)
            qr = _norm_rope(q_ref[:, qs].astype(F32), wq_ref[...], cos, sin) * scale
            qt_ref[h, :, g * tm:(g + 1) * tm] = qr.T.astype(BF16)


def _aprep(proj, cos, sin, wq, wk, offs, batch, seq, tm, tk):
    assert ATT_KV_HEADS == 2
    n = proj.shape[0]
    t = seq // tm
    cpt = tm // tk
    gw = GROUP * HEAD_DIM
    return pl.pallas_call(
        _aprep_kernel,
        grid=(batch, t),
        in_specs=[
            pl.BlockSpec((tm, gw), lambda b, i: (b * t + i, offs["aq0"] // gw)),
            pl.BlockSpec((tm, gw), lambda b, i: (b * t + i, offs["aq1"] // gw)),
            pl.BlockSpec((tm, KV_W), lambda b, i: (b * t + i, offs["ak"] // KV_W)),
            pl.BlockSpec((tm, KV_W), lambda b, i: (b * t + i, offs["av"] // KV_W)),
            pl.BlockSpec((tm, HEAD_DIM), lambda b, i: (i, 0)),
            pl.BlockSpec((tm, HEAD_DIM), lambda b, i: (i, 0)),
            pl.BlockSpec((1, HEAD_DIM), lambda b, i: (0, 0)),
            pl.BlockSpec((1, HEAD_DIM), lambda b, i: (0, 0)),
        ],
        out_specs=[
            pl.BlockSpec((None, None, ATT_KV_HEADS, HEAD_DIM, GROUP * tm),
                         lambda b, i: (b, i, 0, 0, 0)),
            pl.BlockSpec((tm, KV_W), lambda b, i: (b * t + i, 0)),
            pl.BlockSpec((None, ATT_KV_HEADS, cpt, VT_ROWS, tk), lambda b, i: (b, 0, i, 0, 0)),
        ],
        out_shape=[
            jax.ShapeDtypeStruct((batch, t, ATT_KV_HEADS, HEAD_DIM, GROUP * tm), BF16),
            jax.ShapeDtypeStruct((n, KV_W), BF16),
            jax.ShapeDtypeStruct((batch, ATT_KV_HEADS, seq // tk, VT_ROWS, tk), BF16),
        ],
        compiler_params=pltpu.CompilerParams(dimension_semantics=("parallel", "parallel")),
        name="aprep",
    )(proj, proj, proj, proj, cos, sin, wq, wk)


def _attn_kernel(qt_ref, z_ref, k_ref, vt_ref, o_ref, acc_ref, s0_ref, s1_ref, p0_ref, p1_ref,
                 *, tq, tk, seq):
    cols = GROUP * tq
    n_chunks = seq // tk
    qt = qt_ref[...]

    def scores(c, s_ref):
        start = pl.multiple_of(c * tk, tk)
        s = jnp.dot(k_ref[pl.ds(start, tk), :], qt, preferred_element_type=F32)
        s_ref[...] = s
        return jnp.max(s, axis=0, keepdims=True)

    def softmax(s_ref, p_ref, m, smax):
        m_new = jnp.maximum(m, smax)
        p_ref[...] = jnp.exp((s_ref[...] - m_new).astype(BF16))
        return m_new, jnp.exp(m - m_new)

    def accumulate(c, p_ref, alpha):
        acc_ref[...] = alpha * acc_ref[...] + jnp.dot(vt_ref[c], p_ref[...],
                                                      preferred_element_type=F32)

    def pair(c, carry, last):
        m, alpha_prev, smax0 = carry
        m, alpha0 = softmax(s0_ref, p0_ref, m, smax0)
        smax1 = scores(c + 1, s1_ref)
        accumulate(jnp.maximum(c - 1, 0), p1_ref, alpha_prev)
        m, alpha1 = softmax(s1_ref, p1_ref, m, smax1)
        if not last:
            smax0 = scores(c + 2, s0_ref)
        accumulate(c, p0_ref, alpha0)
        return m, alpha1, smax0

    acc_ref[...] = jnp.zeros(acc_ref.shape, F32)
    p1_ref[...] = jnp.zeros(p1_ref.shape, BF16)
    smax = scores(0, s0_ref)
    carry = (jnp.full((1, cols), -jnp.inf, F32), jnp.ones((1, cols), F32), smax)
    carry = lax.fori_loop(0, n_chunks // 2 - 1, lambda j, cr: pair(2 * j, cr, False), carry)
    _, alpha, _ = pair(n_chunks - 2, carry, True)
    accumulate(n_chunks - 1, p1_ref, alpha)
    acc = acc_ref[...]
    o_t = acc[:HEAD_DIM] / acc[HEAD_DIM:HEAD_DIM + 1]
    for g in range(GROUP):
        sl = slice(g * HEAD_DIM, (g + 1) * HEAD_DIM)
        z = z_ref[:, sl].astype(F32)
        o_ref[:, sl] = (o_t[:, g * tq:(g + 1) * tq].T * (z * jax.nn.sigmoid(z))).astype(BF16)


def _attention(proj, qt, k_rot, vt, offs, batch, seq, tq, tk):
    n = proj.shape[0]
    gw = GROUP * HEAD_DIM
    t = seq // tq
    cols = GROUP * tq
    assert (seq // tk) % 2 == 0 and offs["az1"] == offs["az0"] + gw
    kern = functools.partial(_attn_kernel, tq=tq, tk=tk, seq=seq)
    return pl.pallas_call(
        kern,
        grid=(batch, ATT_KV_HEADS, t),
        in_specs=[
            pl.BlockSpec((None, None, None, HEAD_DIM, cols), lambda b, h, i: (b, i, h, 0, 0)),
            pl.BlockSpec((tq, gw), lambda b, h, i: (b * t + i, offs["az0"] // gw + h)),
            pl.BlockSpec((seq, HEAD_DIM), lambda b, h, i: (b, h)),
            pl.BlockSpec((None, None, seq // tk, VT_ROWS, tk), lambda b, h, i: (b, h, 0, 0, 0)),
        ],
        out_specs=pl.BlockSpec((tq, gw), lambda b, h, i: (b * t + i, h)),
        out_shape=jax.ShapeDtypeStruct((n, ATT_W), BF16),
        scratch_shapes=[pltpu.VMEM((VT_ROWS, cols), F32),
                        pltpu.VMEM((tk, cols), F32), pltpu.VMEM((tk, cols), F32),
                        pltpu.VMEM((tk, cols), BF16), pltpu.VMEM((tk, cols), BF16)],
        compiler_params=pltpu.CompilerParams(
            dimension_semantics=("parallel", "parallel", "arbitrary")),
        name="attn",
    )(qt, proj, k_rot, vt)


HEADS_PER_TILE = 4
CAT_W = HEADS_PER_TILE * CHUNK
N_TILES = DN_HEADS // HEADS_PER_TILE


def _split3(x):
    p1 = x.astype(BF16)
    r1 = x - p1.astype(F32)
    p2 = r1.astype(BF16)
    p3 = (r1 - p2.astype(F32)).astype(BF16)
    return p1, p2, p3


def _softplus(x):
    return jnp.maximum(x, 0.0) + jnp.log(1.0 + jnp.exp(-jnp.abs(x)))


def _dot(a, b):
    return jnp.dot(a, b, preferred_element_type=F32)


def _dchunk_kernel(q0_ref, q1_ref, k0_ref, k1_ref, v0_ref, v1_ref, c_ref, r_ref, pc_ref, pr_ref,
                   a1_ref, a2_ref, au_ref, ae_ref, *, cpb):
    C, W, T = CHUNK, CAT_W, HEADS_PER_TILE
    rows = cpb * C

    def cum_matrix(n, lower):
        i = lax.broadcasted_iota(jnp.int32, (n, n), 0)
        j = lax.broadcasted_iota(jnp.int32, (n, n), 1)
        same = (i >> 6) == (j >> 6)
        return (same & ((i >= j) if lower else (i <= j))).astype(BF16)

    col_lower, col_upper = cum_matrix(rows, True), cum_matrix(rows, False)
    row_lower, row_upper = cum_matrix(W, True), cum_matrix(W, False)

    blk = c_ref[...]
    beta_all = jax.nn.sigmoid(blk)
    g_parts = _split3(-jnp.exp(pc_ref[0:1, :]) * _softplus(blk + pc_ref[1:2, :]))
    gcol = (sum(_dot(col_lower, p) for p in g_parts),
            sum(_dot(col_upper, p) for p in g_parts))
    blk_r = r_ref[...]
    g_r = (-jnp.exp(pr_ref[0][None]) * _softplus(blk_r + pr_ref[1][None])).reshape(cpb * 8, W)
    gr_parts = _split3(g_r)
    grow = (sum(_dot(p, row_upper) for p in gr_parts),
            sum(_dot(p, row_lower) for p in gr_parts))

    ri = lax.broadcasted_iota(jnp.int32, (C, W), 0)
    li = lax.broadcasted_iota(jnp.int32, (C, W), 1)
    ci = li & (C - 1)
    lb = li >> 6
    incl = (ri >= ci, ri <= ci)
    offdiag = ri != ci
    eye = (ri == ci).astype(F32)
    same = {s: (ri >> lg) == (ci >> lg) for lg, s in ((1, 2), (2, 4), (3, 8), (4, 16), (5, 32))}
    level_masks = [same[4] & ~same[2], same[8] & ~same[4], same[16] & ~same[8],
                   same[32] & ~same[16], ~same[32]]
    tile_sel = [lb == t for t in range(T)]
    head_of_lane = lax.broadcasted_iota(jnp.int32, (C, T * DN_DK), 1) >> 7

    def block_diag(y):
        zero = jnp.zeros_like(y)
        return jnp.concatenate([jnp.where(tile_sel[t], y, zero) for t in range(T)], axis=0)

    def cat_bcast(cols):
        out = jnp.broadcast_to(cols[T - 1], (C, W))
        for t in range(T - 2, -1, -1):
            out = jnp.where(tile_sel[t], jnp.broadcast_to(cols[t], (C, W)), out)
        return out

    pairs = [(c, t) for c in range(cpb) for t in range(N_TILES)]
    units = [(c, t, d) for c in range(cpb) for t in range(N_TILES) for d in range(2)]

    qk, kk, q4s, k4s, v4s = {}, {}, {}, {}, {}
    for c, t in pairs:
        rs = slice(c * C, (c + 1) * C)
        q4 = (q0_ref, q1_ref)[t][rs, :]
        k4 = (k0_ref, k1_ref)[t][rs, :]
        v4 = (v0_ref, v1_ref)[t][rs, :]
        zero = jnp.zeros_like(k4)
        k_bd = jnp.concatenate([jnp.where(head_of_lane == h, k4, zero) for h in range(T)], axis=0)
        r = lax.dot_general(jnp.concatenate([q4, k4], axis=0), k_bd, (((1,), (1,)), ((), ())),
                            preferred_element_type=F32)
        qk[c, t], kk[c, t] = r[:C], r[C:]
        q4s[c, t], k4s[c, t], v4s[c, t] = q4, k4, v4

    lmat, amat, xinv = {}, {}, {}
    for c, t, d in units:
        rs = slice(c * C, (c + 1) * C)
        base = d * DN_HEADS + t * T
        gc = cat_bcast([gcol[d][rs, 2 * DN_HEADS + base + h:2 * DN_HEADS + base + h + 1]
                        for h in range(T)])
        beta = cat_bcast([beta_all[rs, base + h:base + h + 1] for h in range(T)])
        row = c * 8 + d * N_TILES + t
        gr = jnp.broadcast_to(grow[d][row:row + 1, :], (C, W))
        decay = jnp.exp(jnp.where(incl[d], gc - gr, -1e30))
        l = jnp.where(offdiag, beta * kk[c, t] * decay, 0.0)
        lmat[c, t, d] = l
        amat[c, t, d] = qk[c, t] * decay
        xinv[c, t, d] = eye - jnp.where(same[2], l, 0.0)

    for msk in level_masks:
        ys, xbs = {}, {}
        for u in units:
            e = jnp.where(msk, lmat[u], 0.0).astype(BF16)
            xbs[u] = xinv[u].astype(BF16)
            ys[u] = _dot(e, block_diag(xbs[u]))
        for u in units:
            xinv[u] = xinv[u] - _dot(xbs[u], block_diag(ys[u].astype(BF16)))

    for c, t, d in units:
        rs = slice(c * C, (c + 1) * C)
        kd_rows, rhs_rows = [], []
        for h in range(T):
            head = t * T + h
            col = d * DN_HEADS + head
            hs = slice(h * DN_DK, (h + 1) * DN_DK)
            out_s = slice(head * DN_DK, (head + 1) * DN_DK)
            gc = jnp.broadcast_to(gcol[d][rs, 2 * DN_HEADS + col:2 * DN_HEADS + col + 1],
                                  (C, DN_DK))
            beta = jnp.broadcast_to(beta_all[rs, col:col + 1], (C, DN_DK))
            g_last = gc[C - 1:C, :] if d == 0 else gc[0:1, :]
            eg = jnp.exp(gc)
            kf = k4s[c, t][:, hs].astype(F32)
            rhs = jnp.concatenate([kf * (beta * eg), v4s[c, t][:, hs].astype(F32) * beta],
                                  axis=1).astype(BF16)
            zero = jnp.zeros_like(rhs)
            rhs_rows.append(jnp.concatenate([rhs if hh == h else zero for hh in range(T)], axis=1))
            kd_rows.append(kf * jnp.exp(g_last - gc))
            a1_ref[c, d, C:2 * C, out_s] = (q4s[c, t][:, hs].astype(F32) * eg).astype(BF16)
            ae_ref[c, d, :, out_s] = jnp.broadcast_to(jnp.exp(g_last), (8, DN_DK))
        wu = _dot(xinv[c, t, d].astype(BF16), jnp.concatenate(rhs_rows, axis=0))
        for h in range(T):
            head = t * T + h
            out_s = slice(head * DN_DK, (head + 1) * DN_DK)
            a1_ref[c, d, 0:C, out_s] = wu[:, 2 * h * DN_DK:(2 * h + 1) * DN_DK].astype(BF16)
            au_ref[c, d, :, out_s] = wu[:, (2 * h + 1) * DN_DK:(2 * h + 2) * DN_DK].astype(BF16)
        cat_s = slice(t * W, (t + 1) * W)
        a2_ref[c, d, 0:C, cat_s] = amat[c, t, d].astype(BF16)
        a2_ref[c, d, C:C + DN_DK, cat_s] = jnp.concatenate(kd_rows, axis=0).T.astype(BF16)


def _dchunk(proj, small, small_r, par_col, par_row, offs, batch, seq, cpb=4):
    assert N_TILES == 2
    nc = seq // CHUNK
    steps = nc // cpb
    tw = HEADS_PER_TILE * DN_DK
    kern = functools.partial(_dchunk_kernel, cpb=cpb)
    out5 = lambda b, i: (b, i, 0, 0, 0)
    tile_cols = [offs["dq"], offs["dq"] + tw, offs["dk0"], offs["dk1"], offs["dv"], offs["dv"] + tw]
    qkv_specs = [pl.BlockSpec((cpb * CHUNK, tw), functools.partial(
        lambda b, i, col: (b * steps + i, col), col=c // tw)) for c in tile_cols]
    return pl.pallas_call(
        kern,
        grid=(batch, steps),
        in_specs=qkv_specs + [
            pl.BlockSpec((cpb * CHUNK, LANES), lambda b, i: (b * steps + i, 0)),
            pl.BlockSpec((cpb, 8, CAT_W), lambda b, i: (b * steps + i, 0, 0)),
            pl.BlockSpec((8, LANES), lambda b, i: (0, 0)),
            pl.BlockSpec((2, 8, CAT_W), lambda b, i: (0, 0, 0)),
        ],
        out_specs=[
            pl.BlockSpec((None, cpb, 2, 2 * CHUNK, DN_KW), out5),
            pl.BlockSpec((None, cpb, 2, CHUNK + DN_DK, DN_HEADS * CHUNK), out5),
            pl.BlockSpec((None, cpb, 2, CHUNK, DN_VW), out5),
            pl.BlockSpec((None, cpb, 2, 8, DN_VW), out5),
        ],
        out_shape=[
            jax.ShapeDtypeStruct((batch, nc, 2, 2 * CHUNK, DN_KW), BF16),
            jax.ShapeDtypeStruct((batch, nc, 2, CHUNK + DN_DK, DN_HEADS * CHUNK), BF16),
            jax.ShapeDtypeStruct((batch, nc, 2, CHUNK, DN_VW), BF16),
            jax.ShapeDtypeStruct((batch, nc, 2, 8, DN_VW), F32),
        ],
        compiler_params=pltpu.CompilerParams(dimension_semantics=("parallel", "parallel")),
        name="dchunk",
    )(proj, proj, proj, proj, proj, proj, small, small_r, par_col, par_row)


def _dscan_kernel(a1f_ref, a1b_ref, a2f_ref, a2b_ref, auf_ref, aub_ref, aef_ref, aeb_ref,
                  of_ref, ob_ref, s_ref, *, batch, cps):
    C = CHUNK
    PW = 2 * DN_DV
    n_pairs = DN_HEADS // 2

    @pl.when(pl.program_id(0) == 0)
    def _():
        s_ref[...] = jnp.zeros(s_ref.shape, F32)

    first = lax.broadcasted_iota(jnp.int32, (DN_DK, PW), 1) < DN_DV
    first_c = lax.broadcasted_iota(jnp.int32, (C, PW), 1) < DN_DV
    refs = ((a1f_ref, a2f_ref, auf_ref, aef_ref, of_ref), (a1b_ref, a2b_ref, aub_ref, aeb_ref, ob_ref))
    units = [(b, d, p) for b in range(batch) for d in range(2) for p in range(n_pairs)]
    sidx = lambda u: (u[0] * 2 + u[1]) * n_pairs + u[2]

    states = {u: s_ref[sidx(u)] for u in units}
    for step in range(cps):
        r1, r2 = {}, {}
        cc = (step, cps - 1 - step)
        for u in units:
            b, d, p = u
            r1[u] = _dot(refs[d][0][b, cc[d], :, p * PW:(p + 1) * PW], states[u].astype(BF16))
        for u in units:
            b, d, p = u
            vb = (refs[d][2][b, cc[d], :, p * PW:(p + 1) * PW].astype(F32) - r1[u][:C]).astype(BF16)
            zero = jnp.zeros_like(vb)
            v_bd = jnp.concatenate([jnp.where(first_c, vb, zero), jnp.where(first_c, zero, vb)],
                                   axis=0)
            r2[u] = _dot(refs[d][1][b, cc[d], :, p * 2 * C:(p + 1) * 2 * C], v_bd)
        for u in units:
            b, d, p = u
            refs[d][4][b, cc[d], :, p * PW:(p + 1) * PW] = (r1[u][C:] + r2[u][:C]).astype(BF16)
            e = refs[d][3][b, cc[d], 0:1, p * PW:(p + 1) * PW]
            upd = r2[u][C:]
            s = states[u]
            states[u] = jnp.concatenate(
                [s[:DN_DK] * e + jnp.where(first, upd, 0.0),
                 s[DN_DK:] * e + jnp.where(first, 0.0, upd)], axis=0)
    for u in units:
        s_ref[sidx(u)] = states[u]


def _dscan(a1, a2, au, ae, batch, seq, cps=2):
    nc = seq // CHUNK
    steps = nc // cps
    kern = functools.partial(_dscan_kernel, batch=batch, cps=cps)
    fwd = lambda i: (0, i, 0, 0, 0)
    bwd = lambda i: (0, steps - 1 - i, 1, 0, 0)

    def spec(arr, imap):
        return pl.BlockSpec((batch, cps, None) + arr.shape[3:], imap)

    o_spec = lambda imap: pl.BlockSpec((batch, cps, CHUNK, DN_VW), imap)
    o_shape = jax.ShapeDtypeStruct((batch, nc, CHUNK, DN_VW), BF16)
    return pl.pallas_call(
        kern,
        grid=(steps,),
        in_specs=[spec(a1, fwd), spec(a1, bwd), spec(a2, fwd), spec(a2, bwd),
                  spec(au, fwd), spec(au, bwd), spec(ae, fwd), spec(ae, bwd)],
        out_specs=[o_spec(lambda i: (0, i, 0, 0)), o_spec(lambda i: (0, steps - 1 - i, 0, 0))],
        out_shape=[o_shape, o_shape],
        scratch_shapes=[pltpu.VMEM((batch * 2 * (DN_HEADS // 2), 2 * DN_DK, 2 * DN_DV), F32)],
        compiler_params=pltpu.CompilerParams(dimension_semantics=("arbitrary",)),
        name="dscan",
    )(a1, a1, a2, a2, au, au, ae, ae)


def _post_kernel(oa_ref, of_ref, ob_ref, dz_ref, ga_ref, gd_ref, x_ref, p_ref,
                 wa_ref, wd_ref, wo_ref, wg_ref, wp_ref, nd_ref, npost_ref, nple_ref, o_ref):
    def rms(t, w):
        return t * lax.rsqrt(jnp.mean(t * t, axis=-1, keepdims=True) + EPS) * w

    nd = nd_ref[...]
    parts = []
    for h in range(DN_HEADS):
        sl = slice(h * DN_DV, (h + 1) * DN_DV)
        od = of_ref[:, sl].astype(F32) + ob_ref[:, sl].astype(F32)
        z = dz_ref[:, sl].astype(F32)
        parts.append((rms(od, nd) * (z * jax.nn.sigmoid(z))).astype(BF16))
    y_dn = jnp.dot(jnp.concatenate(parts, axis=1), wd_ref[...], preferred_element_type=F32)
    y_att = jnp.dot(oa_ref[...], wa_ref[...], preferred_element_type=F32)
    merged = (jax.nn.sigmoid(ga_ref[...].astype(F32)) * y_att
              + jax.nn.sigmoid(gd_ref[...].astype(F32)) * y_dn)
    mix = jnp.dot(merged.astype(BF16), wo_ref[...], preferred_element_type=F32)
    x1 = x_ref[...] + rms(mix, npost_ref[...])
    gate = jax.nn.sigmoid(jnp.dot(x1.astype(BF16), wg_ref[...], preferred_element_type=F32))
    e = jnp.dot(p_ref[...].astype(BF16), wp_ref[...], preferred_element_type=F32)
    o_ref[...] = x1 + rms(gate * e, nple_ref[...])


def _post(o_att, o_f, o_b, proj, x2, p2, wa, wd, wo, wg, wp, nd, npost, nple, offs, tm=512):
    n, d = x2.shape
    ple = p2.shape[1]
    row = lambda i: (i, 0)
    const = lambda i: (0, 0)
    return pl.pallas_call(
        _post_kernel,
        grid=(n // tm,),
        in_specs=[
            pl.BlockSpec((tm, ATT_W), row),
            pl.BlockSpec((tm, DN_VW), row),
            pl.BlockSpec((tm, DN_VW), row),
            pl.BlockSpec((tm, DN_VW), lambda i: (i, offs["dz"] // DN_VW)),
            pl.BlockSpec((tm, d), lambda i: (i, offs["ga"] // d)),
            pl.BlockSpec((tm, d), lambda i: (i, offs["gd"] // d)),
            pl.BlockSpec((tm, d), row),
            pl.BlockSpec((tm, ple), row),
            pl.BlockSpec((ATT_W, d), const),
            pl.BlockSpec((DN_VW, d), const),
            pl.BlockSpec((d, d), const),
            pl.BlockSpec((d, d), const),
            pl.BlockSpec((ple, d), const),
            pl.BlockSpec((1, DN_DV), const),
            pl.BlockSpec((1, d), const),
            pl.BlockSpec((1, d), const),
        ],
        out_specs=pl.BlockSpec((tm, d), row),
        out_shape=jax.ShapeDtypeStruct((n, d), F32),
        compiler_params=pltpu.CompilerParams(dimension_semantics=("parallel",)),
        name="post",
    )(o_att, o_f, o_b, proj, proj, proj, x2, p2, wa, wd, wo, wg, wp, nd, npost, nple)


def _rope_tables(seq):
    rows = seq // GRID_W
    row = jnp.broadcast_to(jnp.arange(rows)[:, None], (rows, GRID_W)).reshape(seq)
    col = jnp.broadcast_to(jnp.arange(GRID_W)[None, :], (rows, GRID_W)).reshape(seq)
    n_freq = HEAD_DIM // 4
    inv_freq = ROPE_THETA ** (-jnp.arange(n_freq, dtype=F32) / n_freq)
    ang = jnp.concatenate([row.astype(F32)[:, None] * inv_freq,
                           col.astype(F32)[:, None] * inv_freq], axis=-1)
    cos, sin = jnp.cos(ang), jnp.sin(ang)
    return jnp.concatenate([cos, cos], axis=-1), jnp.concatenate([-sin, sin], axis=-1)


def kernel(x, p, norm_pre, w_in, q_norm, k_norm, conv_w, a_log, dt_bias, dn_norm, w_br_att,
           w_br_dn, w_out, norm_post, w_ple_proj, w_ple_gate, ple_norm):
    batch, seq, d = x.shape
    depth = w_in.shape[0]
    n = batch * seq
    assert seq % GRID_W == 0 and seq % CHUNK == 0 and d % LANES == 0
    cos, sin = _rope_tables(seq)
    x2 = x.reshape(n, d)
    for i in range(depth):
        w_main, w_small, offs, pieces = _regroup_columns(w_in[i].astype(BF16))
        conv_w8 = jnp.pad(conv_w[i], ((0, 8 - CONV_K), (0, 0)))
        proj, small = _proj(x2, norm_pre[i][None, :], w_main, w_small, conv_w8, pieces, seq)

        tq, tk = min(ATT_TQ, seq), min(ATT_TK, seq // 2)
        qt, k_rot, vt = _aprep(proj, cos, sin, _deinterleave(q_norm[i])[None, :],
                               _deinterleave(k_norm[i])[None, :], offs, batch, seq, tq, tk)
        o_att = _attention(proj, qt, k_rot, vt, offs, batch, seq, tq, tk)

        nsm = 4 * DN_HEADS
        nck = n // CHUNK
        small_r = small[:, 2 * DN_HEADS:nsm].reshape(nck, CHUNK, 2, N_TILES, HEADS_PER_TILE)
        small_r = small_r.transpose(0, 2, 3, 4, 1).reshape(nck, 2 * N_TILES, CAT_W)
        small_r = jnp.pad(small_r, ((0, 0), (0, 8 - 2 * N_TILES), (0, 0)))
        par = jnp.stack([a_log[i].reshape(-1), dt_bias[i].reshape(-1)])
        par_col = jnp.pad(jnp.concatenate([jnp.zeros_like(par), par], axis=1),
                          ((0, 6), (0, LANES - nsm)))
        par_row = jnp.repeat(par.reshape(2, 2 * N_TILES, HEADS_PER_TILE), CHUNK, axis=2)
        par_row = jnp.pad(par_row, ((0, 0), (0, 8 - 2 * N_TILES), (0, 0)))
        a1, a2, au, ae = _dchunk(proj, small, small_r, par_col, par_row, offs, batch, seq)
        o_f, o_b = _dscan(a1, a2, au, ae, batch, seq)
        o_f, o_b = o_f.reshape(n, DN_VW), o_b.reshape(n, DN_VW)

        x2 = _post(o_att, o_f, o_b, proj, x2, p[i].reshape(n, -1),
                   w_br_att[i].astype(BF16), w_br_dn[i].astype(BF16), w_out[i].astype(BF16),
                   w_ple_gate[i].astype(BF16), w_ple_proj[i].astype(BF16),
                   dn_norm[i][None, :], norm_post[i][None, :], ple_norm[i][None, :], offs)
    return x2.reshape(batch, seq, d)
```

```python
import functools

import numpy as np
import jax
import jax.numpy as jnp
from jax import lax
from jax.experimental import pallas as pl
from jax.experimental.pallas import tpu as pltpu

F32 = jnp.float32
BF16 = jnp.bfloat16

GRID_W = 64
ATT_HEADS = 8
ATT_KV_HEADS = 2
HEAD_DIM = 128
ROPE_THETA = 10000.0
DN_HEADS = 8
DN_DK = 128
DN_DV = 128
CONV_K = 5
CHUNK = 64
EPS = 1e-6

ATT_W = ATT_HEADS * HEAD_DIM
KV_W = ATT_KV_HEADS * HEAD_DIM
DN_KW = DN_HEADS * DN_DK
DN_VW = DN_HEADS * DN_DV
GROUP = ATT_HEADS // ATT_KV_HEADS
LANES = 128


def _deinterleave(t):
    lead = t.shape[:-1]
    t = t.reshape(lead + (-1, HEAD_DIM // 2, 2))
    return jnp.swapaxes(t, -1, -2).reshape(lead + (-1,))


PROJ_TN = 4352
CONV_HALO = 16
CONV_PIECE = 512


def _regroup_columns(w):
    d_model = w.shape[0]
    sizes = (ATT_W, KV_W, KV_W, ATT_W, DN_KW, DN_KW, DN_VW, 2 * DN_HEADS, 2 * DN_HEADS, DN_VW,
             d_model, d_model)
    assert w.shape[1] == sum(sizes)
    aq, ak, av, az, dq, dk, dv, db, da, dz, ga, gd = jnp.split(w, np.cumsum(sizes)[:-1], axis=1)
    aq0, aq1 = jnp.split(_deinterleave(aq), 2, axis=1)
    az0, az1 = jnp.split(az, 2, axis=1)
    dk0, dk1 = jnp.split(dk, 2, axis=1)
    groups = [("ga", ga, "plain", 0), ("gd", gd, "plain", 0), ("dq", dq, "conv_q", 0),
              ("dk0", dk0, "conv_k", DN_KW), ("aq0", aq0, "plain", 0),
              ("ak", _deinterleave(ak), "plain", 0),
              ("av", av, "plain", 0), ("dk1", dk1, "conv_k", DN_KW + DN_KW // 2),
              ("dz", dz, "plain", 0), ("dv", dv, "conv_v", 2 * DN_KW), ("aq1", aq1, "plain", 0),
              ("az0", az0, "plain", 0), ("az1", az1, "plain", 0)]
    offsets, off = {}, 0
    pieces = [[] for _ in range(sum(g.shape[1] for _, g, _, _ in groups) // PROJ_TN)]
    for name, g, kind, ch in groups:
        offsets[name] = off
        width = g.shape[1]
        tile, local = divmod(off, PROJ_TN)
        assert local + width <= PROJ_TN
        if kind == "plain":
            last = pieces[tile][-1] if pieces[tile] else None
            if last is not None and last[2] == "plain" and last[0] + last[1] == local:
                pieces[tile][-1] = (last[0], last[1] + width, "plain", 0)
            else:
                pieces[tile].append((local, width, "plain", 0))
        else:
            for k in range(0, width, CONV_PIECE):
                pieces[tile].append((local + k, CONV_PIECE, kind, ch + k))
        off += width
    main = jnp.concatenate([g.astype(BF16) for _, g, _, _ in groups], axis=1)
    small = jnp.pad(jnp.concatenate([db, da], axis=1), ((0, 0), (0, LANES - 4 * DN_HEADS)))
    return main, small.astype(BF16), offsets, pieces


def _proj_kernel(x_ref, xp_ref, xn_ref, g_ref, w_ref, ws_ref, cw_ref, o_ref, os_ref, h_ref,
                 *, tiles_per_seq, pieces):
    i, j = pl.program_id(0), pl.program_id(1)
    tm = x_ref.shape[0]
    ext = tm + 2 * CONV_HALO
    half = CONV_K // 2

    def normed(x):
        r = lax.rsqrt(jnp.mean(x * x, axis=-1, keepdims=True) + EPS)
        return (x * r * g_ref[...]).astype(BF16)

    @pl.when(j == 0)
    def _():
        t = i % tiles_per_seq
        h = normed(x_ref[...])
        zero = jnp.zeros((CONV_HALO, x_ref.shape[1]), BF16)
        h_ref[0:CONV_HALO, :] = jnp.where(t == 0, zero, normed(xp_ref[...]))
        h_ref[CONV_HALO:CONV_HALO + tm, :] = h
        h_ref[CONV_HALO + tm:, :] = jnp.where(t == tiles_per_seq - 1, zero, normed(xn_ref[...]))
        os_ref[...] = jnp.dot(h, ws_ref[...], preferred_element_type=F32)

    def tile(tile_pieces):
        for start, width, kind, ch in tile_pieces:
            cols = slice(start, start + width)
            if kind == "plain":
                o_ref[:, cols] = jnp.dot(h_ref[CONV_HALO:CONV_HALO + tm, :], w_ref[:, cols],
                                         preferred_element_type=F32).astype(BF16)
                continue
            r = jnp.dot(h_ref[...], w_ref[:, cols], preferred_element_type=F32)
            taps = cw_ref[:, ch:ch + width]
            acc = r[CONV_HALO:CONV_HALO + tm] * taps[half:half + 1]
            for t in range(CONV_K):
                if t != half:
                    shifted = pltpu.roll(r, (half - t) % ext, axis=0)
                    acc = acc + shifted[CONV_HALO:CONV_HALO + tm] * taps[t:t + 1]
            y = acc * jax.nn.sigmoid(acc)
            for hh in range(width // DN_DK):
                yh = y[:, hh * DN_DK:(hh + 1) * DN_DK]
                if kind != "conv_v":
                    inv = lax.rsqrt(jnp.sum(yh * yh, axis=-1, keepdims=True) + EPS)
                    yh = yh * (inv * DN_DK ** -0.5 if kind == "conv_q" else inv)
                o_ref[:, start + hh * DN_DK:start + (hh + 1) * DN_DK] = yh.astype(BF16)

    for jj, tile_pieces in enumerate(pieces):
        pl.when(j == jj)(functools.partial(tile, tile_pieces))


def _proj(x2, g, w_main, w_small, conv_w8, pieces, seq, tm=512):
    n, d = x2.shape
    main_w = w_main.shape[1]
    tm = min(tm, seq)
    hb = tm // CONV_HALO
    last_hb = n // CONV_HALO - 1
    kern = functools.partial(_proj_kernel, tiles_per_seq=seq // tm, pieces=pieces)
    return pl.pallas_call(
        kern,
        grid=(n // tm, main_w // PROJ_TN),
        in_specs=[
            pl.BlockSpec((tm, d), lambda i, j: (i, 0)),
            pl.BlockSpec((CONV_HALO, d), lambda i, j: (jnp.maximum(i * hb - 1, 0), 0)),
            pl.BlockSpec((CONV_HALO, d), lambda i, j: (jnp.minimum((i + 1) * hb, last_hb), 0)),
            pl.BlockSpec((1, d), lambda i, j: (0, 0)),
            pl.BlockSpec((d, PROJ_TN), lambda i, j: (0, j)),
            pl.BlockSpec((d, LANES), lambda i, j: (0, 0)),
            pl.BlockSpec(conv_w8.shape, lambda i, j: (0, 0)),
        ],
        out_specs=[
            pl.BlockSpec((tm, PROJ_TN), lambda i, j: (i, j)),
            pl.BlockSpec((tm, LANES), lambda i, j: (i, 0)),
        ],
        out_shape=[
            jax.ShapeDtypeStruct((n, main_w), BF16),
            jax.ShapeDtypeStruct((n, LANES), F32),
        ],
        scratch_shapes=[pltpu.VMEM((tm + 2 * CONV_HALO, d), BF16)],
        compiler_params=pltpu.CompilerParams(dimension_semantics=("parallel", "arbitrary")),
        name="proj",
    )(x2, x2, x2, g, w_main, w_small, conv_w8)


def _norm_rope(xh, w, cos, sin):
    r = lax.rsqrt(jnp.mean(xh * xh, axis=-1, keepdims=True) + EPS)
    xn = xh * r * w
    return xn * cos + pltpu.roll(xn, HEAD_DIM // 2, axis=1) * sin


ATT_TQ = 1024
ATT_TK = 512
VT_ROWS = HEAD_DIM + 16


def _aprep_kernel(q0_ref, q1_ref, k_ref, v_ref, cos_ref, sin_ref, wq_ref, wk_ref,
                  qt_ref, kr_ref, vt_ref):
    cos, sin = cos_ref[...], sin_ref[...]
    scale = HEAD_DIM ** -0.5
    tm = k_ref.shape[0]
    tk = vt_ref.shape[-1]
    for h, q_ref in enumerate((q0_ref, q1_ref)):
        sl = slice(h * HEAD_DIM, (h + 1) * HEAD_DIM)
        kr_ref[:, sl] = _norm_rope(k_ref[:, sl].astype(F32), wk_ref[...], cos, sin).astype(BF16)
        for c in range(tm // tk):
            vt_ref[h, c, 0:HEAD_DIM, :] = v_ref[c * tk:(c + 1) * tk, sl].astype(F32).T.astype(BF16)
            vt_ref[h, c, HEAD_DIM:, :] = jnp.ones((VT_ROWS - HEAD_DIM, tk), BF16)
        for g in range(GROUP):
            qs = slice(g * HEAD_DIM, (g + 1) * HEAD_DIM)
            qr = _norm_rope(q_ref[:, qs].astype(F32), wq_ref[...], cos, sin) * scale
            qt_ref[h, :, g * tm:(g + 1) * tm] = qr.T.astype(BF16)


def _aprep(proj, cos, sin, wq, wk, offs, batch, seq, tm, tk):
    assert ATT_KV_HEADS == 2
    n = proj.shape[0]
    t = seq // tm
    cpt = tm // tk
    gw = GROUP * HEAD_DIM
    return pl.pallas_call(
        _aprep_kernel,
        grid=(batch, t),
        in_specs=[
            pl.BlockSpec((tm, gw), lambda b, i: (b * t + i, offs["aq0"] // gw)),
            pl.BlockSpec((tm, gw), lambda b, i: (b * t + i, offs["aq1"] // gw)),
            pl.BlockSpec((tm, KV_W), lambda b, i: (b * t + i, offs["ak"] // KV_W)),
            pl.BlockSpec((tm, KV_W), lambda b, i: (b * t + i, offs["av"] // KV_W)),
            pl.BlockSpec((tm, HEAD_DIM), lambda b, i: (i, 0)),
            pl.BlockSpec((tm, HEAD_DIM), lambda b, i: (i, 0)),
            pl.BlockSpec((1, HEAD_DIM), lambda b, i: (0, 0)),
            pl.BlockSpec((1, HEAD_DIM), lambda b, i: (0, 0)),
        ],
        out_specs=[
            pl.BlockSpec((None, None, ATT_KV_HEADS, HEAD_DIM, GROUP * tm),
                         lambda b, i: (b, i, 0, 0, 0)),
            pl.BlockSpec((tm, KV_W), lambda b, i: (b * t + i, 0)),
            pl.BlockSpec((None, ATT_KV_HEADS, cpt, VT_ROWS, tk), lambda b, i: (b, 0, i, 0, 0)),
        ],
        out_shape=[
            jax.ShapeDtypeStruct((batch, t, ATT_KV_HEADS, HEAD_DIM, GROUP * tm), BF16),
            jax.ShapeDtypeStruct((n, KV_W), BF16),
            jax.ShapeDtypeStruct((batch, ATT_KV_HEADS, seq // tk, VT_ROWS, tk), BF16),
        ],
        compiler_params=pltpu.CompilerParams(dimension_semantics=("parallel", "parallel")),
        name="aprep",
    )(proj, proj, proj, proj, cos, sin, wq, wk)


def _attn_kernel(qt_ref, z_ref, k_ref, vt_ref, o_ref, acc_ref, s0_ref, s1_ref, p0_ref, p1_ref,
                 *, tq, tk, seq):
    cols = GROUP * tq
    n_chunks = seq // tk
    qt = qt_ref[...]

    def scores(c, s_ref):
        start = pl.multiple_of(c * tk, tk)
        s = jnp.dot(k_ref[pl.ds(start, tk), :], qt, preferred_element_type=F32)
        s_ref[...] = s
        return jnp.max(s, axis=0, keepdims=True)

    def softmax(s_ref, p_ref, m, smax):
        m_new = jnp.maximum(m, smax)
        p_ref[...] = jnp.exp((s_ref[...] - m_new).astype(BF16))
        return m_new, jnp.exp(m - m_new)

    def accumulate(c, p_ref, alpha):
        acc_ref[...] = alpha * acc_ref[...] + jnp.dot(vt_ref[c], p_ref[...],
                                                      preferred_element_type=F32)

    def pair(c, carry, last):
        m, alpha_prev, smax0 = carry
        m, alpha0 = softmax(s0_ref, p0_ref, m, smax0)
        smax1 = scores(c + 1, s1_ref)
        accumulate(jnp.maximum(c - 1, 0), p1_ref, alpha_prev)
        m, alpha1 = softmax(s1_ref, p1_ref, m, smax1)
        if not last:
            smax0 = scores(c + 2, s0_ref)
        accumulate(c, p0_ref, alpha0)
        return m, alpha1, smax0

    acc_ref[...] = jnp.zeros(acc_ref.shape, F32)
    p1_ref[...] = jnp.zeros(p1_ref.shape, BF16)
    smax = scores(0, s0_ref)
    carry = (jnp.full((1, cols), -jnp.inf, F32), jnp.ones((1, cols), F32), smax)
    carry = lax.fori_loop(0, n_chunks // 2 - 1, lambda j, cr: pair(2 * j, cr, False), carry)
    _, alpha, _ = pair(n_chunks - 2, carry, True)
    accumulate(n_chunks - 1, p1_ref, alpha)
    acc = acc_ref[...]
    o_t = acc[:HEAD_DIM] / acc[HEAD_DIM:HEAD_DIM + 1]
    for g in range(GROUP):
        sl = slice(g * HEAD_DIM, (g + 1) * HEAD_DIM)
        z = z_ref[:, sl].astype(F32)
        o_ref[:, sl] = (o_t[:, g * tq:(g + 1) * tq].T * (z * jax.nn.sigmoid(z))).astype(BF16)


def _attention(proj, qt, k_rot, vt, offs, batch, seq, tq, tk):
    n = proj.shape[0]
    gw = GROUP * HEAD_DIM
    t = seq // tq
    cols = GROUP * tq
    assert (seq // tk) % 2 == 0 and offs["az1"] == offs["az0"] + gw
    kern = functools.partial(_attn_kernel, tq=tq, tk=tk, seq=seq)
    return pl.pallas_call(
        kern,
        grid=(batch, ATT_KV_HEADS, t),
        in_specs=[
            pl.BlockSpec((None, None, None, HEAD_DIM, cols), lambda b, h, i: (b, i, h, 0, 0)),
            pl.BlockSpec((tq, gw), lambda b, h, i: (b * t + i, offs["az0"] // gw + h)),
            pl.BlockSpec((seq, HEAD_DIM), lambda b, h, i: (b, h)),
            pl.BlockSpec((None, None, seq // tk, VT_ROWS, tk), lambda b, h, i: (b, h, 0, 0, 0)),
        ],
        out_specs=pl.BlockSpec((tq, gw), lambda b, h, i: (b * t + i, h)),
        out_shape=jax.ShapeDtypeStruct((n, ATT_W), BF16),
        scratch_shapes=[pltpu.VMEM((VT_ROWS, cols), F32),
                        pltpu.VMEM((tk, cols), F32), pltpu.VMEM((tk, cols), F32),
                        pltpu.VMEM((tk, cols), BF16), pltpu.VMEM((tk, cols), BF16)],
        compiler_params=pltpu.CompilerParams(
            dimension_semantics=("parallel", "parallel", "arbitrary")),
        name="attn",
    )(qt, proj, k_rot, vt)


HEADS_PER_TILE = 4
CAT_W = HEADS_PER_TILE * CHUNK
N_TILES = DN_HEADS // HEADS_PER_TILE


def _split3(x):
    p1 = x.astype(BF16)
    r1 = x - p1.astype(F32)
    p2 = r1.astype(BF16)
    p3 = (r1 - p2.astype(F32)).astype(BF16)
    return p1, p2, p3


def _softplus(x):
    return jnp.maximum(x, 0.0) + jnp.log(1.0 + jnp.exp(-jnp.abs(x)))


def _dot(a, b):
    return jnp.dot(a, b, preferred_element_type=F32)


def _dchunk_kernel(q0_ref, q1_ref, k0_ref, k1_ref, v0_ref, v1_ref, c_ref, r_ref, pc_ref, pr_ref,
                   a1_ref, a2_ref, au_ref, ae_ref, *, cpb):
    C, W, T = CHUNK, CAT_W, HEADS_PER_TILE
    rows = cpb * C

    def cum_matrix(n, lower):
        i = lax.broadcasted_iota(jnp.int32, (n, n), 0)
        j = lax.broadcasted_iota(jnp.int32, (n, n), 1)
        same = (i >> 6) == (j >> 6)
        return (same & ((i >= j) if lower else (i <= j))).astype(BF16)

    col_lower, col_upper = cum_matrix(rows, True), cum_matrix(rows, False)
    row_lower, row_upper = cum_matrix(W, True), cum_matrix(W, False)

    blk = c_ref[...]
    beta_all = jax.nn.sigmoid(blk)
    g_parts = _split3(-jnp.exp(pc_ref[0:1, :]) * _softplus(blk + pc_ref[1:2, :]))
    gcol = (sum(_dot(col_lower, p) for p in g_parts),
            sum(_dot(col_upper, p) for p in g_parts))
    blk_r = r_ref[...]
    g_r = (-jnp.exp(pr_ref[0][None]) * _softplus(blk_r + pr_ref[1][None])).reshape(cpb * 8, W)
    gr_parts = _split3(g_r)
    grow = (sum(_dot(p, row_upper) for p in gr_parts),
            sum(_dot(p, row_lower) for p in gr_parts))

    ri = lax.broadcasted_iota(jnp.int32, (C, W), 0)
    li = lax.broadcasted_iota(jnp.int32, (C, W), 1)
    ci = li & (C - 1)
    lb = li >> 6
    incl = (ri >= ci, ri <= ci)
    offdiag = ri != ci
    eye = (ri == ci).astype(F32)
    same = {s: (ri >> lg) == (ci >> lg) for lg, s in ((1, 2), (2, 4), (3, 8), (4, 16), (5, 32))}
    level_masks = [same[4] & ~same[2], same[8] & ~same[4], same[16] & ~same[8],
                   same[32] & ~same[16], ~same[32]]
    tile_sel = [lb == t for t in range(T)]
    head_of_lane = lax.broadcasted_iota(jnp.int32, (C, T * DN_DK), 1) >> 7

    def block_diag(y):
        zero = jnp.zeros_like(y)
        return jnp.concatenate([jnp.where(tile_sel[t], y, zero) for t in range(T)], axis=0)

    def cat_bcast(cols):
        out = jnp.broadcast_to(cols[T - 1], (C, W))
        for t in range(T - 2, -1, -1):
            out = jnp.where(tile_sel[t], jnp.broadcast_to(cols[t], (C, W)), out)
        return out

    pairs = [(c, t) for c in range(cpb) for t in range(N_TILES)]
    units = [(c, t, d) for c in range(cpb) for t in range(N_TILES) for d in range(2)]

    qk, kk, q4s, k4s, v4s = {}, {}, {}, {}, {}
    for c, t in pairs:
        rs = slice(c * C, (c + 1) * C)
        q4 = (q0_ref, q1_ref)[t][rs, :]
        k4 = (k0_ref, k1_ref)[t][rs, :]
        v4 = (v0_ref, v1_ref)[t][rs, :]
        zero = jnp.zeros_like(k4)
        k_bd = jnp.concatenate([jnp.where(head_of_lane == h, k4, zero) for h in range(T)], axis=0)
        r = lax.dot_general(jnp.concatenate([q4, k4], axis=0), k_bd, (((1,), (1,)), ((), ())),
                            preferred_element_type=F32)
        qk[c, t], kk[c, t] = r[:C], r[C:]
        q4s[c, t], k4s[c, t], v4s[c, t] = q4, k4, v4

    lmat, amat, xinv = {}, {}, {}
    for c, t, d in units:
        rs = slice(c * C, (c + 1) * C)
        base = d * DN_HEADS + t * T
        gc = cat_bcast([gcol[d][rs, 2 * DN_HEADS + base + h:2 * DN_HEADS + base + h + 1]
                        for h in range(T)])
        beta = cat_bcast([beta_all[rs, base + h:base + h + 1] for h in range(T)])
        row = c * 8 + d * N_TILES + t
        gr = jnp.broadcast_to(grow[d][row:row + 1, :], (C, W))
        decay = jnp.exp(jnp.where(incl[d], gc - gr, -1e30))
        l = jnp.where(offdiag, beta * kk[c, t] * decay, 0.0)
        lmat[c, t, d] = l
        amat[c, t, d] = qk[c, t] * decay
        xinv[c, t, d] = eye - jnp.where(same[2], l, 0.0)

    for msk in level_masks:
        ys, xbs = {}, {}
        for u in units:
            e = jnp.where(msk, lmat[u], 0.0).astype(BF16)
            xbs[u] = xinv[u].astype(BF16)
            ys[u] = _dot(e, block_diag(xbs[u]))
        for u in units:
            xinv[u] = xinv[u] - _dot(xbs[u], block_diag(ys[u].astype(BF16)))

    for c, t, d in units:
        rs = slice(c * C, (c + 1) * C)
        kd_rows, rhs_rows = [], []
        for h in range(T):
            head = t * T + h
            col = d * DN_HEADS + head
            hs = slice(h * DN_DK, (h + 1) * DN_DK)
            out_s = slice(head * DN_DK, (head + 1) * DN_DK)
            gc = jnp.broadcast_to(gcol[d][rs, 2 * DN_HEADS + col:2 * DN_HEADS + col + 1],
                                  (C, DN_DK))
            beta = jnp.broadcast_to(beta_all[rs, col:col + 1], (C, DN_DK))
            g_last = gc[C - 1:C, :] if d == 0 else gc[0:1, :]
            eg = jnp.exp(gc)
            kf = k4s[c, t][:, hs].astype(F32)
            rhs = jnp.concatenate([kf * (beta * eg), v4s[c, t][:, hs].astype(F32) * beta],
                                  axis=1).astype(BF16)
            zero = jnp.zeros_like(rhs)
            rhs_rows.append(jnp.concatenate([rhs if hh == h else zero for hh in range(T)], axis=1))
            kd_rows.append(kf * jnp.exp(g_last - gc))
            a1_ref[c, d, C:2 * C, out_s] = (q4s[c, t][:, hs].astype(F32) * eg).astype(BF16)
            ae_ref[c, d, :, out_s] = jnp.exp(g_last)
        wu = _dot(xinv[c, t, d].astype(BF16), jnp.concatenate(rhs_rows, axis=0))
        for h in range(T):
            head = t * T + h
            out_s = slice(head * DN_DK, (head + 1) * DN_DK)
            a1_ref[c, d, 0:C, out_s] = wu[:, 2 * h * DN_DK:(2 * h + 1) * DN_DK].astype(BF16)
            au_ref[c, d, :, out_s] = wu[:, (2 * h + 1) * DN_DK:(2 * h + 2) * DN_DK].astype(BF16)
        cat_s = slice(t * W, (t + 1) * W)
        a2_ref[c, d, 0:C, cat_s] = amat[c, t, d].astype(BF16)
        a2_ref[c, d, C:C + DN_DK, cat_s] = jnp.concatenate(kd_rows, axis=0).T.astype(BF16)


def _dchunk(proj, small, small_r, par_col, par_row, offs, batch, seq, cpb=4):
    assert N_TILES == 2
    nc = seq // CHUNK
    steps = nc // cpb
    tw = HEADS_PER_TILE * DN_DK
    kern = functools.partial(_dchunk_kernel, cpb=cpb)
    out5 = lambda b, i: (b, i, 0, 0, 0)
    tile_cols = [offs["dq"], offs["dq"] + tw, offs["dk0"], offs["dk1"], offs["dv"], offs["dv"] + tw]
    qkv_specs = [pl.BlockSpec((cpb * CHUNK, tw), functools.partial(
        lambda b, i, col: (b * steps + i, col), col=c // tw)) for c in tile_cols]
    return pl.pallas_call(
        kern,
        grid=(batch, steps),
        in_specs=qkv_specs + [
            pl.BlockSpec((cpb * CHUNK, LANES), lambda b, i: (b * steps + i, 0)),
            pl.BlockSpec((cpb, 8, CAT_W), lambda b, i: (b * steps + i, 0, 0)),
            pl.BlockSpec((8, LANES), lambda b, i: (0, 0)),
            pl.BlockSpec((2, 8, CAT_W), lambda b, i: (0, 0, 0)),
        ],
        out_specs=[
            pl.BlockSpec((None, cpb, 2, 2 * CHUNK, DN_KW), out5),
            pl.BlockSpec((None, cpb, 2, CHUNK + DN_DK, DN_HEADS * CHUNK), out5),
            pl.BlockSpec((None, cpb, 2, CHUNK, DN_VW), out5),
            pl.BlockSpec((None, cpb, 2, 1, DN_VW), out5),
        ],
        out_shape=[
            jax.ShapeDtypeStruct((batch, nc, 2, 2 * CHUNK, DN_KW), BF16),
            jax.ShapeDtypeStruct((batch, nc, 2, CHUNK + DN_DK, DN_HEADS * CHUNK), BF16),
            jax.ShapeDtypeStruct((batch, nc, 2, CHUNK, DN_VW), BF16),
            jax.ShapeDtypeStruct((batch, nc, 2, 1, DN_VW), F32),
        ],
        compiler_params=pltpu.CompilerParams(dimension_semantics=("parallel", "parallel")),
        name="dchunk",
    )(proj, proj, proj, proj, proj, proj, small, small_r, par_col, par_row)


def _dscan_kernel(a1f_ref, a1b_ref, a2f_ref, a2b_ref, auf_ref, aub_ref, aef_ref, aeb_ref,
                  of_ref, ob_ref, s_ref, *, batch, cps):
    C = CHUNK
    PW = 2 * DN_DV
    n_pairs = DN_HEADS // 2

    @pl.when(pl.program_id(0) == 0)
    def _():
        s_ref[...] = jnp.zeros(s_ref.shape, F32)

    first = lax.broadcasted_iota(jnp.int32, (DN_DK, PW), 1) < DN_DV
    first_c = lax.broadcasted_iota(jnp.int32, (C, PW), 1) < DN_DV
    refs = ((a1f_ref, a2f_ref, auf_ref, aef_ref, of_ref), (a1b_ref, a2b_ref, aub_ref, aeb_ref, ob_ref))
    units = [(b, d, p) for b in range(batch) for d in range(2) for p in range(n_pairs)]
    sidx = lambda u: (u[0] * 2 + u[1]) * n_pairs + u[2]

    states = {u: s_ref[sidx(u)] for u in units}
    for step in range(cps):
        r1, r2 = {}, {}
        cc = (step, cps - 1 - step)
        for u in units:
            b, d, p = u
            r1[u] = _dot(refs[d][0][b, cc[d], :, p * PW:(p + 1) * PW], states[u].astype(BF16))
        for u in units:
            b, d, p = u
            vb = (refs[d][2][b, cc[d], :, p * PW:(p + 1) * PW].astype(F32) - r1[u][:C]).astype(BF16)
            zero = jnp.zeros_like(vb)
            v_bd = jnp.concatenate([jnp.where(first_c, vb, zero), jnp.where(first_c, zero, vb)],
                                   axis=0)
            r2[u] = _dot(refs[d][1][b, cc[d], :, p * 2 * C:(p + 1) * 2 * C], v_bd)
        for u in units:
            b, d, p = u
            refs[d][4][b, cc[d], :, p * PW:(p + 1) * PW] = (r1[u][C:] + r2[u][:C]).astype(BF16)
            e = refs[d][3][b, cc[d], 0:1, p * PW:(p + 1) * PW]
            upd = r2[u][C:]
            s = states[u]
            states[u] = jnp.concatenate(
                [s[:DN_DK] * e + jnp.where(first, upd, 0.0),
                 s[DN_DK:] * e + jnp.where(first, 0.0, upd)], axis=0)
    for u in units:
        s_ref[sidx(u)] = states[u]


def _dscan(a1, a2, au, ae, batch, seq, cps=2):
    nc = seq // CHUNK
    steps = nc // cps
    kern = functools.partial(_dscan_kernel, batch=batch, cps=cps)
    fwd = lambda i: (0, i, 0, 0, 0)
    bwd = lambda i: (0, steps - 1 - i, 1, 0, 0)

    def spec(arr, imap):
        return pl.BlockSpec((batch, cps, None) + arr.shape[3:], imap)

    o_spec = lambda imap: pl.BlockSpec((batch, cps, CHUNK, DN_VW), imap)
    o_shape = jax.ShapeDtypeStruct((batch, nc, CHUNK, DN_VW), BF16)
    return pl.pallas_call(
        kern,
        grid=(steps,),
        in_specs=[spec(a1, fwd), spec(a1, bwd), spec(a2, fwd), spec(a2, bwd),
                  spec(au, fwd), spec(au, bwd), spec(ae, fwd), spec(ae, bwd)],
        out_specs=[o_spec(lambda i: (0, i, 0, 0)), o_spec(lambda i: (0, steps - 1 - i, 0, 0))],
        out_shape=[o_shape, o_shape],
        scratch_shapes=[pltpu.VMEM((batch * 2 * (DN_HEADS // 2), 2 * DN_DK, 2 * DN_DV), F32)],
        compiler_params=pltpu.CompilerParams(dimension_semantics=("arbitrary",)),
        name="dscan",
    )(a1, a1, a2, a2, au, au, ae, ae)


def _post_kernel(oa_ref, of_ref, ob_ref, dz_ref, ga_ref, gd_ref, x_ref, p_ref,
                 wa_ref, wd_ref, wo_ref, wg_ref, wp_ref, nd_ref, npost_ref, nple_ref, o_ref):
    def rms(t, w):
        return t * lax.rsqrt(jnp.mean(t * t, axis=-1, keepdims=True) + EPS) * w

    nd = nd_ref[...]
    parts = []
    for h in range(DN_HEADS):
        sl = slice(h * DN_DV, (h + 1) * DN_DV)
        od = of_ref[:, sl].astype(F32) + ob_ref[:, sl].astype(F32)
        z = dz_ref[:, sl].astype(F32)
        parts.append((rms(od, nd) * (z * jax.nn.sigmoid(z))).astype(BF16))
    y_dn = jnp.dot(jnp.concatenate(parts, axis=1), wd_ref[...], preferred_element_type=F32)
    y_att = jnp.dot(oa_ref[...], wa_ref[...], preferred_element_type=F32)
    merged = (jax.nn.sigmoid(ga_ref[...].astype(F32)) * y_att
              + jax.nn.sigmoid(gd_ref[...].astype(F32)) * y_dn)
    mix = jnp.dot(merged.astype(BF16), wo_ref[...], preferred_element_type=F32)
    x1 = x_ref[...] + rms(mix, npost_ref[...])
    gate = jax.nn.sigmoid(jnp.dot(x1.astype(BF16), wg_ref[...], preferred_element_type=F32))
    e = jnp.dot(p_ref[...].astype(BF16), wp_ref[...], preferred_element_type=F32)
    o_ref[...] = x1 + rms(gate * e, nple_ref[...])


def _post(o_att, o_f, o_b, proj, x2, p2, wa, wd, wo, wg, wp, nd, npost, nple, offs, tm=512):
    n, d = x2.shape
    ple = p2.shape[1]
    row = lambda i: (i, 0)
    const = lambda i: (0, 0)
    return pl.pallas_call(
        _post_kernel,
        grid=(n // tm,),
        in_specs=[
            pl.BlockSpec((tm, ATT_W), row),
            pl.BlockSpec((tm, DN_VW), row),
            pl.BlockSpec((tm, DN_VW), row),
            pl.BlockSpec((tm, DN_VW), lambda i: (i, offs["dz"] // DN_VW)),
            pl.BlockSpec((tm, d), lambda i: (i, offs["ga"] // d)),
            pl.BlockSpec((tm, d), lambda i: (i, offs["gd"] // d)),
            pl.BlockSpec((tm, d), row),
            pl.BlockSpec((tm, ple), row),
            pl.BlockSpec((ATT_W, d), const),
            pl.BlockSpec((DN_VW, d), const),
            pl.BlockSpec((d, d), const),
            pl.BlockSpec((d, d), const),
            pl.BlockSpec((ple, d), const),
            pl.BlockSpec((1, DN_DV), const),
            pl.BlockSpec((1, d), const),
            pl.BlockSpec((1, d), const),
        ],
        out_specs=pl.BlockSpec((tm, d), row),
        out_shape=jax.ShapeDtypeStruct((n, d), F32),
        compiler_params=pltpu.CompilerParams(dimension_semantics=("parallel",)),
        name="post",
    )(o_att, o_f, o_b, proj, proj, proj, x2, p2, wa, wd, wo, wg, wp, nd, npost, nple)


def _rope_tables(seq):
    rows = seq // GRID_W
    row = np.broadcast_to(np.arange(rows)[:, None], (rows, GRID_W)).reshape(seq)
    col = np.broadcast_to(np.arange(GRID_W)[None, :], (rows, GRID_W)).reshape(seq)
    n_freq = HEAD_DIM // 4
    inv_freq = np.float32(ROPE_THETA) ** (-np.arange(n_freq, dtype=np.float32) / np.float32(n_freq))
    ang = np.concatenate([row.astype(np.float32)[:, None] * inv_freq,
                          col.astype(np.float32)[:, None] * inv_freq], axis=-1)
    cos, sin = np.cos(ang), np.sin(ang)
    return (jnp.asarray(np.concatenate([cos, cos], axis=-1), F32),
            jnp.asarray(np.concatenate([-sin, sin], axis=-1), F32))


def kernel(x, p, norm_pre, w_in, q_norm, k_norm, conv_w, a_log, dt_bias, dn_norm, w_br_att,
           w_br_dn, w_out, norm_post, w_ple_proj, w_ple_gate, ple_norm):
    batch, seq, d = x.shape
    depth = w_in.shape[0]
    n = batch * seq
    assert seq % GRID_W == 0 and seq % CHUNK == 0 and d % LANES == 0
    cos, sin = _rope_tables(seq)
    x2 = x.reshape(n, d)
    for i in range(depth):
        w_main, w_small, offs, pieces = _regroup_columns(w_in[i])
        conv_w8 = jnp.pad(conv_w[i], ((0, 8 - CONV_K), (0, 0)))
        proj, small = _proj(x2, norm_pre[i][None, :], w_main, w_small, conv_w8, pieces, seq)

        tq, tk = min(ATT_TQ, seq), min(ATT_TK, seq // 2)
        qt, k_rot, vt = _aprep(proj, cos, sin, _deinterleave(q_norm[i])[None, :],
                               _deinterleave(k_norm[i])[None, :], offs, batch, seq, tq, tk)
        o_att = _attention(proj, qt, k_rot, vt, offs, batch, seq, tq, tk)

        nsm = 4 * DN_HEADS
        nck = n // CHUNK
        small_r = small[:, 2 * DN_HEADS:nsm].reshape(nck, CHUNK, 2, N_TILES, HEADS_PER_TILE)
        small_r = small_r.transpose(0, 2, 3, 4, 1).reshape(nck, 2 * N_TILES, CAT_W)
        small_r = jnp.pad(small_r, ((0, 0), (0, 8 - 2 * N_TILES), (0, 0)))
        par = jnp.stack([a_log[i].reshape(-1), dt_bias[i].reshape(-1)])
        par_col = jnp.pad(jnp.concatenate([jnp.zeros_like(par), par], axis=1),
                          ((0, 6), (0, LANES - nsm)))
        par_row = jnp.repeat(par.reshape(2, 2 * N_TILES, HEADS_PER_TILE), CHUNK, axis=2)
        par_row = jnp.pad(par_row, ((0, 0), (0, 8 - 2 * N_TILES), (0, 0)))
        a1, a2, au, ae = _dchunk(proj, small, small_r, par_col, par_row, offs, batch, seq)
        o_f, o_b = _dscan(a1, a2, au, ae, batch, seq)
        o_f, o_b = o_f.reshape(n, DN_VW), o_b.reshape(n, DN_VW)

        x2 = _post(o_att, o_f, o_b, proj, x2, p[i].reshape(n, -1),
                   w_br_att[i].astype(BF16), w_br_dn[i].astype(BF16), w_out[i].astype(BF16),
                   w_ple_gate[i].astype(BF16), w_ple_proj[i].astype(BF16),
                   dn_norm[i][None, :], norm_post[i][None, :], ple_norm[i][None, :], offs)
    return x2.reshape(batch, seq, d)
```

```python
import functools

import numpy as np
import jax
import jax.numpy as jnp
from jax import lax
from jax.experimental import pallas as pl
from jax.experimental.pallas import tpu as pltpu

F32 = jnp.float32
BF16 = jnp.bfloat16

GRID_W = 64
ATT_HEADS = 8
ATT_KV_HEADS = 2
HEAD_DIM = 128
ROPE_THETA = 10000.0
DN_HEADS = 8
DN_DK = 128
DN_DV = 128
CONV_K = 5
CHUNK = 64
EPS = 1e-6

ATT_W = ATT_HEADS * HEAD_DIM
KV_W = ATT_KV_HEADS * HEAD_DIM
DN_KW = DN_HEADS * DN_DK
DN_VW = DN_HEADS * DN_DV
GROUP = ATT_HEADS // ATT_KV_HEADS
LANES = 128


def _deinterleave(t):
    lead = t.shape[:-1]
    t = t.reshape(lead + (-1, HEAD_DIM // 2, 2))
    return jnp.swapaxes(t, -1, -2).reshape(lead + (-1,))


PROJ_TN = 4352
CONV_HALO = 16
CONV_PIECE = 512


def _regroup_columns(w):
    d_model = w.shape[0]
    sizes = (ATT_W, KV_W, KV_W, ATT_W, DN_KW, DN_KW, DN_VW, 2 * DN_HEADS, 2 * DN_HEADS, DN_VW,
             d_model, d_model)
    assert w.shape[1] == sum(sizes)
    aq, ak, av, az, dq, dk, dv, db, da, dz, ga, gd = jnp.split(w, np.cumsum(sizes)[:-1], axis=1)
    aq0, aq1 = jnp.split(_deinterleave(aq), 2, axis=1)
    az0, az1 = jnp.split(az, 2, axis=1)
    dk0, dk1 = jnp.split(dk, 2, axis=1)
    groups = [("ga", ga, "plain", 0), ("gd", gd, "plain", 0), ("dq", dq, "conv_q", 0),
              ("dk0", dk0, "conv_k", DN_KW), ("aq0", aq0, "plain", 0),
              ("ak", _deinterleave(ak), "plain", 0),
              ("av", av, "plain", 0), ("dk1", dk1, "conv_k", DN_KW + DN_KW // 2),
              ("dz", dz, "plain", 0), ("dv", dv, "conv_v", 2 * DN_KW), ("aq1", aq1, "plain", 0),
              ("az0", az0, "plain", 0), ("az1", az1, "plain", 0)]
    offsets, off = {}, 0
    pieces = [[] for _ in range(sum(g.shape[1] for _, g, _, _ in groups) // PROJ_TN)]
    for name, g, kind, ch in groups:
        offsets[name] = off
        width = g.shape[1]
        tile, local = divmod(off, PROJ_TN)
        assert local + width <= PROJ_TN
        if kind == "plain":
            last = pieces[tile][-1] if pieces[tile] else None
            if last is not None and last[2] == "plain" and last[0] + last[1] == local:
                pieces[tile][-1] = (last[0], last[1] + width, "plain", 0)
            else:
                pieces[tile].append((local, width, "plain", 0))
        else:
            for k in range(0, width, CONV_PIECE):
                pieces[tile].append((local + k, CONV_PIECE, kind, ch + k))
        off += width
    main = jnp.concatenate([g.astype(BF16) for _, g, _, _ in groups], axis=1)
    small = jnp.pad(jnp.concatenate([db, da], axis=1), ((0, 0), (0, LANES - 4 * DN_HEADS)))
    return main, small.astype(BF16), offsets, pieces


def _proj_kernel(x_ref, xp_ref, xn_ref, g_ref, w_ref, ws_ref, cw_ref, o_ref, os_ref, h_ref,
                 *, tiles_per_seq, pieces):
    i, j = pl.program_id(0), pl.program_id(1)
    tm = x_ref.shape[0]
    ext = tm + 2 * CONV_HALO
    half = CONV_K // 2

    def normed(x):
        r = lax.rsqrt(jnp.mean(x * x, axis=-1, keepdims=True) + EPS)
        return (x * r * g_ref[...]).astype(BF16)

    @pl.when(j == 0)
    def _():
        t = i % tiles_per_seq
        h = normed(x_ref[...])
        zero = jnp.zeros((CONV_HALO, x_ref.shape[1]), BF16)
        h_ref[0:CONV_HALO, :] = jnp.where(t == 0, zero, normed(xp_ref[...]))
        h_ref[CONV_HALO:CONV_HALO + tm, :] = h
        h_ref[CONV_HALO + tm:, :] = jnp.where(t == tiles_per_seq - 1, zero, normed(xn_ref[...]))
        os_ref[...] = jnp.dot(h, ws_ref[...], preferred_element_type=F32)

    def tile(tile_pieces):
        for start, width, kind, ch in tile_pieces:
            cols = slice(start, start + width)
            if kind == "plain":
                o_ref[:, cols] = jnp.dot(h_ref[CONV_HALO:CONV_HALO + tm, :], w_ref[:, cols],
                                         preferred_element_type=F32).astype(BF16)
                continue
            r = jnp.dot(h_ref[...], w_ref[:, cols], preferred_element_type=F32)
            taps = cw_ref[:, ch:ch + width]
            acc = r[CONV_HALO:CONV_HALO + tm] * taps[half:half + 1]
            for t in range(CONV_K):
                if t != half:
                    shifted = pltpu.roll(r, (half - t) % ext, axis=0)
                    acc = acc + shifted[CONV_HALO:CONV_HALO + tm] * taps[t:t + 1]
            hx = 0.5 * acc
            y = hx * jnp.tanh(hx) + hx
            for hh in range(width // DN_DK):
                yh = y[:, hh * DN_DK:(hh + 1) * DN_DK]
                if kind != "conv_v":
                    inv = lax.rsqrt(jnp.sum(yh * yh, axis=-1, keepdims=True) + EPS)
                    yh = yh * (inv * DN_DK ** -0.5 if kind == "conv_q" else inv)
                o_ref[:, start + hh * DN_DK:start + (hh + 1) * DN_DK] = yh.astype(BF16)

    for jj, tile_pieces in enumerate(pieces):
        pl.when(j == jj)(functools.partial(tile, tile_pieces))


def _proj(x2, g, w_main, w_small, conv_w8, pieces, seq, tm=512):
    n, d = x2.shape
    main_w = w_main.shape[1]
    tm = min(tm, seq)
    hb = tm // CONV_HALO
    last_hb = n // CONV_HALO - 1
    kern = functools.partial(_proj_kernel, tiles_per_seq=seq // tm, pieces=pieces)
    return pl.pallas_call(
        kern,
        grid=(n // tm, main_w // PROJ_TN),
        in_specs=[
            pl.BlockSpec((tm, d), lambda i, j: (i, 0)),
            pl.BlockSpec((CONV_HALO, d), lambda i, j: (jnp.maximum(i * hb - 1, 0), 0)),
            pl.BlockSpec((CONV_HALO, d), lambda i, j: (jnp.minimum((i + 1) * hb, last_hb), 0)),
            pl.BlockSpec((1, d), lambda i, j: (0, 0)),
            pl.BlockSpec((d, PROJ_TN), lambda i, j: (0, j)),
            pl.BlockSpec((d, LANES), lambda i, j: (0, 0)),
            pl.BlockSpec(conv_w8.shape, lambda i, j: (0, 0)),
        ],
        out_specs=[
            pl.BlockSpec((tm, PROJ_TN), lambda i, j: (i, j)),
            pl.BlockSpec((tm, LANES), lambda i, j: (i, 0)),
        ],
        out_shape=[
            jax.ShapeDtypeStruct((n, main_w), BF16),
            jax.ShapeDtypeStruct((n, LANES), F32),
        ],
        scratch_shapes=[pltpu.VMEM((tm + 2 * CONV_HALO, d), BF16)],
        compiler_params=pltpu.CompilerParams(dimension_semantics=("parallel", "arbitrary")),
        name="proj",
    )(x2, x2, x2, g, w_main, w_small, conv_w8)


def _norm_rope(xh, w, cos, sin):
    r = lax.rsqrt(jnp.mean(xh * xh, axis=-1, keepdims=True) + EPS)
    xn = xh * r * w
    return xn * cos + pltpu.roll(xn, HEAD_DIM // 2, axis=1) * sin


ATT_TQ = 1024
ATT_TK = 512
VT_ROWS = HEAD_DIM + 16


def _aprep_kernel(q0_ref, q1_ref, k_ref, v_ref, cos_ref, sin_ref, wq_ref, wk_ref,
                  qt_ref, kr_ref, vt_ref):
    cos, sin = cos_ref[...], sin_ref[...]
    scale = HEAD_DIM ** -0.5
    tm = k_ref.shape[0]
    tk = vt_ref.shape[-1]
    for h, q_ref in enumerate((q0_ref, q1_ref)):
        sl = slice(h * HEAD_DIM, (h + 1) * HEAD_DIM)
        kr_ref[:, sl] = _norm_rope(k_ref[:, sl].astype(F32), wk_ref[...], cos, sin).astype(BF16)
        for c in range(tm // tk):
            vt_ref[h, c, 0:HEAD_DIM, :] = v_ref[c * tk:(c + 1) * tk, sl].astype(F32).T.astype(BF16)
            vt_ref[h, c, HEAD_DIM:, :] = jnp.ones((VT_ROWS - HEAD_DIM, tk), BF16)
        for g in range(GROUP):
            qs = slice(g * HEAD_DIM, (g + 1) * HEAD_DIM)
            qr = _norm_rope(q_ref[:, qs].astype(F32), wq_ref[...], cos, sin) * scale
            qt_ref[h, :, g * tm:(g + 1) * tm] = qr.T.astype(BF16)


def _aprep(proj, cos, sin, wq, wk, offs, batch, seq, tm, tk):
    assert ATT_KV_HEADS == 2
    n = proj.shape[0]
    t = seq // tm
    cpt = tm // tk
    gw = GROUP * HEAD_DIM
    return pl.pallas_call(
        _aprep_kernel,
        grid=(batch, t),
        in_specs=[
            pl.BlockSpec((tm, gw), lambda b, i: (b * t + i, offs["aq0"] // gw)),
            pl.BlockSpec((tm, gw), lambda b, i: (b * t + i, offs["aq1"] // gw)),
            pl.BlockSpec((tm, KV_W), lambda b, i: (b * t + i, offs["ak"] // KV_W)),
            pl.BlockSpec((tm, KV_W), lambda b, i: (b * t + i, offs["av"] // KV_W)),
            pl.BlockSpec((tm, HEAD_DIM), lambda b, i: (i, 0)),
            pl.BlockSpec((tm, HEAD_DIM), lambda b, i: (i, 0)),
            pl.BlockSpec((1, HEAD_DIM), lambda b, i: (0, 0)),
            pl.BlockSpec((1, HEAD_DIM), lambda b, i: (0, 0)),
        ],
        out_specs=[
            pl.BlockSpec((None, None, ATT_KV_HEADS, HEAD_DIM, GROUP * tm),
                         lambda b, i: (b, i, 0, 0, 0)),
            pl.BlockSpec((tm, KV_W), lambda b, i: (b * t + i, 0)),
            pl.BlockSpec((None, ATT_KV_HEADS, cpt, VT_ROWS, tk), lambda b, i: (b, 0, i, 0, 0)),
        ],
        out_shape=[
            jax.ShapeDtypeStruct((batch, t, ATT_KV_HEADS, HEAD_DIM, GROUP * tm), BF16),
            jax.ShapeDtypeStruct((n, KV_W), BF16),
            jax.ShapeDtypeStruct((batch, ATT_KV_HEADS, seq // tk, VT_ROWS, tk), BF16),
        ],
        compiler_params=pltpu.CompilerParams(dimension_semantics=("parallel", "parallel")),
        name="aprep",
    )(proj, proj, proj, proj, cos, sin, wq, wk)


def _attn_kernel(qt_ref, z_ref, k_ref, vt_ref, o_ref, acc_ref, s0_ref, s1_ref, p0_ref, p1_ref,
                 *, tq, tk, seq):
    cols = GROUP * tq
    n_chunks = seq // tk
    qt = qt_ref[...]

    def scores(c, s_ref):
        start = pl.multiple_of(c * tk, tk)
        s = jnp.dot(k_ref[pl.ds(start, tk), :], qt, preferred_element_type=F32)
        s_ref[...] = s
        return jnp.max(s, axis=0, keepdims=True)

    def softmax(s_ref, p_ref, m, smax):
        m_new = jnp.maximum(m, smax)
        p_ref[...] = jnp.exp((s_ref[...] - m_new).astype(BF16))
        return m_new, jnp.exp(m - m_new)

    def accumulate(c, p_ref, alpha, init=False):
        pv = jnp.dot(vt_ref[c], p_ref[...], preferred_element_type=F32)
        acc_ref[...] = pv if init else alpha * acc_ref[...] + pv

    def pair(c, carry, first, last):
        m, alpha_prev, smax0 = carry
        m, alpha0 = softmax(s0_ref, p0_ref, m, smax0)
        smax1 = scores(c + 1, s1_ref)
        if not first:
            accumulate(c - 1, p1_ref, alpha_prev)
        m, alpha1 = softmax(s1_ref, p1_ref, m, smax1)
        if not last:
            smax0 = scores(c + 2, s0_ref)
        accumulate(c, p0_ref, alpha0, init=first)
        return m, alpha1, smax0

    n_pairs = n_chunks // 2
    smax = scores(0, s0_ref)
    carry = (jnp.full((1, cols), -jnp.inf, F32), jnp.ones((1, cols), F32), smax)
    carry = pair(0, carry, True, n_pairs == 1)
    if n_pairs > 1:
        carry = lax.fori_loop(1, n_pairs - 1, lambda j, cr: pair(2 * j, cr, False, False), carry)
        carry = pair(n_chunks - 2, carry, False, True)
    accumulate(n_chunks - 1, p1_ref, carry[1])
    acc = acc_ref[...]
    o_t = acc[:HEAD_DIM] / acc[HEAD_DIM:HEAD_DIM + 1]
    for g in range(GROUP):
        sl = slice(g * HEAD_DIM, (g + 1) * HEAD_DIM)
        z = z_ref[:, sl].astype(F32)
        o_ref[:, sl] = (o_t[:, g * tq:(g + 1) * tq].T * (z * jax.nn.sigmoid(z))).astype(BF16)


def _attention(proj, qt, k_rot, vt, offs, batch, seq, tq, tk):
    n = proj.shape[0]
    gw = GROUP * HEAD_DIM
    t = seq // tq
    cols = GROUP * tq
    assert (seq // tk) % 2 == 0 and offs["az1"] == offs["az0"] + gw
    kern = functools.partial(_attn_kernel, tq=tq, tk=tk, seq=seq)
    return pl.pallas_call(
        kern,
        grid=(batch, ATT_KV_HEADS, t),
        in_specs=[
            pl.BlockSpec((None, None, None, HEAD_DIM, cols), lambda b, h, i: (b, i, h, 0, 0)),
            pl.BlockSpec((tq, gw), lambda b, h, i: (b * t + i, offs["az0"] // gw + h)),
            pl.BlockSpec((seq, HEAD_DIM), lambda b, h, i: (b, h)),
            pl.BlockSpec((None, None, seq // tk, VT_ROWS, tk), lambda b, h, i: (b, h, 0, 0, 0)),
        ],
        out_specs=pl.BlockSpec((tq, gw), lambda b, h, i: (b * t + i, h)),
        out_shape=jax.ShapeDtypeStruct((n, ATT_W), BF16),
        scratch_shapes=[pltpu.VMEM((VT_ROWS, cols), F32),
                        pltpu.VMEM((tk, cols), F32), pltpu.VMEM((tk, cols), F32),
                        pltpu.VMEM((tk, cols), BF16), pltpu.VMEM((tk, cols), BF16)],
        compiler_params=pltpu.CompilerParams(
            dimension_semantics=("parallel", "parallel", "arbitrary")),
        name="attn",
    )(qt, proj, k_rot, vt)


HEADS_PER_TILE = 4
CAT_W = HEADS_PER_TILE * CHUNK
N_TILES = DN_HEADS // HEADS_PER_TILE


def _split3(x):
    p1 = x.astype(BF16)
    r1 = x - p1.astype(F32)
    p2 = r1.astype(BF16)
    p3 = (r1 - p2.astype(F32)).astype(BF16)
    return p1, p2, p3


def _softplus(x):
    return jnp.maximum(x, 0.0) + jnp.log(1.0 + jnp.exp(-jnp.abs(x)))


def _dot(a, b):
    return jnp.dot(a, b, preferred_element_type=F32)


def _dchunk_kernel(q0_ref, q1_ref, k0_ref, k1_ref, v0_ref, v1_ref, c_ref, r_ref, pc_ref, pr_ref,
                   a1_ref, a2_ref, au_ref, ae_ref, *, cpb):
    C, W, T = CHUNK, CAT_W, HEADS_PER_TILE
    rows = cpb * C

    def cum_matrix(n, lower):
        i = lax.broadcasted_iota(jnp.int32, (n, n), 0)
        j = lax.broadcasted_iota(jnp.int32, (n, n), 1)
        same = (i >> 6) == (j >> 6)
        return (same & ((i >= j) if lower else (i <= j))).astype(BF16)

    col_lower, col_upper = cum_matrix(rows, True), cum_matrix(rows, False)
    row_lower, row_upper = cum_matrix(W, True), cum_matrix(W, False)

    blk = c_ref[...]
    beta_all = jax.nn.sigmoid(blk)
    g_parts = _split3(-jnp.exp(pc_ref[0:1, :]) * _softplus(blk + pc_ref[1:2, :]))
    gcol = (sum(_dot(col_lower, p) for p in g_parts),
            sum(_dot(col_upper, p) for p in g_parts))
    blk_r = r_ref[...]
    g_r = (-jnp.exp(pr_ref[0][None]) * _softplus(blk_r + pr_ref[1][None])).reshape(cpb * 8, W)
    gr_parts = _split3(g_r)
    grow = (sum(_dot(p, row_upper) for p in gr_parts),
            sum(_dot(p, row_lower) for p in gr_parts))

    ri = lax.broadcasted_iota(jnp.int32, (C, W), 0)
    li = lax.broadcasted_iota(jnp.int32, (C, W), 1)
    ci = li & (C - 1)
    lb = li >> 6
    incl = (ri >= ci, ri <= ci)
    offdiag = ri != ci
    eye = (ri == ci).astype(F32)
    same = {s: (ri >> lg) == (ci >> lg) for lg, s in ((1, 2), (2, 4), (3, 8), (4, 16), (5, 32))}
    level_masks = [same[4] & ~same[2], same[8] & ~same[4], same[16] & ~same[8],
                   same[32] & ~same[16], ~same[32]]
    tile_sel = [lb == t for t in range(T)]
    head_of_lane = lax.broadcasted_iota(jnp.int32, (C, T * DN_DK), 1) >> 7

    def block_diag(y):
        zero = jnp.zeros_like(y)
        return jnp.concatenate([jnp.where(tile_sel[t], y, zero) for t in range(T)], axis=0)

    def cat_bcast(cols):
        out = jnp.broadcast_to(cols[T - 1], (C, W))
        for t in range(T - 2, -1, -1):
            out = jnp.where(tile_sel[t], jnp.broadcast_to(cols[t], (C, W)), out)
        return out

    pairs = [(c, t) for c in range(cpb) for t in range(N_TILES)]
    units = [(c, t, d) for c in range(cpb) for t in range(N_TILES) for d in range(2)]

    qk, kk, q4s, k4s, v4s = {}, {}, {}, {}, {}
    for c, t in pairs:
        rs = slice(c * C, (c + 1) * C)
        q4 = (q0_ref, q1_ref)[t][rs, :]
        k4 = (k0_ref, k1_ref)[t][rs, :]
        v4 = (v0_ref, v1_ref)[t][rs, :]
        zero = jnp.zeros_like(k4)
        k_bd = jnp.concatenate([jnp.where(head_of_lane == h, k4, zero) for h in range(T)], axis=0)
        r = lax.dot_general(jnp.concatenate([q4, k4], axis=0), k_bd, (((1,), (1,)), ((), ())),
                            preferred_element_type=F32)
        qk[c, t], kk[c, t] = r[:C], r[C:]
        q4s[c, t], k4s[c, t], v4s[c, t] = q4, k4, v4

    lmat, amat, xinv = {}, {}, {}
    for c, t, d in units:
        rs = slice(c * C, (c + 1) * C)
        base = d * DN_HEADS + t * T
        gc = cat_bcast([gcol[d][rs, 2 * DN_HEADS + base + h:2 * DN_HEADS + base + h + 1]
                        for h in range(T)])
        beta = cat_bcast([beta_all[rs, base + h:base + h + 1] for h in range(T)])
        row = c * 8 + d * N_TILES + t
        gr = jnp.broadcast_to(grow[d][row:row + 1, :], (C, W))
        decay = jnp.exp(jnp.where(incl[d], gc - gr, -1e30))
        l = jnp.where(offdiag, beta * kk[c, t] * decay, 0.0)
        lmat[c, t, d] = l
        amat[c, t, d] = qk[c, t] * decay
        xinv[c, t, d] = eye - jnp.where(same[2], l, 0.0)

    for msk in level_masks:
        ys, xbs = {}, {}
        for u in units:
            e = jnp.where(msk, lmat[u], 0.0).astype(BF16)
            xbs[u] = xinv[u].astype(BF16)
            ys[u] = _dot(e, block_diag(xbs[u]))
        for u in units:
            xinv[u] = xinv[u] - _dot(xbs[u], block_diag(ys[u].astype(BF16)))

    for c, t, d in units:
        rs = slice(c * C, (c + 1) * C)
        kd_rows, rhs_rows = [], []
        for h in range(T):
            head = t * T + h
            col = d * DN_HEADS + head
            hs = slice(h * DN_DK, (h + 1) * DN_DK)
            out_s = slice(head * DN_DK, (head + 1) * DN_DK)
            gc = jnp.broadcast_to(gcol[d][rs, 2 * DN_HEADS + col:2 * DN_HEADS + col + 1],
                                  (C, DN_DK))
            beta = jnp.broadcast_to(beta_all[rs, col:col + 1], (C, DN_DK))
            g_last = gc[C - 1:C, :] if d == 0 else gc[0:1, :]
            eg = jnp.exp(gc)
            kf = k4s[c, t][:, hs].astype(F32)
            rhs = jnp.concatenate([kf * (beta * eg), v4s[c, t][:, hs].astype(F32) * beta],
                                  axis=1).astype(BF16)
            zero = jnp.zeros_like(rhs)
            rhs_rows.append(jnp.concatenate([rhs if hh == h else zero for hh in range(T)], axis=1))
            kd_rows.append(kf * jnp.exp(g_last - gc))
            a1_ref[c, d, C:2 * C, out_s] = (q4s[c, t][:, hs].astype(F32) * eg).astype(BF16)
            ae_ref[c, d, :, out_s] = jnp.exp(g_last)
        wu = _dot(xinv[c, t, d].astype(BF16), jnp.concatenate(rhs_rows, axis=0))
        for h in range(T):
            head = t * T + h
            out_s = slice(head * DN_DK, (head + 1) * DN_DK)
            a1_ref[c, d, 0:C, out_s] = wu[:, 2 * h * DN_DK:(2 * h + 1) * DN_DK].astype(BF16)
            au_ref[c, d, :, out_s] = wu[:, (2 * h + 1) * DN_DK:(2 * h + 2) * DN_DK].astype(BF16)
        cat_s = slice(t * W, (t + 1) * W)
        a2_ref[c, d, 0:C, cat_s] = amat[c, t, d].astype(BF16)
        a2_ref[c, d, C:C + DN_DK, cat_s] = jnp.concatenate(kd_rows, axis=0).T.astype(BF16)


def _dchunk(proj, small, small_r, par_col, par_row, offs, batch, seq, cpb=4):
    assert N_TILES == 2
    nc = seq // CHUNK
    steps = nc // cpb
    tw = HEADS_PER_TILE * DN_DK
    kern = functools.partial(_dchunk_kernel, cpb=cpb)
    out5 = lambda b, i: (b, i, 0, 0, 0)
    tile_cols = [offs["dq"], offs["dq"] + tw, offs["dk0"], offs["dk1"], offs["dv"], offs["dv"] + tw]
    qkv_specs = [pl.BlockSpec((cpb * CHUNK, tw), functools.partial(
        lambda b, i, col: (b * steps + i, col), col=c // tw)) for c in tile_cols]
    return pl.pallas_call(
        kern,
        grid=(batch, steps),
        in_specs=qkv_specs + [
            pl.BlockSpec((cpb * CHUNK, LANES), lambda b, i: (b * steps + i, 0)),
            pl.BlockSpec((cpb, 8, CAT_W), lambda b, i: (b * steps + i, 0, 0)),
            pl.BlockSpec((8, LANES), lambda b, i: (0, 0)),
            pl.BlockSpec((2, 8, CAT_W), lambda b, i: (0, 0, 0)),
        ],
        out_specs=[
            pl.BlockSpec((None, cpb, 2, 2 * CHUNK, DN_KW), out5),
            pl.BlockSpec((None, cpb, 2, CHUNK + DN_DK, DN_HEADS * CHUNK), out5),
            pl.BlockSpec((None, cpb, 2, CHUNK, DN_VW), out5),
            pl.BlockSpec((None, cpb, 2, 1, DN_VW), out5),
        ],
        out_shape=[
            jax.ShapeDtypeStruct((batch, nc, 2, 2 * CHUNK, DN_KW), BF16),
            jax.ShapeDtypeStruct((batch, nc, 2, CHUNK + DN_DK, DN_HEADS * CHUNK), BF16),
            jax.ShapeDtypeStruct((batch, nc, 2, CHUNK, DN_VW), BF16),
            jax.ShapeDtypeStruct((batch, nc, 2, 1, DN_VW), F32),
        ],
        compiler_params=pltpu.CompilerParams(dimension_semantics=("parallel", "parallel")),
        name="dchunk",
    )(proj, proj, proj, proj, proj, proj, small, small_r, par_col, par_row)


def _dscan_kernel(a1f_ref, a1b_ref, a2f_ref, a2b_ref, auf_ref, aub_ref, aef_ref, aeb_ref,
                  of_ref, ob_ref, s_ref, *, batch, cps):
    C = CHUNK
    PW = 2 * DN_DV
    n_pairs = DN_HEADS // 2

    @pl.when(pl.program_id(0) == 0)
    def _():
        s_ref[...] = jnp.zeros(s_ref.shape, F32)

    first = lax.broadcasted_iota(jnp.int32, (DN_DK, PW), 1) < DN_DV
    first_c = lax.broadcasted_iota(jnp.int32, (C, PW), 1) < DN_DV
    refs = ((a1f_ref, a2f_ref, auf_ref, aef_ref, of_ref), (a1b_ref, a2b_ref, aub_ref, aeb_ref, ob_ref))
    units = [(b, d, p) for b in range(batch) for d in range(2) for p in range(n_pairs)]
    sidx = lambda u: (u[0] * 2 + u[1]) * n_pairs + u[2]

    states = {u: s_ref[sidx(u)] for u in units}
    for step in range(cps):
        r1, r2 = {}, {}
        cc = (step, cps - 1 - step)
        for u in units:
            b, d, p = u
            r1[u] = _dot(refs[d][0][b, cc[d], :, p * PW:(p + 1) * PW], states[u].astype(BF16))
        for u in units:
            b, d, p = u
            vb = (refs[d][2][b, cc[d], :, p * PW:(p + 1) * PW].astype(F32) - r1[u][:C]).astype(BF16)
            zero = jnp.zeros_like(vb)
            v_bd = jnp.concatenate([jnp.where(first_c, vb, zero), jnp.where(first_c, zero, vb)],
                                   axis=0)
            r2[u] = _dot(refs[d][1][b, cc[d], :, p * 2 * C:(p + 1) * 2 * C], v_bd)
        for u in units:
            b, d, p = u
            refs[d][4][b, cc[d], :, p * PW:(p + 1) * PW] = (r1[u][C:] + r2[u][:C]).astype(BF16)
            e = refs[d][3][b, cc[d], 0:1, p * PW:(p + 1) * PW]
            upd = r2[u][C:]
            s = states[u]
            states[u] = jnp.concatenate(
                [s[:DN_DK] * e + jnp.where(first, upd, 0.0),
                 s[DN_DK:] * e + jnp.where(first, 0.0, upd)], axis=0)
    for u in units:
        s_ref[sidx(u)] = states[u]


def _dscan(a1, a2, au, ae, batch, seq, cps=2):
    nc = seq // CHUNK
    steps = nc // cps
    kern = functools.partial(_dscan_kernel, batch=batch, cps=cps)
    fwd = lambda i: (0, i, 0, 0, 0)
    bwd = lambda i: (0, steps - 1 - i, 1, 0, 0)

    def spec(arr, imap):
        return pl.BlockSpec((batch, cps, None) + arr.shape[3:], imap)

    o_spec = lambda imap: pl.BlockSpec((batch, cps, CHUNK, DN_VW), imap)
    o_shape = jax.ShapeDtypeStruct((batch, nc, CHUNK, DN_VW), BF16)
    return pl.pallas_call(
        kern,
        grid=(steps,),
        in_specs=[spec(a1, fwd), spec(a1, bwd), spec(a2, fwd), spec(a2, bwd),
                  spec(au, fwd), spec(au, bwd), spec(ae, fwd), spec(ae, bwd)],
        out_specs=[o_spec(lambda i: (0, i, 0, 0)), o_spec(lambda i: (0, steps - 1 - i, 0, 0))],
        out_shape=[o_shape, o_shape],
        scratch_shapes=[pltpu.VMEM((batch * 2 * (DN_HEADS // 2), 2 * DN_DK, 2 * DN_DV), F32)],
        compiler_params=pltpu.CompilerParams(dimension_semantics=("arbitrary",)),
        name="dscan",
    )(a1, a1, a2, a2, au, au, ae, ae)


def _post_kernel(oa_ref, of_ref, ob_ref, dz_ref, ga_ref, gd_ref, x_ref, p_ref,
                 wa_ref, wd_ref, wo_ref, wg_ref, wp_ref, nd_ref, npost_ref, nple_ref, o_ref):
    def rms(t, w):
        return t * lax.rsqrt(jnp.mean(t * t, axis=-1, keepdims=True) + EPS) * w

    nd = nd_ref[...]
    parts = []
    for h in range(DN_HEADS):
        sl = slice(h * DN_DV, (h + 1) * DN_DV)
        od = of_ref[:, sl].astype(F32) + ob_ref[:, sl].astype(F32)
        z = dz_ref[:, sl].astype(F32)
        parts.append((rms(od, nd) * (z * jax.nn.sigmoid(z))).astype(BF16))
    y_dn = jnp.dot(jnp.concatenate(parts, axis=1), wd_ref[...], preferred_element_type=F32)
    y_att = jnp.dot(oa_ref[...], wa_ref[...], preferred_element_type=F32)
    merged = (jax.nn.sigmoid(ga_ref[...].astype(F32)) * y_att
              + jax.nn.sigmoid(gd_ref[...].astype(F32)) * y_dn)
    mix = jnp.dot(merged.astype(BF16), wo_ref[...], preferred_element_type=F32)
    x1 = x_ref[...] + rms(mix, npost_ref[...])
    gate = jax.nn.sigmoid(jnp.dot(x1.astype(BF16), wg_ref[...], preferred_element_type=F32))
    e = jnp.dot(p_ref[...].astype(BF16), wp_ref[...], preferred_element_type=F32)
    o_ref[...] = x1 + rms(gate * e, nple_ref[...])


def _post(o_att, o_f, o_b, proj, x2, p2, wa, wd, wo, wg, wp, nd, npost, nple, offs, tm=512):
    n, d = x2.shape
    ple = p2.shape[1]
    row = lambda i: (i, 0)
    const = lambda i: (0, 0)
    return pl.pallas_call(
        _post_kernel,
        grid=(n // tm,),
        in_specs=[
            pl.BlockSpec((tm, ATT_W), row),
            pl.BlockSpec((tm, DN_VW), row),
            pl.BlockSpec((tm, DN_VW), row),
            pl.BlockSpec((tm, DN_VW), lambda i: (i, offs["dz"] // DN_VW)),
            pl.BlockSpec((tm, d), lambda i: (i, offs["ga"] // d)),
            pl.BlockSpec((tm, d), lambda i: (i, offs["gd"] // d)),
            pl.BlockSpec((tm, d), row),
            pl.BlockSpec((tm, ple), row),
            pl.BlockSpec((ATT_W, d), const),
            pl.BlockSpec((DN_VW, d), const),
            pl.BlockSpec((d, d), const),
            pl.BlockSpec((d, d), const),
            pl.BlockSpec((ple, d), const),
            pl.BlockSpec((1, DN_DV), const),
            pl.BlockSpec((1, d), const),
            pl.BlockSpec((1, d), const),
        ],
        out_specs=pl.BlockSpec((tm, d), row),
        out_shape=jax.ShapeDtypeStruct((n, d), F32),
        compiler_params=pltpu.CompilerParams(dimension_semantics=("parallel",)),
        name="post",
    )(o_att, o_f, o_b, proj, proj, proj, x2, p2, wa, wd, wo, wg, wp, nd, npost, nple)


def _rope_tables(seq):
    rows = seq // GRID_W
    row = np.broadcast_to(np.arange(rows)[:, None], (rows, GRID_W)).reshape(seq)
    col = np.broadcast_to(np.arange(GRID_W)[None, :], (rows, GRID_W)).reshape(seq)
    n_freq = HEAD_DIM // 4
    inv_freq = np.float32(ROPE_THETA) ** (-np.arange(n_freq, dtype=np.float32) / np.float32(n_freq))
    ang = np.concatenate([row.astype(np.float32)[:, None] * inv_freq,
                          col.astype(np.float32)[:, None] * inv_freq], axis=-1)
    cos, sin = np.cos(ang), np.sin(ang)
    return (jnp.asarray(np.concatenate([cos, cos], axis=-1), F32),
            jnp.asarray(np.concatenate([-sin, sin], axis=-1), F32))


def kernel(x, p, norm_pre, w_in, q_norm, k_norm, conv_w, a_log, dt_bias, dn_norm, w_br_att,
           w_br_dn, w_out, norm_post, w_ple_proj, w_ple_gate, ple_norm):
    batch, seq, d = x.shape
    depth = w_in.shape[0]
    n = batch * seq
    assert seq % GRID_W == 0 and seq % CHUNK == 0 and d % LANES == 0
    cos, sin = _rope_tables(seq)
    x2 = x.reshape(n, d)
    for i in range(depth):
        w_main, w_small, offs, pieces = _regroup_columns(w_in[i])
        conv_w8 = jnp.pad(conv_w[i], ((0, 8 - CONV_K), (0, 0)))
        proj, small = _proj(x2, norm_pre[i][None, :], w_main, w_small, conv_w8, pieces, seq)

        tq, tk = min(ATT_TQ, seq), min(ATT_TK, seq // 2)
        qt, k_rot, vt = _aprep(proj, cos, sin, _deinterleave(q_norm[i])[None, :],
                               _deinterleave(k_norm[i])[None, :], offs, batch, seq, tq, tk)
        o_att = _attention(proj, qt, k_rot, vt, offs, batch, seq, tq, tk)

        nsm = 4 * DN_HEADS
        nck = n // CHUNK
        small_r = small[:, 2 * DN_HEADS:nsm].reshape(nck, CHUNK, 2, N_TILES, HEADS_PER_TILE)
        small_r = small_r.transpose(0, 2, 3, 4, 1).reshape(nck, 2 * N_TILES, CAT_W)
        small_r = jnp.pad(small_r, ((0, 0), (0, 8 - 2 * N_TILES), (0, 0)))
        par = jnp.stack([a_log[i].reshape(-1), dt_bias[i].reshape(-1)])
        par_col = jnp.pad(jnp.concatenate([jnp.zeros_like(par), par], axis=1),
                          ((0, 6), (0, LANES - nsm)))
        par_row = jnp.repeat(par.reshape(2, 2 * N_TILES, HEADS_PER_TILE), CHUNK, axis=2)
        par_row = jnp.pad(par_row, ((0, 0), (0, 8 - 2 * N_TILES), (0, 0)))
        a1, a2, au, ae = _dchunk(proj, small, small_r, par_col, par_row, offs, batch, seq)
        o_f, o_b = _dscan(a1, a2, au, ae, batch, seq)
        o_f, o_b = o_f.reshape(n, DN_VW), o_b.reshape(n, DN_VW)

        x2 = _post(o_att, o_f, o_b, proj, x2, p[i].reshape(n, -1),
                   w_br_att[i].astype(BF16), w_br_dn[i].astype(BF16), w_out[i].astype(BF16),
                   w_ple_gate[i].astype(BF16), w_ple_proj[i].astype(BF16),
                   dn_norm[i][None, :], norm_post[i][None, :], ple_norm[i][None, :], offs)
    return x2.reshape(batch, seq, d)
```

```python
import functools

import numpy as np
import jax
import jax.numpy as jnp
from jax import lax
from jax.experimental import pallas as pl
from jax.experimental.pallas import tpu as pltpu

F32 = jnp.float32
BF16 = jnp.bfloat16

GRID_W = 64
ATT_HEADS = 8
ATT_KV_HEADS = 2
HEAD_DIM = 128
ROPE_THETA = 10000.0
DN_HEADS = 8
DN_DK = 128
DN_DV = 128
CONV_K = 5
CHUNK = 64
EPS = 1e-6

ATT_W = ATT_HEADS * HEAD_DIM
KV_W = ATT_KV_HEADS * HEAD_DIM
DN_KW = DN_HEADS * DN_DK
DN_VW = DN_HEADS * DN_DV
GROUP = ATT_HEADS // ATT_KV_HEADS
LANES = 128


def _deinterleave(t):
    lead = t.shape[:-1]
    t = t.reshape(lead + (-1, HEAD_DIM // 2, 2))
    return jnp.swapaxes(t, -1, -2).reshape(lead + (-1,))


PROJ_TN = 4352
CONV_HALO = 16
CONV_PIECE = 512


def _regroup_columns(w):
    d_model = w.shape[0]
    sizes = (ATT_W, KV_W, KV_W, ATT_W, DN_KW, DN_KW, DN_VW, 2 * DN_HEADS, 2 * DN_HEADS, DN_VW,
             d_model, d_model)
    assert w.shape[1] == sum(sizes)
    aq, ak, av, az, dq, dk, dv, db, da, dz, ga, gd = jnp.split(w, np.cumsum(sizes)[:-1], axis=1)
    aq0, aq1 = jnp.split(_deinterleave(aq), 2, axis=1)
    az0, az1 = jnp.split(az, 2, axis=1)
    dk0, dk1 = jnp.split(dk, 2, axis=1)
    groups = [("ga", ga, "plain", 0), ("gd", gd, "plain", 0), ("dq", dq, "conv_q", 0),
              ("dk0", dk0, "conv_k", DN_KW), ("aq0", aq0, "plain", 0),
              ("ak", _deinterleave(ak), "plain", 0),
              ("av", av, "plain", 0), ("dk1", dk1, "conv_k", DN_KW + DN_KW // 2),
              ("dz", dz, "plain", 0), ("dv", dv, "conv_v", 2 * DN_KW), ("aq1", aq1, "plain", 0),
              ("az0", az0, "plain", 0), ("az1", az1, "plain", 0)]
    offsets, off = {}, 0
    pieces = [[] for _ in range(sum(g.shape[1] for _, g, _, _ in groups) // PROJ_TN)]
    for name, g, kind, ch in groups:
        offsets[name] = off
        width = g.shape[1]
        tile, local = divmod(off, PROJ_TN)
        assert local + width <= PROJ_TN
        if kind == "plain":
            last = pieces[tile][-1] if pieces[tile] else None
            if last is not None and last[2] == "plain" and last[0] + last[1] == local:
                pieces[tile][-1] = (last[0], last[1] + width, "plain", 0)
            else:
                pieces[tile].append((local, width, "plain", 0))
        else:
            for k in range(0, width, CONV_PIECE):
                pieces[tile].append((local + k, CONV_PIECE, kind, ch + k))
        off += width
    main = jnp.concatenate([g.astype(BF16) for _, g, _, _ in groups], axis=1)
    small = jnp.pad(jnp.concatenate([db, da], axis=1), ((0, 0), (0, LANES - 4 * DN_HEADS)))
    return main, small.astype(BF16), offsets, pieces


def _proj_kernel(x_ref, xp_ref, xn_ref, g_ref, w_ref, ws_ref, cw_ref, o_ref, os_ref, h_ref,
                 *, tiles_per_seq, pieces):
    i, j = pl.program_id(0), pl.program_id(1)
    tm = x_ref.shape[0]
    ext = tm + 2 * CONV_HALO
    half = CONV_K // 2

    def normed(x):
        r = lax.rsqrt(jnp.mean(x * x, axis=-1, keepdims=True) + EPS)
        return (x * r * g_ref[...]).astype(BF16)

    @pl.when(j == 0)
    def _():
        t = i % tiles_per_seq
        h = normed(x_ref[...])
        zero = jnp.zeros((CONV_HALO, x_ref.shape[1]), BF16)
        h_ref[0:CONV_HALO, :] = jnp.where(t == 0, zero, normed(xp_ref[...]))
        h_ref[CONV_HALO:CONV_HALO + tm, :] = h
        h_ref[CONV_HALO + tm:, :] = jnp.where(t == tiles_per_seq - 1, zero, normed(xn_ref[...]))
        os_ref[...] = jnp.dot(h, ws_ref[...], preferred_element_type=F32)

    def tile(tile_pieces):
        for start, width, kind, ch in tile_pieces:
            cols = slice(start, start + width)
            if kind == "plain":
                o_ref[:, cols] = jnp.dot(h_ref[CONV_HALO:CONV_HALO + tm, :], w_ref[:, cols],
                                         preferred_element_type=F32).astype(BF16)
                continue
            r = jnp.dot(h_ref[...], w_ref[:, cols], preferred_element_type=F32)
            taps = cw_ref[:, ch:ch + width]
            acc = r[CONV_HALO:CONV_HALO + tm] * taps[half:half + 1]
            for t in range(CONV_K):
                if t != half:
                    shifted = pltpu.roll(r, (half - t) % ext, axis=0)
                    acc = acc + shifted[CONV_HALO:CONV_HALO + tm] * taps[t:t + 1]
            y = acc * jax.nn.sigmoid(acc)
            for hh in range(width // DN_DK):
                yh = y[:, hh * DN_DK:(hh + 1) * DN_DK]
                if kind != "conv_v":
                    inv = lax.rsqrt(jnp.sum(yh * yh, axis=-1, keepdims=True) + EPS)
                    yh = yh * (inv * DN_DK ** -0.5 if kind == "conv_q" else inv)
                o_ref[:, start + hh * DN_DK:start + (hh + 1) * DN_DK] = yh.astype(BF16)

    for jj, tile_pieces in enumerate(pieces):
        pl.when(j == jj)(functools.partial(tile, tile_pieces))


def _proj(x2, g, w_main, w_small, conv_w8, pieces, seq, tm=1024):
    n, d = x2.shape
    main_w = w_main.shape[1]
    tm = min(tm, seq)
    hb = tm // CONV_HALO
    last_hb = n // CONV_HALO - 1
    kern = functools.partial(_proj_kernel, tiles_per_seq=seq // tm, pieces=pieces)
    return pl.pallas_call(
        kern,
        grid=(n // tm, main_w // PROJ_TN),
        in_specs=[
            pl.BlockSpec((tm, d), lambda i, j: (i, 0)),
            pl.BlockSpec((CONV_HALO, d), lambda i, j: (jnp.maximum(i * hb - 1, 0), 0)),
            pl.BlockSpec((CONV_HALO, d), lambda i, j: (jnp.minimum((i + 1) * hb, last_hb), 0)),
            pl.BlockSpec((1, d), lambda i, j: (0, 0)),
            pl.BlockSpec((d, PROJ_TN), lambda i, j: (0, j)),
            pl.BlockSpec((d, LANES), lambda i, j: (0, 0)),
            pl.BlockSpec(conv_w8.shape, lambda i, j: (0, 0)),
        ],
        out_specs=[
            pl.BlockSpec((tm, PROJ_TN), lambda i, j: (i, j)),
            pl.BlockSpec((tm, LANES), lambda i, j: (i, 0)),
        ],
        out_shape=[
            jax.ShapeDtypeStruct((n, main_w), BF16),
            jax.ShapeDtypeStruct((n, LANES), F32),
        ],
        scratch_shapes=[pltpu.VMEM((tm + 2 * CONV_HALO, d), BF16)],
        compiler_params=pltpu.CompilerParams(dimension_semantics=("parallel", "arbitrary")),
        name="proj",
    )(x2, x2, x2, g, w_main, w_small, conv_w8)


def _norm_rope(xh, w, cos, sin):
    r = lax.rsqrt(jnp.mean(xh * xh, axis=-1, keepdims=True) + EPS)
    xn = xh * r * w
    return xn * cos + pltpu.roll(xn, HEAD_DIM // 2, axis=1) * sin


ATT_TQ = 1024
ATT_TK = 512
VT_ROWS = HEAD_DIM + 16


def _aprep_kernel(q0_ref, q1_ref, k_ref, v_ref, cos_ref, sin_ref, wq_ref, wk_ref,
                  qt_ref, kr_ref, vt_ref):
    cos, sin = cos_ref[...], sin_ref[...]
    scale = HEAD_DIM ** -0.5
    tm = k_ref.shape[0]
    tk = vt_ref.shape[-1]
    for h, q_ref in enumerate((q0_ref, q1_ref)):
        sl = slice(h * HEAD_DIM, (h + 1) * HEAD_DIM)
        kr_ref[:, sl] = _norm_rope(k_ref[:, sl].astype(F32), wk_ref[...], cos, sin).astype(BF16)
        for c in range(tm // tk):
            vt_ref[h, c, 0:HEAD_DIM, :] = v_ref[c * tk:(c + 1) * tk, sl].astype(F32).T.astype(BF16)
            vt_ref[h, c, HEAD_DIM:, :] = jnp.ones((VT_ROWS - HEAD_DIM, tk), BF16)
        for g in range(GROUP):
            qs = slice(g * HEAD_DIM, (g + 1) * HEAD_DIM)
            qr = _norm_rope(q_ref[:, qs].astype(F32), wq_ref[...], cos, sin) * scale
            qt_ref[h, :, g * tm:(g + 1) * tm] = qr.T.astype(BF16)


def _aprep(proj, cos, sin, wq, wk, offs, batch, seq, tm, tk):
    assert ATT_KV_HEADS == 2
    n = proj.shape[0]
    t = seq // tm
    cpt = tm // tk
    gw = GROUP * HEAD_DIM
    return pl.pallas_call(
        _aprep_kernel,
        grid=(batch, t),
        in_specs=[
            pl.BlockSpec((tm, gw), lambda b, i: (b * t + i, offs["aq0"] // gw)),
            pl.BlockSpec((tm, gw), lambda b, i: (b * t + i, offs["aq1"] // gw)),
            pl.BlockSpec((tm, KV_W), lambda b, i: (b * t + i, offs["ak"] // KV_W)),
            pl.BlockSpec((tm, KV_W), lambda b, i: (b * t + i, offs["av"] // KV_W)),
            pl.BlockSpec((tm, HEAD_DIM), lambda b, i: (i, 0)),
            pl.BlockSpec((tm, HEAD_DIM), lambda b, i: (i, 0)),
            pl.BlockSpec((1, HEAD_DIM), lambda b, i: (0, 0)),
            pl.BlockSpec((1, HEAD_DIM), lambda b, i: (0, 0)),
        ],
        out_specs=[
            pl.BlockSpec((None, None, ATT_KV_HEADS, HEAD_DIM, GROUP * tm),
                         lambda b, i: (b, i, 0, 0, 0)),
            pl.BlockSpec((tm, KV_W), lambda b, i: (b * t + i, 0)),
            pl.BlockSpec((None, ATT_KV_HEADS, cpt, VT_ROWS, tk), lambda b, i: (b, 0, i, 0, 0)),
        ],
        out_shape=[
            jax.ShapeDtypeStruct((batch, t, ATT_KV_HEADS, HEAD_DIM, GROUP * tm), BF16),
            jax.ShapeDtypeStruct((n, KV_W), BF16),
            jax.ShapeDtypeStruct((batch, ATT_KV_HEADS, seq // tk, VT_ROWS, tk), BF16),
        ],
        compiler_params=pltpu.CompilerParams(dimension_semantics=("parallel", "parallel")),
        name="aprep",
    )(proj, proj, proj, proj, cos, sin, wq, wk)


def _attn_kernel(qt_ref, z_ref, k_ref, vt_ref, o_ref, acc_ref, s0_ref, s1_ref, p0_ref, p1_ref,
                 *, tq, tk, seq):
    cols = GROUP * tq
    n_chunks = seq // tk
    qt = qt_ref[...]

    def scores(c, s_ref):
        start = pl.multiple_of(c * tk, tk)
        s = jnp.dot(k_ref[pl.ds(start, tk), :], qt, preferred_element_type=F32)
        s_ref[...] = s
        return jnp.max(s, axis=0, keepdims=True)

    def softmax(s_ref, p_ref, m, smax):
        m_new = jnp.maximum(m, smax)
        p_ref[...] = jnp.exp((s_ref[...] - m_new).astype(BF16))
        return m_new, jnp.exp(m - m_new)

    def accumulate(c, p_ref, alpha, init=False):
        pv = jnp.dot(vt_ref[c], p_ref[...], preferred_element_type=F32)
        acc_ref[...] = pv if init else alpha * acc_ref[...] + pv

    def pair(c, carry, first, last):
        m, alpha_prev, smax0 = carry
        m, alpha0 = softmax(s0_ref, p0_ref, m, smax0)
        smax1 = scores(c + 1, s1_ref)
        if not first:
            accumulate(c - 1, p1_ref, alpha_prev)
        m, alpha1 = softmax(s1_ref, p1_ref, m, smax1)
        if not last:
            smax0 = scores(c + 2, s0_ref)
        accumulate(c, p0_ref, alpha0, init=first)
        return m, alpha1, smax0

    n_pairs = n_chunks // 2
    smax = scores(0, s0_ref)
    carry = (jnp.full((1, cols), -jnp.inf, F32), jnp.ones((1, cols), F32), smax)
    carry = pair(0, carry, True, n_pairs == 1)
    if n_pairs > 1:
        carry = lax.fori_loop(1, n_pairs - 1, lambda j, cr: pair(2 * j, cr, False, False), carry)
        carry = pair(n_chunks - 2, carry, False, True)
    accumulate(n_chunks - 1, p1_ref, carry[1])
    acc = acc_ref[...]
    o_t = acc[:HEAD_DIM] / acc[HEAD_DIM:HEAD_DIM + 1]
    for g in range(GROUP):
        sl = slice(g * HEAD_DIM, (g + 1) * HEAD_DIM)
        z = z_ref[:, sl].astype(F32)
        o_ref[:, sl] = (o_t[:, g * tq:(g + 1) * tq].T * (z * jax.nn.sigmoid(z))).astype(BF16)


def _attention(proj, qt, k_rot, vt, offs, batch, seq, tq, tk):
    n = proj.shape[0]
    gw = GROUP * HEAD_DIM
    t = seq // tq
    cols = GROUP * tq
    assert (seq // tk) % 2 == 0 and offs["az1"] == offs["az0"] + gw
    kern = functools.partial(_attn_kernel, tq=tq, tk=tk, seq=seq)
    return pl.pallas_call(
        kern,
        grid=(batch, ATT_KV_HEADS, t),
        in_specs=[
            pl.BlockSpec((None, None, None, HEAD_DIM, cols), lambda b, h, i: (b, i, h, 0, 0)),
            pl.BlockSpec((tq, gw), lambda b, h, i: (b * t + i, offs["az0"] // gw + h)),
            pl.BlockSpec((seq, HEAD_DIM), lambda b, h, i: (b, h)),
            pl.BlockSpec((None, None, seq // tk, VT_ROWS, tk), lambda b, h, i: (b, h, 0, 0, 0)),
        ],
        out_specs=pl.BlockSpec((tq, gw), lambda b, h, i: (b * t + i, h)),
        out_shape=jax.ShapeDtypeStruct((n, ATT_W), BF16),
        scratch_shapes=[pltpu.VMEM((VT_ROWS, cols), F32),
                        pltpu.VMEM((tk, cols), F32), pltpu.VMEM((tk, cols), F32),
                        pltpu.VMEM((tk, cols), BF16), pltpu.VMEM((tk, cols), BF16)],
        compiler_params=pltpu.CompilerParams(
            dimension_semantics=("parallel", "parallel", "arbitrary")),
        name="attn",
    )(qt, proj, k_rot, vt)


HEADS_PER_TILE = 4
CAT_W = HEADS_PER_TILE * CHUNK
N_TILES = DN_HEADS // HEADS_PER_TILE


def _split3(x):
    p1 = x.astype(BF16)
    r1 = x - p1.astype(F32)
    p2 = r1.astype(BF16)
    p3 = (r1 - p2.astype(F32)).astype(BF16)
    return p1, p2, p3


def _softplus(x):
    return jnp.maximum(x, 0.0) + jnp.log(1.0 + jnp.exp(-jnp.abs(x)))


def _dot(a, b):
    return jnp.dot(a, b, preferred_element_type=F32)


def _dchunk_kernel(q0_ref, q1_ref, k0_ref, k1_ref, v0_ref, v1_ref, c_ref, r_ref, pc_ref, pr_ref,
                   a1_ref, a2_ref, au_ref, ae_ref, *, cpb):
    C, W, T = CHUNK, CAT_W, HEADS_PER_TILE
    rows = cpb * C

    def cum_matrix(n, lower):
        i = lax.broadcasted_iota(jnp.int32, (n, n), 0)
        j = lax.broadcasted_iota(jnp.int32, (n, n), 1)
        same = (i >> 6) == (j >> 6)
        return (same & ((i >= j) if lower else (i <= j))).astype(BF16)

    col_lower, col_upper = cum_matrix(rows, True), cum_matrix(rows, False)
    row_lower, row_upper = cum_matrix(W, True), cum_matrix(W, False)

    blk = c_ref[...]
    beta_all = jax.nn.sigmoid(blk)
    g_parts = _split3(-jnp.exp(pc_ref[0:1, :]) * _softplus(blk + pc_ref[1:2, :]))
    gcol = (sum(_dot(col_lower, p) for p in g_parts),
            sum(_dot(col_upper, p) for p in g_parts))
    blk_r = r_ref[...]
    g_r = (-jnp.exp(pr_ref[0][None]) * _softplus(blk_r + pr_ref[1][None])).reshape(cpb * 8, W)
    gr_parts = _split3(g_r)
    grow = (sum(_dot(p, row_upper) for p in gr_parts),
            sum(_dot(p, row_lower) for p in gr_parts))

    ri = lax.broadcasted_iota(jnp.int32, (C, W), 0)
    li = lax.broadcasted_iota(jnp.int32, (C, W), 1)
    ci = li & (C - 1)
    lb = li >> 6
    incl = (ri >= ci, ri <= ci)
    offdiag = ri != ci
    eye = (ri == ci).astype(F32)
    same = {s: (ri >> lg) == (ci >> lg) for lg, s in ((1, 2), (2, 4), (3, 8), (4, 16), (5, 32))}
    level_masks = [same[4] & ~same[2], same[8] & ~same[4], same[16] & ~same[8],
                   same[32] & ~same[16], ~same[32]]
    tile_sel = [lb == t for t in range(T)]
    head_of_lane = lax.broadcasted_iota(jnp.int32, (C, T * DN_DK), 1) >> 7

    def block_diag(y):
        zero = jnp.zeros_like(y)
        return jnp.concatenate([jnp.where(tile_sel[t], y, zero) for t in range(T)], axis=0)

    def cat_bcast(cols):
        out = jnp.broadcast_to(cols[T - 1], (C, W))
        for t in range(T - 2, -1, -1):
            out = jnp.where(tile_sel[t], jnp.broadcast_to(cols[t], (C, W)), out)
        return out

    pairs = [(c, t) for c in range(cpb) for t in range(N_TILES)]
    units = [(c, t, d) for c in range(cpb) for t in range(N_TILES) for d in range(2)]

    qk, kk, q4s, k4s, v4s = {}, {}, {}, {}, {}
    for c, t in pairs:
        rs = slice(c * C, (c + 1) * C)
        q4 = (q0_ref, q1_ref)[t][rs, :]
        k4 = (k0_ref, k1_ref)[t][rs, :]
        v4 = (v0_ref, v1_ref)[t][rs, :]
        zero = jnp.zeros_like(k4)
        k_bd = jnp.concatenate([jnp.where(head_of_lane == h, k4, zero) for h in range(T)], axis=0)
        r = lax.dot_general(jnp.concatenate([q4, k4], axis=0), k_bd, (((1,), (1,)), ((), ())),
                            preferred_element_type=F32)
        qk[c, t], kk[c, t] = r[:C], r[C:]
        q4s[c, t], k4s[c, t], v4s[c, t] = q4, k4, v4

    lmat, amat, xinv = {}, {}, {}
    for c, t, d in units:
        rs = slice(c * C, (c + 1) * C)
        base = d * DN_HEADS + t * T
        gc = cat_bcast([gcol[d][rs, 2 * DN_HEADS + base + h:2 * DN_HEADS + base + h + 1]
                        for h in range(T)])
        beta = cat_bcast([beta_all[rs, base + h:base + h + 1] for h in range(T)])
        row = c * 8 + d * N_TILES + t
        gr = jnp.broadcast_to(grow[d][row:row + 1, :], (C, W))
        decay = jnp.exp(jnp.where(incl[d], gc - gr, -1e30))
        l = jnp.where(offdiag, beta * kk[c, t] * decay, 0.0)
        lmat[c, t, d] = l
        amat[c, t, d] = qk[c, t] * decay
        xinv[c, t, d] = eye - jnp.where(same[2], l, 0.0)

    for msk in level_masks:
        ys, xbs = {}, {}
        for u in units:
            e = jnp.where(msk, lmat[u], 0.0).astype(BF16)
            xbs[u] = xinv[u].astype(BF16)
            ys[u] = _dot(e, block_diag(xbs[u]))
        for u in units:
            xinv[u] = xinv[u] - _dot(xbs[u], block_diag(ys[u].astype(BF16)))

    for c, t, d in units:
        rs = slice(c * C, (c + 1) * C)
        kd_rows, rhs_rows = [], []
        for h in range(T):
            head = t * T + h
            col = d * DN_HEADS + head
            hs = slice(h * DN_DK, (h + 1) * DN_DK)
            out_s = slice(head * DN_DK, (head + 1) * DN_DK)
            gc = jnp.broadcast_to(gcol[d][rs, 2 * DN_HEADS + col:2 * DN_HEADS + col + 1],
                                  (C, DN_DK))
            beta = jnp.broadcast_to(beta_all[rs, col:col + 1], (C, DN_DK))
            g_last = gc[C - 1:C, :] if d == 0 else gc[0:1, :]
            eg = jnp.exp(gc)
            kf = k4s[c, t][:, hs].astype(F32)
            rhs = jnp.concatenate([kf * (beta * eg), v4s[c, t][:, hs].astype(F32) * beta],
                                  axis=1).astype(BF16)
            zero = jnp.zeros_like(rhs)
            rhs_rows.append(jnp.concatenate([rhs if hh == h else zero for hh in range(T)], axis=1))
            kd_rows.append(kf * jnp.exp(g_last - gc))
            a1_ref[c, d, C:2 * C, out_s] = (q4s[c, t][:, hs].astype(F32) * eg).astype(BF16)
            ae_ref[c, d, :, out_s] = jnp.exp(g_last)
        wu = _dot(xinv[c, t, d].astype(BF16), jnp.concatenate(rhs_rows, axis=0))
        for h in range(T):
            head = t * T + h
            out_s = slice(head * DN_DK, (head + 1) * DN_DK)
            a1_ref[c, d, 0:C, out_s] = wu[:, 2 * h * DN_DK:(2 * h + 1) * DN_DK].astype(BF16)
            au_ref[c, d, :, out_s] = wu[:, (2 * h + 1) * DN_DK:(2 * h + 2) * DN_DK].astype(BF16)
        cat_s = slice(t * W, (t + 1) * W)
        a2_ref[c, d, 0:C, cat_s] = amat[c, t, d].astype(BF16)
        a2_ref[c, d, C:C + DN_DK, cat_s] = jnp.concatenate(kd_rows, axis=0).T.astype(BF16)


def _dchunk(proj, small, small_r, par_col, par_row, offs, batch, seq, cpb=4):
    assert N_TILES == 2
    nc = seq // CHUNK
    steps = nc // cpb
    tw = HEADS_PER_TILE * DN_DK
    kern = functools.partial(_dchunk_kernel, cpb=cpb)
    out5 = lambda b, i: (b, i, 0, 0, 0)
    tile_cols = [offs["dq"], offs["dq"] + tw, offs["dk0"], offs["dk1"], offs["dv"], offs["dv"] + tw]
    qkv_specs = [pl.BlockSpec((cpb * CHUNK, tw), functools.partial(
        lambda b, i, col: (b * steps + i, col), col=c // tw)) for c in tile_cols]
    return pl.pallas_call(
        kern,
        grid=(batch, steps),
        in_specs=qkv_specs + [
            pl.BlockSpec((cpb * CHUNK, LANES), lambda b, i: (b * steps + i, 0)),
            pl.BlockSpec((cpb, 8, CAT_W), lambda b, i: (b * steps + i, 0, 0)),
            pl.BlockSpec((8, LANES), lambda b, i: (0, 0)),
            pl.BlockSpec((2, 8, CAT_W), lambda b, i: (0, 0, 0)),
        ],
        out_specs=[
            pl.BlockSpec((None, cpb, 2, 2 * CHUNK, DN_KW), out5),
            pl.BlockSpec((None, cpb, 2, CHUNK + DN_DK, DN_HEADS * CHUNK), out5),
            pl.BlockSpec((None, cpb, 2, CHUNK, DN_VW), out5),
            pl.BlockSpec((None, cpb, 2, 1, DN_VW), out5),
        ],
        out_shape=[
            jax.ShapeDtypeStruct((batch, nc, 2, 2 * CHUNK, DN_KW), BF16),
            jax.ShapeDtypeStruct((batch, nc, 2, CHUNK + DN_DK, DN_HEADS * CHUNK), BF16),
            jax.ShapeDtypeStruct((batch, nc, 2, CHUNK, DN_VW), BF16),
            jax.ShapeDtypeStruct((batch, nc, 2, 1, DN_VW), F32),
        ],
        compiler_params=pltpu.CompilerParams(dimension_semantics=("parallel", "parallel")),
        name="dchunk",
    )(proj, proj, proj, proj, proj, proj, small, small_r, par_col, par_row)


def _dscan_kernel(a1f_ref, a1b_ref, a2f_ref, a2b_ref, auf_ref, aub_ref, aef_ref, aeb_ref,
                  of_ref, ob_ref, s_ref, *, batch, cps):
    C = CHUNK
    PW = 2 * DN_DV
    n_pairs = DN_HEADS // 2

    @pl.when(pl.program_id(0) == 0)
    def _():
        s_ref[...] = jnp.zeros(s_ref.shape, F32)

    first = lax.broadcasted_iota(jnp.int32, (DN_DK, PW), 1) < DN_DV
    first_c = lax.broadcasted_iota(jnp.int32, (C, PW), 1) < DN_DV
    refs = ((a1f_ref, a2f_ref, auf_ref, aef_ref, of_ref), (a1b_ref, a2b_ref, aub_ref, aeb_ref, ob_ref))
    units = [(b, d, p) for b in range(batch) for d in range(2) for p in range(n_pairs)]
    sidx = lambda u: (u[0] * 2 + u[1]) * n_pairs + u[2]

    states = {u: s_ref[sidx(u)] for u in units}
    for step in range(cps):
        r1, r2 = {}, {}
        cc = (step, cps - 1 - step)
        for u in units:
            b, d, p = u
            r1[u] = _dot(refs[d][0][b, cc[d], :, p * PW:(p + 1) * PW], states[u].astype(BF16))
        for u in units:
            b, d, p = u
            vb = (refs[d][2][b, cc[d], :, p * PW:(p + 1) * PW].astype(F32) - r1[u][:C]).astype(BF16)
            zero = jnp.zeros_like(vb)
            v_bd = jnp.concatenate([jnp.where(first_c, vb, zero), jnp.where(first_c, zero, vb)],
                                   axis=0)
            r2[u] = _dot(refs[d][1][b, cc[d], :, p * 2 * C:(p + 1) * 2 * C], v_bd)
        for u in units:
            b, d, p = u
            refs[d][4][b, cc[d], :, p * PW:(p + 1) * PW] = (r1[u][C:] + r2[u][:C]).astype(BF16)
            e = refs[d][3][b, cc[d], 0:1, p * PW:(p + 1) * PW]
            upd = r2[u][C:]
            s = states[u]
            states[u] = jnp.concatenate(
                [s[:DN_DK] * e + jnp.where(first, upd, 0.0),
                 s[DN_DK:] * e + jnp.where(first, 0.0, upd)], axis=0)
    for u in units:
        s_ref[sidx(u)] = states[u]


def _dscan(a1, a2, au, ae, batch, seq, cps=2):
    nc = seq // CHUNK
    steps = nc // cps
    kern = functools.partial(_dscan_kernel, batch=batch, cps=cps)
    fwd = lambda i: (0, i, 0, 0, 0)
    bwd = lambda i: (0, steps - 1 - i, 1, 0, 0)

    def spec(arr, imap):
        return pl.BlockSpec((batch, cps, None) + arr.shape[3:], imap)

    o_spec = lambda imap: pl.BlockSpec((batch, cps, CHUNK, DN_VW), imap)
    o_shape = jax.ShapeDtypeStruct((batch, nc, CHUNK, DN_VW), BF16)
    return pl.pallas_call(
        kern,
        grid=(steps,),
        in_specs=[spec(a1, fwd), spec(a1, bwd), spec(a2, fwd), spec(a2, bwd),
                  spec(au, fwd), spec(au, bwd), spec(ae, fwd), spec(ae, bwd)],
        out_specs=[o_spec(lambda i: (0, i, 0, 0)), o_spec(lambda i: (0, steps - 1 - i, 0, 0))],
        out_shape=[o_shape, o_shape],
        scratch_shapes=[pltpu.VMEM((batch * 2 * (DN_HEADS // 2), 2 * DN_DK, 2 * DN_DV), F32)],
        compiler_params=pltpu.CompilerParams(dimension_semantics=("arbitrary",)),
        name="dscan",
    )(a1, a1, a2, a2, au, au, ae, ae)


def _post_kernel(oa_ref, of_ref, ob_ref, dz_ref, ga_ref, gd_ref, x_ref, p_ref,
                 wa_ref, wd_ref, wo_ref, wg_ref, wp_ref, nd_ref, npost_ref, nple_ref, o_ref):
    def rms(t, w):
        return t * lax.rsqrt(jnp.mean(t * t, axis=-1, keepdims=True) + EPS) * w

    nd = nd_ref[...]
    parts = []
    for h in range(DN_HEADS):
        sl = slice(h * DN_DV, (h + 1) * DN_DV)
        od = of_ref[:, sl].astype(F32) + ob_ref[:, sl].astype(F32)
        z = dz_ref[:, sl].astype(F32)
        parts.append((rms(od, nd) * (z * jax.nn.sigmoid(z))).astype(BF16))
    y_dn = jnp.dot(jnp.concatenate(parts, axis=1), wd_ref[...], preferred_element_type=F32)
    y_att = jnp.dot(oa_ref[...], wa_ref[...], preferred_element_type=F32)
    merged = (jax.nn.sigmoid(ga_ref[...].astype(F32)) * y_att
              + jax.nn.sigmoid(gd_ref[...].astype(F32)) * y_dn)
    mix = jnp.dot(merged.astype(BF16), wo_ref[...], preferred_element_type=F32)
    x1 = x_ref[...] + rms(mix, npost_ref[...])
    gate = jax.nn.sigmoid(jnp.dot(x1.astype(BF16), wg_ref[...], preferred_element_type=F32))
    e = jnp.dot(p_ref[...].astype(BF16), wp_ref[...], preferred_element_type=F32)
    o_ref[...] = x1 + rms(gate * e, nple_ref[...])


def _post(o_att, o_f, o_b, proj, x2, p2, wa, wd, wo, wg, wp, nd, npost, nple, offs, tm=512):
    n, d = x2.shape
    ple = p2.shape[1]
    row = lambda i: (i, 0)
    const = lambda i: (0, 0)
    return pl.pallas_call(
        _post_kernel,
        grid=(n // tm,),
        in_specs=[
            pl.BlockSpec((tm, ATT_W), row),
            pl.BlockSpec((tm, DN_VW), row),
            pl.BlockSpec((tm, DN_VW), row),
            pl.BlockSpec((tm, DN_VW), lambda i: (i, offs["dz"] // DN_VW)),
            pl.BlockSpec((tm, d), lambda i: (i, offs["ga"] // d)),
            pl.BlockSpec((tm, d), lambda i: (i, offs["gd"] // d)),
            pl.BlockSpec((tm, d), row),
            pl.BlockSpec((tm, ple), row),
            pl.BlockSpec((ATT_W, d), const),
            pl.BlockSpec((DN_VW, d), const),
            pl.BlockSpec((d, d), const),
            pl.BlockSpec((d, d), const),
            pl.BlockSpec((ple, d), const),
            pl.BlockSpec((1, DN_DV), const),
            pl.BlockSpec((1, d), const),
            pl.BlockSpec((1, d), const),
        ],
        out_specs=pl.BlockSpec((tm, d), row),
        out_shape=jax.ShapeDtypeStruct((n, d), F32),
        compiler_params=pltpu.CompilerParams(dimension_semantics=("parallel",)),
        name="post",
    )(o_att, o_f, o_b, proj, proj, proj, x2, p2, wa, wd, wo, wg, wp, nd, npost, nple)


def _rope_tables(seq):
    rows = seq // GRID_W
    row = np.broadcast_to(np.arange(rows)[:, None], (rows, GRID_W)).reshape(seq)
    col = np.broadcast_to(np.arange(GRID_W)[None, :], (rows, GRID_W)).reshape(seq)
    n_freq = HEAD_DIM // 4
    inv_freq = np.float32(ROPE_THETA) ** (-np.arange(n_freq, dtype=np.float32) / np.float32(n_freq))
    ang = np.concatenate([row.astype(np.float32)[:, None] * inv_freq,
                          col.astype(np.float32)[:, None] * inv_freq], axis=-1)
    cos, sin = np.cos(ang), np.sin(ang)
    return (jnp.asarray(np.concatenate([cos, cos], axis=-1), F32),
            jnp.asarray(np.concatenate([-sin, sin], axis=-1), F32))


def kernel(x, p, norm_pre, w_in, q_norm, k_norm, conv_w, a_log, dt_bias, dn_norm, w_br_att,
           w_br_dn, w_out, norm_post, w_ple_proj, w_ple_gate, ple_norm):
    batch, seq, d = x.shape
    depth = w_in.shape[0]
    n = batch * seq
    assert seq % GRID_W == 0 and seq % CHUNK == 0 and d % LANES == 0
    cos, sin = _rope_tables(seq)
    x2 = x.reshape(n, d)
    for i in range(depth):
        w_main, w_small, offs, pieces = _regroup_columns(w_in[i])
        conv_w8 = jnp.pad(conv_w[i], ((0, 8 - CONV_K), (0, 0)))
        proj, small = _proj(x2, norm_pre[i][None, :], w_main, w_small, conv_w8, pieces, seq)

        tq, tk = min(ATT_TQ, seq), min(ATT_TK, seq // 2)
        qt, k_rot, vt = _aprep(proj, cos, sin, _deinterleave(q_norm[i])[None, :],
                               _deinterleave(k_norm[i])[None, :], offs, batch, seq, tq, tk)
        o_att = _attention(proj, qt, k_rot, vt, offs, batch, seq, tq, tk)

        nsm = 4 * DN_HEADS
        nck = n // CHUNK
        small_r = small[:, 2 * DN_HEADS:nsm].reshape(nck, CHUNK, 2, N_TILES, HEADS_PER_TILE)
        small_r = small_r.transpose(0, 2, 3, 4, 1).reshape(nck, 2 * N_TILES, CAT_W)
        small_r = jnp.pad(small_r, ((0, 0), (0, 8 - 2 * N_TILES), (0, 0)))
        par = jnp.stack([a_log[i].reshape(-1), dt_bias[i].reshape(-1)])
        par_col = jnp.pad(jnp.concatenate([jnp.zeros_like(par), par], axis=1),
                          ((0, 6), (0, LANES - nsm)))
        par_row = jnp.repeat(par.reshape(2, 2 * N_TILES, HEADS_PER_TILE), CHUNK, axis=2)
        par_row = jnp.pad(par_row, ((0, 0), (0, 8 - 2 * N_TILES), (0, 0)))
        a1, a2, au, ae = _dchunk(proj, small, small_r, par_col, par_row, offs, batch, seq)
        o_f, o_b = _dscan(a1, a2, au, ae, batch, seq)
        o_f, o_b = o_f.reshape(n, DN_VW), o_b.reshape(n, DN_VW)

        x2 = _post(o_att, o_f, o_b, proj, x2, p[i].reshape(n, -1),
                   w_br_att[i].astype(BF16), w_br_dn[i].astype(BF16), w_out[i].astype(BF16),
                   w_ple_gate[i].astype(BF16), w_ple_proj[i].astype(BF16),
                   dn_norm[i][None, :], norm_post[i][None, :], ple_norm[i][None, :], offs)
    return x2.reshape(batch, seq, d)
```

```python
import functools

import numpy as np
import jax
import jax.numpy as jnp
from jax import lax
from jax.experimental import pallas as pl
from jax.experimental.pallas import tpu as pltpu

F32 = jnp.float32
BF16 = jnp.bfloat16

GRID_W = 64
ATT_HEADS = 8
ATT_KV_HEADS = 2
HEAD_DIM = 128
ROPE_THETA = 10000.0
DN_HEADS = 8
DN_DK = 128
DN_DV = 128
CONV_K = 5
CHUNK = 64
EPS = 1e-6

ATT_W = ATT_HEADS * HEAD_DIM
KV_W = ATT_KV_HEADS * HEAD_DIM
DN_KW = DN_HEADS * DN_DK
DN_VW = DN_HEADS * DN_DV
GROUP = ATT_HEADS // ATT_KV_HEADS
LANES = 128


def _deinterleave(t):
    lead = t.shape[:-1]
    t = t.reshape(lead + (-1, HEAD_DIM // 2, 2))
    return jnp.swapaxes(t, -1, -2).reshape(lead + (-1,))


PROJ_TN = 4352
CONV_HALO = 16
CONV_PIECE = 512


def _regroup_columns(w):
    d_model = w.shape[0]
    sizes = (ATT_W, KV_W, KV_W, ATT_W, DN_KW, DN_KW, DN_VW, 2 * DN_HEADS, 2 * DN_HEADS, DN_VW,
             d_model, d_model)
    assert w.shape[1] == sum(sizes)
    aq, ak, av, az, dq, dk, dv, db, da, dz, ga, gd = jnp.split(w, np.cumsum(sizes)[:-1], axis=1)
    aq0, aq1 = jnp.split(_deinterleave(aq), 2, axis=1)
    az0, az1 = jnp.split(az, 2, axis=1)
    dk0, dk1 = jnp.split(dk, 2, axis=1)
    groups = [("ga", ga, "plain", 0), ("gd", gd, "plain", 0), ("dq", dq, "conv_q", 0),
              ("dk0", dk0, "conv_k", DN_KW), ("aq0", aq0, "plain", 0),
              ("ak", _deinterleave(ak), "plain", 0),
              ("av", av, "plain", 0), ("dk1", dk1, "conv_k", DN_KW + DN_KW // 2),
              ("dz", dz, "plain", 0), ("dv", dv, "conv_v", 2 * DN_KW), ("aq1", aq1, "plain", 0),
              ("az0", az0, "plain", 0), ("az1", az1, "plain", 0)]
    offsets, off = {}, 0
    pieces = [[] for _ in range(sum(g.shape[1] for _, g, _, _ in groups) // PROJ_TN)]
    for name, g, kind, ch in groups:
        offsets[name] = off
        width = g.shape[1]
        tile, local = divmod(off, PROJ_TN)
        assert local + width <= PROJ_TN
        if kind == "plain":
            last = pieces[tile][-1] if pieces[tile] else None
            if last is not None and last[2] == "plain" and last[0] + last[1] == local:
                pieces[tile][-1] = (last[0], last[1] + width, "plain", 0)
            else:
                pieces[tile].append((local, width, "plain", 0))
        else:
            for k in range(0, width, CONV_PIECE):
                pieces[tile].append((local + k, CONV_PIECE, kind, ch + k))
        off += width
    main = jnp.concatenate([g.astype(BF16) for _, g, _, _ in groups], axis=1)
    small = jnp.pad(jnp.concatenate([db, da], axis=1), ((0, 0), (0, LANES - 4 * DN_HEADS)))
    return main, small.astype(BF16), offsets, pieces


def _proj_kernel(x_ref, xp_ref, xn_ref, g_ref, w_ref, ws_ref, cw_ref, o_ref, os_ref, h_ref,
                 *, tiles_per_seq, pieces):
    i, j = pl.program_id(0), pl.program_id(1)
    tm = x_ref.shape[0]
    ext = tm + 2 * CONV_HALO
    half = CONV_K // 2

    def normed(x):
        r = lax.rsqrt(jnp.mean(x * x, axis=-1, keepdims=True) + EPS)
        return (x * r * g_ref[...]).astype(BF16)

    @pl.when(j == 0)
    def _():
        t = i % tiles_per_seq
        h = normed(x_ref[...])
        zero = jnp.zeros((CONV_HALO, x_ref.shape[1]), BF16)
        h_ref[0:CONV_HALO, :] = jnp.where(t == 0, zero, normed(xp_ref[...]))
        h_ref[CONV_HALO:CONV_HALO + tm, :] = h
        h_ref[CONV_HALO + tm:, :] = jnp.where(t == tiles_per_seq - 1, zero, normed(xn_ref[...]))
        os_ref[...] = jnp.dot(h, ws_ref[...], preferred_element_type=F32)

    def tile(tile_pieces):
        for start, width, kind, ch in tile_pieces:
            cols = slice(start, start + width)
            if kind == "plain":
                o_ref[:, cols] = jnp.dot(h_ref[CONV_HALO:CONV_HALO + tm, :], w_ref[:, cols],
                                         preferred_element_type=F32).astype(BF16)
                continue
            r = jnp.dot(h_ref[...], w_ref[:, cols], preferred_element_type=F32)
            taps = cw_ref[:, ch:ch + width]
            acc = r[CONV_HALO:CONV_HALO + tm] * taps[half:half + 1]
            for t in range(CONV_K):
                if t != half:
                    shifted = pltpu.roll(r, (half - t) % ext, axis=0)
                    acc = acc + shifted[CONV_HALO:CONV_HALO + tm] * taps[t:t + 1]
            y = acc * jax.nn.sigmoid(acc)
            for hh in range(width // DN_DK):
                yh = y[:, hh * DN_DK:(hh + 1) * DN_DK]
                if kind != "conv_v":
                    inv = lax.rsqrt(jnp.sum(yh * yh, axis=-1, keepdims=True) + EPS)
                    yh = yh * (inv * DN_DK ** -0.5 if kind == "conv_q" else inv)
                o_ref[:, start + hh * DN_DK:start + (hh + 1) * DN_DK] = yh.astype(BF16)

    for jj, tile_pieces in enumerate(pieces):
        pl.when(j == jj)(functools.partial(tile, tile_pieces))


def _proj(x2, g, w_main, w_small, conv_w8, pieces, seq, tm=1024):
    n, d = x2.shape
    main_w = w_main.shape[1]
    tm = min(tm, seq)
    hb = tm // CONV_HALO
    last_hb = n // CONV_HALO - 1
    kern = functools.partial(_proj_kernel, tiles_per_seq=seq // tm, pieces=pieces)
    return pl.pallas_call(
        kern,
        grid=(n // tm, main_w // PROJ_TN),
        in_specs=[
            pl.BlockSpec((tm, d), lambda i, j: (i, 0)),
            pl.BlockSpec((CONV_HALO, d), lambda i, j: (jnp.maximum(i * hb - 1, 0), 0)),
            pl.BlockSpec((CONV_HALO, d), lambda i, j: (jnp.minimum((i + 1) * hb, last_hb), 0)),
            pl.BlockSpec((1, d), lambda i, j: (0, 0)),
            pl.BlockSpec((d, PROJ_TN), lambda i, j: (0, j)),
            pl.BlockSpec((d, LANES), lambda i, j: (0, 0)),
            pl.BlockSpec(conv_w8.shape, lambda i, j: (0, 0)),
        ],
        out_specs=[
            pl.BlockSpec((tm, PROJ_TN), lambda i, j: (i, j)),
            pl.BlockSpec((tm, LANES), lambda i, j: (i, 0)),
        ],
        out_shape=[
            jax.ShapeDtypeStruct((n, main_w), BF16),
            jax.ShapeDtypeStruct((n, LANES), F32),
        ],
        scratch_shapes=[pltpu.VMEM((tm + 2 * CONV_HALO, d), BF16)],
        compiler_params=pltpu.CompilerParams(dimension_semantics=("parallel", "arbitrary")),
        name="proj",
    )(x2, x2, x2, g, w_main, w_small, conv_w8)


def _norm_rope(xh, w, cos, sin):
    r = lax.rsqrt(jnp.mean(xh * xh, axis=-1, keepdims=True) + EPS)
    xn = xh * r * w
    return xn * cos + pltpu.roll(xn, HEAD_DIM // 2, axis=1) * sin


ATT_TQ = 1024
ATT_TK = 512
VT_ROWS = HEAD_DIM + 16


def _aprep_kernel(q0_ref, q1_ref, k_ref, v_ref, cos_ref, sin_ref, wq_ref, wk_ref,
                  qt_ref, kr_ref, vt_ref):
    cos, sin = cos_ref[...], sin_ref[...]
    scale = HEAD_DIM ** -0.5
    tm = k_ref.shape[0]
    tk = vt_ref.shape[-1]
    for h, q_ref in enumerate((q0_ref, q1_ref)):
        sl = slice(h * HEAD_DIM, (h + 1) * HEAD_DIM)
        kr_ref[:, sl] = _norm_rope(k_ref[:, sl].astype(F32), wk_ref[...], cos, sin).astype(BF16)
        for c in range(tm // tk):
            vt_ref[h, c, 0:HEAD_DIM, :] = v_ref[c * tk:(c + 1) * tk, sl].astype(F32).T.astype(BF16)
            vt_ref[h, c, HEAD_DIM:, :] = jnp.ones((VT_ROWS - HEAD_DIM, tk), BF16)
        for g in range(GROUP):
            qs = slice(g * HEAD_DIM, (g + 1) * HEAD_DIM)
            qr = _norm_rope(q_ref[:, qs].astype(F32), wq_ref[...], cos, sin) * scale
            qt_ref[h, g * tm:(g + 1) * tm, :] = qr.astype(BF16)


def _aprep(proj, cos, sin, wq, wk, offs, batch, seq, tm, tk):
    assert ATT_KV_HEADS == 2
    n = proj.shape[0]
    t = seq // tm
    cpt = tm // tk
    gw = GROUP * HEAD_DIM
    return pl.pallas_call(
        _aprep_kernel,
        grid=(batch, t),
        in_specs=[
            pl.BlockSpec((tm, gw), lambda b, i: (b * t + i, offs["aq0"] // gw)),
            pl.BlockSpec((tm, gw), lambda b, i: (b * t + i, offs["aq1"] // gw)),
            pl.BlockSpec((tm, KV_W), lambda b, i: (b * t + i, offs["ak"] // KV_W)),
            pl.BlockSpec((tm, KV_W), lambda b, i: (b * t + i, offs["av"] // KV_W)),
            pl.BlockSpec((tm, HEAD_DIM), lambda b, i: (i, 0)),
            pl.BlockSpec((tm, HEAD_DIM), lambda b, i: (i, 0)),
            pl.BlockSpec((1, HEAD_DIM), lambda b, i: (0, 0)),
            pl.BlockSpec((1, HEAD_DIM), lambda b, i: (0, 0)),
        ],
        out_specs=[
            pl.BlockSpec((None, None, ATT_KV_HEADS, GROUP * tm, HEAD_DIM),
                         lambda b, i: (b, i, 0, 0, 0)),
            pl.BlockSpec((tm, KV_W), lambda b, i: (b * t + i, 0)),
            pl.BlockSpec((None, ATT_KV_HEADS, cpt, VT_ROWS, tk), lambda b, i: (b, 0, i, 0, 0)),
        ],
        out_shape=[
            jax.ShapeDtypeStruct((batch, t, ATT_KV_HEADS, GROUP * tm, HEAD_DIM), BF16),
            jax.ShapeDtypeStruct((n, KV_W), BF16),
            jax.ShapeDtypeStruct((batch, ATT_KV_HEADS, seq // tk, VT_ROWS, tk), BF16),
        ],
        compiler_params=pltpu.CompilerParams(dimension_semantics=("parallel", "parallel")),
        name="aprep",
    )(proj, proj, proj, proj, cos, sin, wq, wk)


def _attn_kernel(qt_ref, z_ref, k_ref, vt_ref, o_ref, acc_ref, s0_ref, s1_ref, p0_ref, p1_ref,
                 *, tq, tk, seq):
    cols = GROUP * tq
    n_chunks = seq // tk
    qt = qt_ref[...]

    def scores(c, s_ref):
        start = pl.multiple_of(c * tk, tk)
        s = lax.dot_general(k_ref[pl.ds(start, tk), :], qt, (((1,), (1,)), ((), ())),
                            preferred_element_type=F32)
        s_ref[...] = s
        return jnp.max(s, axis=0, keepdims=True)

    def softmax(s_ref, p_ref, m, smax):
        m_new = jnp.maximum(m, smax)
        p_ref[...] = jnp.exp((s_ref[...] - m_new).astype(BF16))
        return m_new, jnp.exp(m - m_new)

    def accumulate(c, p_ref, alpha, init=False):
        pv = jnp.dot(vt_ref[c], p_ref[...], preferred_element_type=F32)
        acc_ref[...] = pv if init else alpha * acc_ref[...] + pv

    def pair(c, carry, first, last):
        m, alpha_prev, smax0 = carry
        m, alpha0 = softmax(s0_ref, p0_ref, m, smax0)
        smax1 = scores(c + 1, s1_ref)
        if not first:
            accumulate(c - 1, p1_ref, alpha_prev)
        m, alpha1 = softmax(s1_ref, p1_ref, m, smax1)
        if not last:
            smax0 = scores(c + 2, s0_ref)
        accumulate(c, p0_ref, alpha0, init=first)
        return m, alpha1, smax0

    n_pairs = n_chunks // 2
    smax = scores(0, s0_ref)
    carry = (jnp.full((1, cols), -jnp.inf, F32), jnp.ones((1, cols), F32), smax)
    carry = pair(0, carry, True, n_pairs == 1)
    if n_pairs > 1:
        carry = lax.fori_loop(1, n_pairs - 1, lambda j, cr: pair(2 * j, cr, False, False), carry)
        carry = pair(n_chunks - 2, carry, False, True)
    accumulate(n_chunks - 1, p1_ref, carry[1])
    acc = acc_ref[...]
    o_t = acc[:HEAD_DIM] / acc[HEAD_DIM:HEAD_DIM + 1]
    for g in range(GROUP):
        sl = slice(g * HEAD_DIM, (g + 1) * HEAD_DIM)
        z = z_ref[:, sl].astype(F32)
        o_ref[:, sl] = (o_t[:, g * tq:(g + 1) * tq].T * (z * jax.nn.sigmoid(z))).astype(BF16)


def _attention(proj, qt, k_rot, vt, offs, batch, seq, tq, tk):
    n = proj.shape[0]
    gw = GROUP * HEAD_DIM
    t = seq // tq
    cols = GROUP * tq
    assert (seq // tk) % 2 == 0 and offs["az1"] == offs["az0"] + gw
    kern = functools.partial(_attn_kernel, tq=tq, tk=tk, seq=seq)
    return pl.pallas_call(
        kern,
        grid=(batch, ATT_KV_HEADS, t),
        in_specs=[
            pl.BlockSpec((None, None, None, cols, HEAD_DIM), lambda b, h, i: (b, i, h, 0, 0)),
            pl.BlockSpec((tq, gw), lambda b, h, i: (b * t + i, offs["az0"] // gw + h)),
            pl.BlockSpec((seq, HEAD_DIM), lambda b, h, i: (b, h)),
            pl.BlockSpec((None, None, seq // tk, VT_ROWS, tk), lambda b, h, i: (b, h, 0, 0, 0)),
        ],
        out_specs=pl.BlockSpec((tq, gw), lambda b, h, i: (b * t + i, h)),
        out_shape=jax.ShapeDtypeStruct((n, ATT_W), BF16),
        scratch_shapes=[pltpu.VMEM((VT_ROWS, cols), F32),
                        pltpu.VMEM((tk, cols), F32), pltpu.VMEM((tk, cols), F32),
                        pltpu.VMEM((tk, cols), BF16), pltpu.VMEM((tk, cols), BF16)],
        compiler_params=pltpu.CompilerParams(
            dimension_semantics=("parallel", "parallel", "arbitrary")),
        name="attn",
    )(qt, proj, k_rot, vt)


HEADS_PER_TILE = 4
CAT_W = HEADS_PER_TILE * CHUNK
N_TILES = DN_HEADS // HEADS_PER_TILE


def _split3(x):
    p1 = x.astype(BF16)
    r1 = x - p1.astype(F32)
    p2 = r1.astype(BF16)
    p3 = (r1 - p2.astype(F32)).astype(BF16)
    return p1, p2, p3


def _softplus(x):
    return jnp.maximum(x, 0.0) + jnp.log(1.0 + jnp.exp(-jnp.abs(x)))


def _dot(a, b):
    return jnp.dot(a, b, preferred_element_type=F32)


def _dchunk_kernel(q0_ref, q1_ref, k0_ref, k1_ref, v0_ref, v1_ref, c_ref, r_ref, pc_ref, pr_ref,
                   a1_ref, a2_ref, au_ref, ae_ref, *, cpb):
    C, W, T = CHUNK, CAT_W, HEADS_PER_TILE
    rows = cpb * C

    def cum_matrix(n, lower):
        i = lax.broadcasted_iota(jnp.int32, (n, n), 0)
        j = lax.broadcasted_iota(jnp.int32, (n, n), 1)
        same = (i >> 6) == (j >> 6)
        return (same & ((i >= j) if lower else (i <= j))).astype(BF16)

    col_lower, col_upper = cum_matrix(rows, True), cum_matrix(rows, False)
    row_lower, row_upper = cum_matrix(W, True), cum_matrix(W, False)

    blk = c_ref[...]
    beta_all = jax.nn.sigmoid(blk)
    g_parts = _split3(-jnp.exp(pc_ref[0:1, :]) * _softplus(blk + pc_ref[1:2, :]))
    gcol = (sum(_dot(col_lower, p) for p in g_parts),
            sum(_dot(col_upper, p) for p in g_parts))
    blk_r = r_ref[...]
    g_r = (-jnp.exp(pr_ref[0][None]) * _softplus(blk_r + pr_ref[1][None])).reshape(cpb * 8, W)
    gr_parts = _split3(g_r)
    grow = (sum(_dot(p, row_upper) for p in gr_parts),
            sum(_dot(p, row_lower) for p in gr_parts))

    ri = lax.broadcasted_iota(jnp.int32, (C, W), 0)
    li = lax.broadcasted_iota(jnp.int32, (C, W), 1)
    ci = li & (C - 1)
    lb = li >> 6
    incl = (ri >= ci, ri <= ci)
    offdiag = ri != ci
    eye = (ri == ci).astype(F32)
    same = {s: (ri >> lg) == (ci >> lg) for lg, s in ((1, 2), (2, 4), (3, 8), (4, 16), (5, 32))}
    level_masks = [same[4] & ~same[2], same[8] & ~same[4], same[16] & ~same[8],
                   same[32] & ~same[16], ~same[32]]
    tile_sel = [lb == t for t in range(T)]
    head_of_lane = lax.broadcasted_iota(jnp.int32, (C, T * DN_DK), 1) >> 7

    def block_diag(y):
        zero = jnp.zeros_like(y)
        return jnp.concatenate([jnp.where(tile_sel[t], y, zero) for t in range(T)], axis=0)

    def cat_bcast(cols):
        out = jnp.broadcast_to(cols[T - 1], (C, W))
        for t in range(T - 2, -1, -1):
            out = jnp.where(tile_sel[t], jnp.broadcast_to(cols[t], (C, W)), out)
        return out

    pairs = [(c, t) for c in range(cpb) for t in range(N_TILES)]
    units = [(c, t, d) for c in range(cpb) for t in range(N_TILES) for d in range(2)]

    qk, kk, q4s, k4s, v4s = {}, {}, {}, {}, {}
    for c, t in pairs:
        rs = slice(c * C, (c + 1) * C)
        q4 = (q0_ref, q1_ref)[t][rs, :]
        k4 = (k0_ref, k1_ref)[t][rs, :]
        v4 = (v0_ref, v1_ref)[t][rs, :]
        zero = jnp.zeros_like(k4)
        k_bd = jnp.concatenate([jnp.where(head_of_lane == h, k4, zero) for h in range(T)], axis=0)
        r = lax.dot_general(jnp.concatenate([q4, k4], axis=0), k_bd, (((1,), (1,)), ((), ())),
                            preferred_element_type=F32)
        qk[c, t], kk[c, t] = r[:C], r[C:]
        q4s[c, t], k4s[c, t], v4s[c, t] = q4, k4, v4

    lmat, amat, xinv = {}, {}, {}
    for c, t, d in units:
        rs = slice(c * C, (c + 1) * C)
        base = d * DN_HEADS + t * T
        gc = cat_bcast([gcol[d][rs, 2 * DN_HEADS + base + h:2 * DN_HEADS + base + h + 1]
                        for h in range(T)])
        beta = cat_bcast([beta_all[rs, base + h:base + h + 1] for h in range(T)])
        row = c * 8 + d * N_TILES + t
        gr = jnp.broadcast_to(grow[d][row:row + 1, :], (C, W))
        decay = jnp.exp(jnp.where(incl[d], gc - gr, -1e30))
        l = jnp.where(offdiag, beta * kk[c, t] * decay, 0.0)
        lmat[c, t, d] = l
        amat[c, t, d] = qk[c, t] * decay
        xinv[c, t, d] = eye - jnp.where(same[2], l, 0.0)

    for msk in level_masks:
        ys, xbs = {}, {}
        for u in units:
            e = jnp.where(msk, lmat[u], 0.0).astype(BF16)
            xbs[u] = xinv[u].astype(BF16)
            ys[u] = _dot(e, block_diag(xbs[u]))
        for u in units:
            xinv[u] = xinv[u] - _dot(xbs[u], block_diag(ys[u].astype(BF16)))

    for c, t, d in units:
        rs = slice(c * C, (c + 1) * C)
        kd_rows, rhs_rows = [], []
        for h in range(T):
            head = t * T + h
            col = d * DN_HEADS + head
            hs = slice(h * DN_DK, (h + 1) * DN_DK)
            out_s = slice(head * DN_DK, (head + 1) * DN_DK)
            gc = jnp.broadcast_to(gcol[d][rs, 2 * DN_HEADS + col:2 * DN_HEADS + col + 1],
                                  (C, DN_DK))
            beta = jnp.broadcast_to(beta_all[rs, col:col + 1], (C, DN_DK))
            g_last = gc[C - 1:C, :] if d == 0 else gc[0:1, :]
            eg = jnp.exp(gc)
            kf = k4s[c, t][:, hs].astype(F32)
            rhs = jnp.concatenate([kf * (beta * eg), v4s[c, t][:, hs].astype(F32) * beta],
                                  axis=1).astype(BF16)
            zero = jnp.zeros_like(rhs)
            rhs_rows.append(jnp.concatenate([rhs if hh == h else zero for hh in range(T)], axis=1))
            kd_rows.append(kf * jnp.exp(g_last - gc))
            a1_ref[c, d, C:2 * C, out_s] = (q4s[c, t][:, hs].astype(F32) * eg).astype(BF16)
            ae_ref[c, d, :, out_s] = jnp.exp(g_last)
        wu = _dot(xinv[c, t, d].astype(BF16), jnp.concatenate(rhs_rows, axis=0))
        for h in range(T):
            head = t * T + h
            out_s = slice(head * DN_DK, (head + 1) * DN_DK)
            a1_ref[c, d, 0:C, out_s] = wu[:, 2 * h * DN_DK:(2 * h + 1) * DN_DK].astype(BF16)
            au_ref[c, d, :, out_s] = wu[:, (2 * h + 1) * DN_DK:(2 * h + 2) * DN_DK].astype(BF16)
        cat_s = slice(t * W, (t + 1) * W)
        a2_ref[c, d, 0:C, cat_s] = amat[c, t, d].astype(BF16)
        a2_ref[c, d, C:C + DN_DK, cat_s] = jnp.concatenate(kd_rows, axis=0).T.astype(BF16)


def _dchunk(proj, small, small_r, par_col, par_row, offs, batch, seq, cpb=4):
    assert N_TILES == 2
    nc = seq // CHUNK
    steps = nc // cpb
    tw = HEADS_PER_TILE * DN_DK
    kern = functools.partial(_dchunk_kernel, cpb=cpb)
    out5 = lambda b, i: (b, i, 0, 0, 0)
    tile_cols = [offs["dq"], offs["dq"] + tw, offs["dk0"], offs["dk1"], offs["dv"], offs["dv"] + tw]
    qkv_specs = [pl.BlockSpec((cpb * CHUNK, tw), functools.partial(
        lambda b, i, col: (b * steps + i, col), col=c // tw)) for c in tile_cols]
    return pl.pallas_call(
        kern,
        grid=(batch, steps),
        in_specs=qkv_specs + [
            pl.BlockSpec((cpb * CHUNK, LANES), lambda b, i: (b * steps + i, 0)),
            pl.BlockSpec((cpb, 8, CAT_W), lambda b, i: (b * steps + i, 0, 0)),
            pl.BlockSpec((8, LANES), lambda b, i: (0, 0)),
            pl.BlockSpec((2, 8, CAT_W), lambda b, i: (0, 0, 0)),
        ],
        out_specs=[
            pl.BlockSpec((None, cpb, 2, 2 * CHUNK, DN_KW), out5),
            pl.BlockSpec((None, cpb, 2, CHUNK + DN_DK, DN_HEADS * CHUNK), out5),
            pl.BlockSpec((None, cpb, 2, CHUNK, DN_VW), out5),
            pl.BlockSpec((None, cpb, 2, 1, DN_VW), out5),
        ],
        out_shape=[
            jax.ShapeDtypeStruct((batch, nc, 2, 2 * CHUNK, DN_KW), BF16),
            jax.ShapeDtypeStruct((batch, nc, 2, CHUNK + DN_DK, DN_HEADS * CHUNK), BF16),
            jax.ShapeDtypeStruct((batch, nc, 2, CHUNK, DN_VW), BF16),
            jax.ShapeDtypeStruct((batch, nc, 2, 1, DN_VW), F32),
        ],
        compiler_params=pltpu.CompilerParams(dimension_semantics=("parallel", "parallel")),
        name="dchunk",
    )(proj, proj, proj, proj, proj, proj, small, small_r, par_col, par_row)


def _dscan_kernel(a1f_ref, a1b_ref, a2f_ref, a2b_ref, auf_ref, aub_ref, aef_ref, aeb_ref,
                  of_ref, ob_ref, s_ref, *, batch, cps):
    C = CHUNK
    PW = 2 * DN_DV
    n_pairs = DN_HEADS // 2

    @pl.when(pl.program_id(0) == 0)
    def _():
        s_ref[...] = jnp.zeros(s_ref.shape, F32)

    first = lax.broadcasted_iota(jnp.int32, (DN_DK, PW), 1) < DN_DV
    first_c = lax.broadcasted_iota(jnp.int32, (C, PW), 1) < DN_DV
    refs = ((a1f_ref, a2f_ref, auf_ref, aef_ref, of_ref), (a1b_ref, a2b_ref, aub_ref, aeb_ref, ob_ref))
    units = [(b, d, p) for b in range(batch) for d in range(2) for p in range(n_pairs)]
    sidx = lambda u: (u[0] * 2 + u[1]) * n_pairs + u[2]

    states = {u: s_ref[sidx(u)] for u in units}
    for step in range(cps):
        r1, r2 = {}, {}
        cc = (step, cps - 1 - step)
        for u in units:
            b, d, p = u
            r1[u] = _dot(refs[d][0][b, cc[d], :, p * PW:(p + 1) * PW], states[u].astype(BF16))
        for u in units:
            b, d, p = u
            vb = (refs[d][2][b, cc[d], :, p * PW:(p + 1) * PW].astype(F32) - r1[u][:C]).astype(BF16)
            zero = jnp.zeros_like(vb)
            v_bd = jnp.concatenate([jnp.where(first_c, vb, zero), jnp.where(first_c, zero, vb)],
                                   axis=0)
            r2[u] = _dot(refs[d][1][b, cc[d], :, p * 2 * C:(p + 1) * 2 * C], v_bd)
        for u in units:
            b, d, p = u
            refs[d][4][b, cc[d], :, p * PW:(p + 1) * PW] = (r1[u][C:] + r2[u][:C]).astype(BF16)
            e = refs[d][3][b, cc[d], 0:1, p * PW:(p + 1) * PW]
            upd = r2[u][C:]
            s = states[u]
            states[u] = jnp.concatenate(
                [s[:DN_DK] * e + jnp.where(first, upd, 0.0),
                 s[DN_DK:] * e + jnp.where(first, 0.0, upd)], axis=0)
    for u in units:
        s_ref[sidx(u)] = states[u]


def _dscan(a1, a2, au, ae, batch, seq, cps=2):
    nc = seq // CHUNK
    steps = nc // cps
    kern = functools.partial(_dscan_kernel, batch=batch, cps=cps)
    fwd = lambda i: (0, i, 0, 0, 0)
    bwd = lambda i: (0, steps - 1 - i, 1, 0, 0)

    def spec(arr, imap):
        return pl.BlockSpec((batch, cps, None) + arr.shape[3:], imap)

    o_spec = lambda imap: pl.BlockSpec((batch, cps, CHUNK, DN_VW), imap)
    o_shape = jax.ShapeDtypeStruct((batch, nc, CHUNK, DN_VW), BF16)
    return pl.pallas_call(
        kern,
        grid=(steps,),
        in_specs=[spec(a1, fwd), spec(a1, bwd), spec(a2, fwd), spec(a2, bwd),
                  spec(au, fwd), spec(au, bwd), spec(ae, fwd), spec(ae, bwd)],
        out_specs=[o_spec(lambda i: (0, i, 0, 0)), o_spec(lambda i: (0, steps - 1 - i, 0, 0))],
        out_shape=[o_shape, o_shape],
        scratch_shapes=[pltpu.VMEM((batch * 2 * (DN_HEADS // 2), 2 * DN_DK, 2 * DN_DV), F32)],
        compiler_params=pltpu.CompilerParams(dimension_semantics=("arbitrary",)),
        name="dscan",
    )(a1, a1, a2, a2, au, au, ae, ae)


def _post_kernel(oa_ref, of_ref, ob_ref, dz_ref, ga_ref, gd_ref, x_ref, p_ref,
                 wa_ref, wd_ref, wo_ref, wg_ref, wp_ref, nd_ref, npost_ref, nple_ref, o_ref):
    def rms(t, w):
        return t * lax.rsqrt(jnp.mean(t * t, axis=-1, keepdims=True) + EPS) * w

    nd = nd_ref[...]
    parts = []
    for h in range(DN_HEADS):
        sl = slice(h * DN_DV, (h + 1) * DN_DV)
        od = of_ref[:, sl].astype(F32) + ob_ref[:, sl].astype(F32)
        z = dz_ref[:, sl].astype(F32)
        parts.append((rms(od, nd) * (z * jax.nn.sigmoid(z))).astype(BF16))
    y_dn = jnp.dot(jnp.concatenate(parts, axis=1), wd_ref[...], preferred_element_type=F32)
    y_att = jnp.dot(oa_ref[...], wa_ref[...], preferred_element_type=F32)
    merged = (jax.nn.sigmoid(ga_ref[...].astype(F32)) * y_att
              + jax.nn.sigmoid(gd_ref[...].astype(F32)) * y_dn)
    mix = jnp.dot(merged.astype(BF16), wo_ref[...], preferred_element_type=F32)
    x1 = x_ref[...] + rms(mix, npost_ref[...])
    gate = jax.nn.sigmoid(jnp.dot(x1.astype(BF16), wg_ref[...], preferred_element_type=F32))
    e = jnp.dot(p_ref[...].astype(BF16), wp_ref[...], preferred_element_type=F32)
    o_ref[...] = x1 + rms(gate * e, nple_ref[...])


def _post(o_att, o_f, o_b, proj, x2, p2, wa, wd, wo, wg, wp, nd, npost, nple, offs, tm=512):
    n, d = x2.shape
    ple = p2.shape[1]
    row = lambda i: (i, 0)
    const = lambda i: (0, 0)
    return pl.pallas_call(
        _post_kernel,
        grid=(n // tm,),
        in_specs=[
            pl.BlockSpec((tm, ATT_W), row),
            pl.BlockSpec((tm, DN_VW), row),
            pl.BlockSpec((tm, DN_VW), row),
            pl.BlockSpec((tm, DN_VW), lambda i: (i, offs["dz"] // DN_VW)),
            pl.BlockSpec((tm, d), lambda i: (i, offs["ga"] // d)),
            pl.BlockSpec((tm, d), lambda i: (i, offs["gd"] // d)),
            pl.BlockSpec((tm, d), row),
            pl.BlockSpec((tm, ple), row),
            pl.BlockSpec((ATT_W, d), const),
            pl.BlockSpec((DN_VW, d), const),
            pl.BlockSpec((d, d), const),
            pl.BlockSpec((d, d), const),
            pl.BlockSpec((ple, d), const),
            pl.BlockSpec((1, DN_DV), const),
            pl.BlockSpec((1, d), const),
            pl.BlockSpec((1, d), const),
        ],
        out_specs=pl.BlockSpec((tm, d), row),
        out_shape=jax.ShapeDtypeStruct((n, d), F32),
        compiler_params=pltpu.CompilerParams(dimension_semantics=("parallel",)),
        name="post",
    )(o_att, o_f, o_b, proj, proj, proj, x2, p2, wa, wd, wo, wg, wp, nd, npost, nple)


def _rope_tables(seq):
    rows = seq // GRID_W
    row = np.broadcast_to(np.arange(rows)[:, None], (rows, GRID_W)).reshape(seq)
    col = np.broadcast_to(np.arange(GRID_W)[None, :], (rows, GRID_W)).reshape(seq)
    n_freq = HEAD_DIM // 4
    inv_freq = np.float32(ROPE_THETA) ** (-np.arange(n_freq, dtype=np.float32) / np.float32(n_freq))
    ang = np.concatenate([row.astype(np.float32)[:, None] * inv_freq,
                          col.astype(np.float32)[:, None] * inv_freq], axis=-1)
    cos, sin = np.cos(ang), np.sin(ang)
    return (jnp.asarray(np.concatenate([cos, cos], axis=-1), F32),
            jnp.asarray(np.concatenate([-sin, sin], axis=-1), F32))


def kernel(x, p, norm_pre, w_in, q_norm, k_norm, conv_w, a_log, dt_bias, dn_norm, w_br_att,
           w_br_dn, w_out, norm_post, w_ple_proj, w_ple_gate, ple_norm):
    batch, seq, d = x.shape
    depth = w_in.shape[0]
    n = batch * seq
    assert seq % GRID_W == 0 and seq % CHUNK == 0 and d % LANES == 0
    cos, sin = _rope_tables(seq)
    x2 = x.reshape(n, d)
    for i in range(depth):
        w_main, w_small, offs, pieces = _regroup_columns(w_in[i])
        conv_w8 = jnp.pad(conv_w[i], ((0, 8 - CONV_K), (0, 0)))
        proj, small = _proj(x2, norm_pre[i][None, :], w_main, w_small, conv_w8, pieces, seq)

        tq, tk = min(ATT_TQ, seq), min(ATT_TK, seq // 2)
        qt, k_rot, vt = _aprep(proj, cos, sin, _deinterleave(q_norm[i])[None, :],
                               _deinterleave(k_norm[i])[None, :], offs, batch, seq, tq, tk)
        o_att = _attention(proj, qt, k_rot, vt, offs, batch, seq, tq, tk)

        nsm = 4 * DN_HEADS
        nck = n // CHUNK
        small_r = small[:, 2 * DN_HEADS:nsm].reshape(nck, CHUNK, 2, N_TILES, HEADS_PER_TILE)
        small_r = small_r.transpose(0, 2, 3, 4, 1).reshape(nck, 2 * N_TILES, CAT_W)
        small_r = jnp.pad(small_r, ((0, 0), (0, 8 - 2 * N_TILES), (0, 0)))
        par = jnp.stack([a_log[i].reshape(-1), dt_bias[i].reshape(-1)])
        par_col = jnp.pad(jnp.concatenate([jnp.zeros_like(par), par], axis=1),
                          ((0, 6), (0, LANES - nsm)))
        par_row = jnp.repeat(par.reshape(2, 2 * N_TILES, HEADS_PER_TILE), CHUNK, axis=2)
        par_row = jnp.pad(par_row, ((0, 0), (0, 8 - 2 * N_TILES), (0, 0)))
        a1, a2, au, ae = _dchunk(proj, small, small_r, par_col, par_row, offs, batch, seq)
        o_f, o_b = _dscan(a1, a2, au, ae, batch, seq)
        o_f, o_b = o_f.reshape(n, DN_VW), o_b.reshape(n, DN_VW)

        x2 = _post(o_att, o_f, o_b, proj, x2, p[i].reshape(n, -1),
                   w_br_att[i].astype(BF16), w_br_dn[i].astype(BF16), w_out[i].astype(BF16),
                   w_ple_gate[i].astype(BF16), w_ple_proj[i].astype(BF16),
                   dn_norm[i][None, :], norm_post[i][None, :], ple_norm[i][None, :], offs)
    return x2.reshape(batch, seq, d)
```

```python
import functools

import numpy as np
import jax
import jax.numpy as jnp
from jax import lax
from jax.experimental import pallas as pl
from jax.experimental.pallas import tpu as pltpu

F32 = jnp.float32
BF16 = jnp.bfloat16

GRID_W = 64
ATT_HEADS = 8
ATT_KV_HEADS = 2
HEAD_DIM = 128
ROPE_THETA = 10000.0
DN_HEADS = 8
DN_DK = 128
DN_DV = 128
CONV_K = 5
CHUNK = 64
EPS = 1e-6

ATT_W = ATT_HEADS * HEAD_DIM
KV_W = ATT_KV_HEADS * HEAD_DIM
DN_KW = DN_HEADS * DN_DK
DN_VW = DN_HEADS * DN_DV
GROUP = ATT_HEADS // ATT_KV_HEADS
LANES = 128


def _deinterleave(t):
    lead = t.shape[:-1]
    t = t.reshape(lead + (-1, HEAD_DIM // 2, 2))
    return jnp.swapaxes(t, -1, -2).reshape(lead + (-1,))


PROJ_TN = 4352
CONV_HALO = 16
CONV_PIECE = 512


def _regroup_columns(w):
    d_model = w.shape[0]
    sizes = (ATT_W, KV_W, KV_W, ATT_W, DN_KW, DN_KW, DN_VW, 2 * DN_HEADS, 2 * DN_HEADS, DN_VW,
             d_model, d_model)
    assert w.shape[1] == sum(sizes)
    aq, ak, av, az, dq, dk, dv, db, da, dz, ga, gd = jnp.split(w, np.cumsum(sizes)[:-1], axis=1)
    aq0, aq1 = jnp.split(_deinterleave(aq), 2, axis=1)
    az0, az1 = jnp.split(az, 2, axis=1)
    dk0, dk1 = jnp.split(dk, 2, axis=1)
    groups = [("ga", ga, "plain", 0), ("gd", gd, "plain", 0), ("dq", dq, "conv_q", 0),
              ("dk0", dk0, "conv_k", DN_KW), ("aq0", aq0, "plain", 0),
              ("ak", _deinterleave(ak), "plain", 0),
              ("av", av, "plain", 0), ("dk1", dk1, "conv_k", DN_KW + DN_KW // 2),
              ("dz", dz, "plain", 0), ("dv", dv, "conv_v", 2 * DN_KW), ("aq1", aq1, "plain", 0),
              ("az0", az0, "plain", 0), ("az1", az1, "plain", 0)]
    offsets, off = {}, 0
    pieces = [[] for _ in range(sum(g.shape[1] for _, g, _, _ in groups) // PROJ_TN)]
    for name, g, kind, ch in groups:
        offsets[name] = off
        width = g.shape[1]
        tile, local = divmod(off, PROJ_TN)
        assert local + width <= PROJ_TN
        if kind == "plain":
            last = pieces[tile][-1] if pieces[tile] else None
            if last is not None and last[2] == "plain" and last[0] + last[1] == local:
                pieces[tile][-1] = (last[0], last[1] + width, "plain", 0)
            else:
                pieces[tile].append((local, width, "plain", 0))
        else:
            for k in range(0, width, CONV_PIECE):
                pieces[tile].append((local + k, CONV_PIECE, kind, ch + k))
        off += width
    main = jnp.concatenate([g.astype(BF16) for _, g, _, _ in groups], axis=1)
    small = jnp.pad(jnp.concatenate([db, da], axis=1), ((0, 0), (0, LANES - 4 * DN_HEADS)))
    return main, small.astype(BF16), offsets, pieces


def _proj_kernel(x_ref, xp_ref, xn_ref, g_ref, w_ref, ws_ref, cw_ref, o_ref, os_ref, h_ref,
                 *, tiles_per_seq, pieces):
    i, j = pl.program_id(0), pl.program_id(1)
    tm = x_ref.shape[0]
    ext = tm + 2 * CONV_HALO
    half = CONV_K // 2

    def normed(x):
        r = lax.rsqrt(jnp.mean(x * x, axis=-1, keepdims=True) + EPS)
        return (x * r * g_ref[...]).astype(BF16)

    @pl.when(j == 0)
    def _():
        t = i % tiles_per_seq
        h = normed(x_ref[...])
        zero = jnp.zeros((CONV_HALO, x_ref.shape[1]), BF16)
        h_ref[0:CONV_HALO, :] = jnp.where(t == 0, zero, normed(xp_ref[...]))
        h_ref[CONV_HALO:CONV_HALO + tm, :] = h
        h_ref[CONV_HALO + tm:, :] = jnp.where(t == tiles_per_seq - 1, zero, normed(xn_ref[...]))
        os_ref[...] = jnp.dot(h, ws_ref[...], preferred_element_type=F32)

    def tile(tile_pieces):
        for start, width, kind, ch in tile_pieces:
            cols = slice(start, start + width)
            if kind == "plain":
                o_ref[:, cols] = jnp.dot(h_ref[CONV_HALO:CONV_HALO + tm, :], w_ref[:, cols],
                                         preferred_element_type=F32).astype(BF16)
                continue
            r = jnp.dot(h_ref[...], w_ref[:, cols], preferred_element_type=F32)
            taps = cw_ref[:, ch:ch + width]
            acc = r[CONV_HALO:CONV_HALO + tm] * taps[half:half + 1]
            for t in range(CONV_K):
                if t != half:
                    shifted = pltpu.roll(r, (half - t) % ext, axis=0)
                    acc = acc + shifted[CONV_HALO:CONV_HALO + tm] * taps[t:t + 1]
            y = acc * jax.nn.sigmoid(acc)
            for hh in range(width // DN_DK):
                yh = y[:, hh * DN_DK:(hh + 1) * DN_DK]
                if kind != "conv_v":
                    inv = lax.rsqrt(jnp.sum(yh * yh, axis=-1, keepdims=True) + EPS)
                    yh = yh * (inv * DN_DK ** -0.5 if kind == "conv_q" else inv)
                o_ref[:, start + hh * DN_DK:start + (hh + 1) * DN_DK] = yh.astype(BF16)

    for jj, tile_pieces in enumerate(pieces):
        pl.when(j == jj)(functools.partial(tile, tile_pieces))


def _proj(x2, g, w_main, w_small, conv_w8, pieces, seq, tm=1024):
    n, d = x2.shape
    main_w = w_main.shape[1]
    tm = min(tm, seq)
    hb = tm // CONV_HALO
    last_hb = n // CONV_HALO - 1
    kern = functools.partial(_proj_kernel, tiles_per_seq=seq // tm, pieces=pieces)
    return pl.pallas_call(
        kern,
        grid=(n // tm, main_w // PROJ_TN),
        in_specs=[
            pl.BlockSpec((tm, d), lambda i, j: (i, 0)),
            pl.BlockSpec((CONV_HALO, d), lambda i, j: (jnp.maximum(i * hb - 1, 0), 0)),
            pl.BlockSpec((CONV_HALO, d), lambda i, j: (jnp.minimum((i + 1) * hb, last_hb), 0)),
            pl.BlockSpec((1, d), lambda i, j: (0, 0)),
            pl.BlockSpec((d, PROJ_TN), lambda i, j: (0, j)),
            pl.BlockSpec((d, LANES), lambda i, j: (0, 0)),
            pl.BlockSpec(conv_w8.shape, lambda i, j: (0, 0)),
        ],
        out_specs=[
            pl.BlockSpec((tm, PROJ_TN), lambda i, j: (i, j)),
            pl.BlockSpec((tm, LANES), lambda i, j: (i, 0)),
        ],
        out_shape=[
            jax.ShapeDtypeStruct((n, main_w), BF16),
            jax.ShapeDtypeStruct((n, LANES), F32),
        ],
        scratch_shapes=[pltpu.VMEM((tm + 2 * CONV_HALO, d), BF16)],
        compiler_params=pltpu.CompilerParams(dimension_semantics=("parallel", "arbitrary")),
        name="proj",
    )(x2, x2, x2, g, w_main, w_small, conv_w8)


def _norm_rope(xh, w, cos, sin):
    r = lax.rsqrt(jnp.mean(xh * xh, axis=-1, keepdims=True) + EPS)
    xn = xh * r * w
    return xn * cos + pltpu.roll(xn, HEAD_DIM // 2, axis=1) * sin


ATT_TQ = 1024
ATT_TK = 512
VT_ROWS = HEAD_DIM + 16


def _aprep_kernel(q0_ref, q1_ref, k_ref, v_ref, cos_ref, sin_ref, wq_ref, wk_ref,
                  qt_ref, kr_ref, vt_ref):
    cos, sin = cos_ref[...], sin_ref[...]
    scale = HEAD_DIM ** -0.5
    tm = k_ref.shape[0]
    tk = vt_ref.shape[-1]
    for h, q_ref in enumerate((q0_ref, q1_ref)):
        sl = slice(h * HEAD_DIM, (h + 1) * HEAD_DIM)
        kr_ref[:, sl] = _norm_rope(k_ref[:, sl].astype(F32), wk_ref[...], cos, sin).astype(BF16)
        for c in range(tm // tk):
            vt_ref[h, c, 0:HEAD_DIM, :] = v_ref[c * tk:(c + 1) * tk, sl].astype(F32).T.astype(BF16)
            vt_ref[h, c, HEAD_DIM:, :] = jnp.ones((VT_ROWS - HEAD_DIM, tk), BF16)
        for g in range(GROUP):
            qs = slice(g * HEAD_DIM, (g + 1) * HEAD_DIM)
            qr = _norm_rope(q_ref[:, qs].astype(F32), wq_ref[...], cos, sin) * scale
            qt_ref[h, :, g * tm:(g + 1) * tm] = qr.T.astype(BF16)


def _aprep(proj, cos, sin, wq, wk, offs, batch, seq, tm, tk):
    assert ATT_KV_HEADS == 2
    n = proj.shape[0]
    t = seq // tm
    cpt = tm // tk
    gw = GROUP * HEAD_DIM
    return pl.pallas_call(
        _aprep_kernel,
        grid=(batch, t),
        in_specs=[
            pl.BlockSpec((tm, gw), lambda b, i: (b * t + i, offs["aq0"] // gw)),
            pl.BlockSpec((tm, gw), lambda b, i: (b * t + i, offs["aq1"] // gw)),
            pl.BlockSpec((tm, KV_W), lambda b, i: (b * t + i, offs["ak"] // KV_W)),
            pl.BlockSpec((tm, KV_W), lambda b, i: (b * t + i, offs["av"] // KV_W)),
            pl.BlockSpec((tm, HEAD_DIM), lambda b, i: (i, 0)),
            pl.BlockSpec((tm, HEAD_DIM), lambda b, i: (i, 0)),
            pl.BlockSpec((1, HEAD_DIM), lambda b, i: (0, 0)),
            pl.BlockSpec((1, HEAD_DIM), lambda b, i: (0, 0)),
        ],
        out_specs=[
            pl.BlockSpec((None, None, ATT_KV_HEADS, HEAD_DIM, GROUP * tm),
                         lambda b, i: (b, i, 0, 0, 0)),
            pl.BlockSpec((tm, KV_W), lambda b, i: (b * t + i, 0)),
            pl.BlockSpec((None, ATT_KV_HEADS, cpt, VT_ROWS, tk), lambda b, i: (b, 0, i, 0, 0)),
        ],
        out_shape=[
            jax.ShapeDtypeStruct((batch, t, ATT_KV_HEADS, HEAD_DIM, GROUP * tm), BF16),
            jax.ShapeDtypeStruct((n, KV_W), BF16),
            jax.ShapeDtypeStruct((batch, ATT_KV_HEADS, seq // tk, VT_ROWS, tk), BF16),
        ],
        compiler_params=pltpu.CompilerParams(dimension_semantics=("parallel", "parallel")),
        name="aprep",
    )(proj, proj, proj, proj, cos, sin, wq, wk)


def _attn_kernel(qt_ref, z_ref, k_ref, vt_ref, o_ref, acc_ref, s0_ref, s1_ref, p0_ref, p1_ref,
                 *, tq, tk, seq):
    cols = GROUP * tq
    n_chunks = seq // tk
    qt = qt_ref[...]

    def scores(c, s_ref):
        start = pl.multiple_of(c * tk, tk)
        s = jnp.dot(k_ref[pl.ds(start, tk), :], qt, preferred_element_type=F32)
        s_ref[...] = s
        return jnp.max(s, axis=0, keepdims=True)

    def softmax(s_ref, p_ref, m, smax):
        m_new = jnp.maximum(m, smax)
        p_ref[...] = jnp.exp((s_ref[...] - m_new).astype(BF16))
        return m_new, jnp.exp(m - m_new)

    def accumulate(c, p_ref, alpha, init=False):
        pv = jnp.dot(vt_ref[c], p_ref[...], preferred_element_type=F32)
        acc_ref[...] = pv if init else alpha * acc_ref[...] + pv

    def pair(c, carry, first, last):
        m, alpha_prev, smax0 = carry
        m, alpha0 = softmax(s0_ref, p0_ref, m, smax0)
        smax1 = scores(c + 1, s1_ref)
        if not first:
            accumulate(c - 1, p1_ref, alpha_prev)
        m, alpha1 = softmax(s1_ref, p1_ref, m, smax1)
        if not last:
            smax0 = scores(c + 2, s0_ref)
        accumulate(c, p0_ref, alpha0, init=first)
        return m, alpha1, smax0

    n_pairs = n_chunks // 2
    smax = scores(0, s0_ref)
    carry = (jnp.full((1, cols), -jnp.inf, F32), jnp.ones((1, cols), F32), smax)
    carry = pair(0, carry, True, n_pairs == 1)
    if n_pairs > 1:
        carry = lax.fori_loop(1, n_pairs - 1, lambda j, cr: pair(2 * j, cr, False, False), carry)
        carry = pair(n_chunks - 2, carry, False, True)
    accumulate(n_chunks - 1, p1_ref, carry[1])
    acc = acc_ref[...]
    o_t = acc[:HEAD_DIM] / acc[HEAD_DIM:HEAD_DIM + 1]
    for g in range(GROUP):
        sl = slice(g * HEAD_DIM, (g + 1) * HEAD_DIM)
        z = z_ref[:, sl].astype(F32)
        o_ref[:, sl] = (o_t[:, g * tq:(g + 1) * tq].T * (z * jax.nn.sigmoid(z))).astype(BF16)


def _attention(proj, qt, k_rot, vt, offs, batch, seq, tq, tk):
    n = proj.shape[0]
    gw = GROUP * HEAD_DIM
    t = seq // tq
    cols = GROUP * tq
    assert (seq // tk) % 2 == 0 and offs["az1"] == offs["az0"] + gw
    kern = functools.partial(_attn_kernel, tq=tq, tk=tk, seq=seq)
    return pl.pallas_call(
        kern,
        grid=(batch, ATT_KV_HEADS, t),
        in_specs=[
            pl.BlockSpec((None, None, None, HEAD_DIM, cols), lambda b, h, i: (b, i, h, 0, 0)),
            pl.BlockSpec((tq, gw), lambda b, h, i: (b * t + i, offs["az0"] // gw + h)),
            pl.BlockSpec((seq, HEAD_DIM), lambda b, h, i: (b, h)),
            pl.BlockSpec((None, None, seq // tk, VT_ROWS, tk), lambda b, h, i: (b, h, 0, 0, 0)),
        ],
        out_specs=pl.BlockSpec((tq, gw), lambda b, h, i: (b * t + i, h)),
        out_shape=jax.ShapeDtypeStruct((n, ATT_W), BF16),
        scratch_shapes=[pltpu.VMEM((VT_ROWS, cols), F32),
                        pltpu.VMEM((tk, cols), F32), pltpu.VMEM((tk, cols), F32),
                        pltpu.VMEM((tk, cols), BF16), pltpu.VMEM((tk, cols), BF16)],
        compiler_params=pltpu.CompilerParams(
            dimension_semantics=("parallel", "parallel", "arbitrary")),
        name="attn",
    )(qt, proj, k_rot, vt)


HEADS_PER_TILE = 4
CAT_W = HEADS_PER_TILE * CHUNK
N_TILES = DN_HEADS // HEADS_PER_TILE


def _split3(x):
    p1 = x.astype(BF16)
    r1 = x - p1.astype(F32)
    p2 = r1.astype(BF16)
    p3 = (r1 - p2.astype(F32)).astype(BF16)
    return p1, p2, p3


def _softplus(x):
    return jnp.maximum(x, 0.0) + jnp.log(1.0 + jnp.exp(-jnp.abs(x)))


def _dot(a, b):
    return jnp.dot(a, b, preferred_element_type=F32)


def _dchunk_kernel(q0_ref, q1_ref, k0_ref, k1_ref, v0_ref, v1_ref, c_ref, r_ref, pc_ref, pr_ref,
                   a1_ref, a2_ref, au_ref, ae_ref, *, cpb):
    C, W, T = CHUNK, CAT_W, HEADS_PER_TILE
    rows = cpb * C

    def cum_matrix(n, lower):
        i = lax.broadcasted_iota(jnp.int32, (n, n), 0)
        j = lax.broadcasted_iota(jnp.int32, (n, n), 1)
        same = (i >> 6) == (j >> 6)
        return (same & ((i >= j) if lower else (i <= j))).astype(BF16)

    col_lower, col_upper = cum_matrix(rows, True), cum_matrix(rows, False)
    row_lower, row_upper = cum_matrix(W, True), cum_matrix(W, False)

    blk = c_ref[...]
    beta_all = jax.nn.sigmoid(blk)
    g_parts = _split3(-jnp.exp(pc_ref[0:1, :]) * _softplus(blk + pc_ref[1:2, :]))
    gcol = (sum(_dot(col_lower, p) for p in g_parts),
            sum(_dot(col_upper, p) for p in g_parts))
    blk_r = r_ref[...]
    g_r = (-jnp.exp(pr_ref[0][None]) * _softplus(blk_r + pr_ref[1][None])).reshape(cpb * 8, W)
    gr_parts = _split3(g_r)
    grow = (sum(_dot(p, row_upper) for p in gr_parts),
            sum(_dot(p, row_lower) for p in gr_parts))

    ri = lax.broadcasted_iota(jnp.int32, (C, W), 0)
    li = lax.broadcasted_iota(jnp.int32, (C, W), 1)
    ci = li & (C - 1)
    lb = li >> 6
    incl = (ri >= ci, ri <= ci)
    offdiag = ri != ci
    eye = (ri == ci).astype(F32)
    same = {s: (ri >> lg) == (ci >> lg) for lg, s in ((1, 2), (2, 4), (3, 8), (4, 16), (5, 32))}
    level_masks = [same[4] & ~same[2], same[8] & ~same[4], same[16] & ~same[8],
                   same[32] & ~same[16], ~same[32]]
    tile_sel = [lb == t for t in range(T)]
    head_of_lane = lax.broadcasted_iota(jnp.int32, (C, T * DN_DK), 1) >> 7

    def block_diag(y):
        zero = jnp.zeros_like(y)
        return jnp.concatenate([jnp.where(tile_sel[t], y, zero) for t in range(T)], axis=0)

    def cat_bcast(cols):
        out = jnp.broadcast_to(cols[T - 1], (C, W))
        for t in range(T - 2, -1, -1):
            out = jnp.where(tile_sel[t], jnp.broadcast_to(cols[t], (C, W)), out)
        return out

    pairs = [(c, t) for c in range(cpb) for t in range(N_TILES)]
    units = [(c, t, d) for c in range(cpb) for t in range(N_TILES) for d in range(2)]

    qk, kk, q4s, k4s, v4s = {}, {}, {}, {}, {}
    for c, t in pairs:
        rs = slice(c * C, (c + 1) * C)
        q4 = (q0_ref, q1_ref)[t][rs, :]
        k4 = (k0_ref, k1_ref)[t][rs, :]
        v4 = (v0_ref, v1_ref)[t][rs, :]
        zero = jnp.zeros_like(k4)
        k_bd = jnp.concatenate([jnp.where(head_of_lane == h, k4, zero) for h in range(T)], axis=0)
        r = lax.dot_general(jnp.concatenate([q4, k4], axis=0), k_bd, (((1,), (1,)), ((), ())),
                            preferred_element_type=F32)
        qk[c, t], kk[c, t] = r[:C], r[C:]
        q4s[c, t], k4s[c, t], v4s[c, t] = q4, k4, v4

    lmat, amat, xinv = {}, {}, {}
    for c, t, d in units:
        rs = slice(c * C, (c + 1) * C)
        base = d * DN_HEADS + t * T
        gc = cat_bcast([gcol[d][rs, 2 * DN_HEADS + base + h:2 * DN_HEADS + base + h + 1]
                        for h in range(T)])
        beta = cat_bcast([beta_all[rs, base + h:base + h + 1] for h in range(T)])
        row = c * 8 + d * N_TILES + t
        gr = jnp.broadcast_to(grow[d][row:row + 1, :], (C, W))
        decay = jnp.exp(jnp.where(incl[d], gc - gr, -1e30))
        l = jnp.where(offdiag, beta * kk[c, t] * decay, 0.0)
        lmat[c, t, d] = l
        amat[c, t, d] = qk[c, t] * decay
        xinv[c, t, d] = eye - jnp.where(same[2], l, 0.0)

    for msk in level_masks:
        ys, xbs = {}, {}
        for u in units:
            e = jnp.where(msk, lmat[u], 0.0).astype(BF16)
            xbs[u] = xinv[u].astype(BF16)
            ys[u] = _dot(e, block_diag(xbs[u]))
        for u in units:
            xinv[u] = xinv[u] - _dot(xbs[u], block_diag(ys[u].astype(BF16)))

    for c, t, d in units:
        rs = slice(c * C, (c + 1) * C)
        kd_rows, rhs_rows = [], []
        for h in range(T):
            head = t * T + h
            col = d * DN_HEADS + head
            hs = slice(h * DN_DK, (h + 1) * DN_DK)
            out_s = slice(head * DN_DK, (head + 1) * DN_DK)
            gc = jnp.broadcast_to(gcol[d][rs, 2 * DN_HEADS + col:2 * DN_HEADS + col + 1],
                                  (C, DN_DK))
            beta = jnp.broadcast_to(beta_all[rs, col:col + 1], (C, DN_DK))
            g_last = gc[C - 1:C, :] if d == 0 else gc[0:1, :]
            eg = jnp.exp(gc)
            kf = k4s[c, t][:, hs].astype(F32)
            rhs = jnp.concatenate([kf * (beta * eg), v4s[c, t][:, hs].astype(F32) * beta],
                                  axis=1).astype(BF16)
            zero = jnp.zeros_like(rhs)
            rhs_rows.append(jnp.concatenate([rhs if hh == h else zero for hh in range(T)], axis=1))
            kd_rows.append(kf * jnp.exp(g_last - gc))
            a1_ref[c, d, C:2 * C, out_s] = (q4s[c, t][:, hs].astype(F32) * eg).astype(BF16)
            ae_ref[c, d, :, out_s] = jnp.exp(g_last)
        wu = _dot(xinv[c, t, d].astype(BF16), jnp.concatenate(rhs_rows, axis=0))
        for h in range(T):
            head = t * T + h
            out_s = slice(head * DN_DK, (head + 1) * DN_DK)
            a1_ref[c, d, 0:C, out_s] = wu[:, 2 * h * DN_DK:(2 * h + 1) * DN_DK].astype(BF16)
            au_ref[c, d, :, out_s] = wu[:, (2 * h + 1) * DN_DK:(2 * h + 2) * DN_DK].astype(BF16)
        cat_s = slice(t * W, (t + 1) * W)
        a2_ref[c, d, 0:C, cat_s] = amat[c, t, d].astype(BF16)
        a2_ref[c, d, C:C + DN_DK, cat_s] = jnp.concatenate(kd_rows, axis=0).T.astype(BF16)


def _dchunk(proj, small, small_r, par_col, par_row, offs, batch, seq, cpb=4):
    assert N_TILES == 2
    nc = seq // CHUNK
    steps = nc // cpb
    tw = HEADS_PER_TILE * DN_DK
    kern = functools.partial(_dchunk_kernel, cpb=cpb)
    out5 = lambda b, i: (b, i, 0, 0, 0)
    tile_cols = [offs["dq"], offs["dq"] + tw, offs["dk0"], offs["dk1"], offs["dv"], offs["dv"] + tw]
    qkv_specs = [pl.BlockSpec((cpb * CHUNK, tw), functools.partial(
        lambda b, i, col: (b * steps + i, col), col=c // tw)) for c in tile_cols]
    return pl.pallas_call(
        kern,
        grid=(batch, steps),
        in_specs=qkv_specs + [
            pl.BlockSpec((cpb * CHUNK, LANES), lambda b, i: (b * steps + i, 0)),
            pl.BlockSpec((cpb, 8, CAT_W), lambda b, i: (b * steps + i, 0, 0)),
            pl.BlockSpec((8, LANES), lambda b, i: (0, 0)),
            pl.BlockSpec((2, 8, CAT_W), lambda b, i: (0, 0, 0)),
        ],
        out_specs=[
            pl.BlockSpec((None, cpb, 2, 2 * CHUNK, DN_KW), out5),
            pl.BlockSpec((None, cpb, 2, CHUNK + DN_DK, DN_HEADS * CHUNK), out5),
            pl.BlockSpec((None, cpb, 2, CHUNK, DN_VW), out5),
            pl.BlockSpec((None, cpb, 2, 1, DN_VW), out5),
        ],
        out_shape=[
            jax.ShapeDtypeStruct((batch, nc, 2, 2 * CHUNK, DN_KW), BF16),
            jax.ShapeDtypeStruct((batch, nc, 2, CHUNK + DN_DK, DN_HEADS * CHUNK), BF16),
            jax.ShapeDtypeStruct((batch, nc, 2, CHUNK, DN_VW), BF16),
            jax.ShapeDtypeStruct((batch, nc, 2, 1, DN_VW), F32),
        ],
        compiler_params=pltpu.CompilerParams(dimension_semantics=("parallel", "parallel")),
        name="dchunk",
    )(proj, proj, proj, proj, proj, proj, small, small_r, par_col, par_row)


def _dscan_kernel(a1f_ref, a1b_ref, a2f_ref, a2b_ref, auf_ref, aub_ref, aef_ref, aeb_ref,
                  of_ref, ob_ref, s_ref, *, batch, cps):
    C = CHUNK
    PW = 2 * DN_DV
    n_pairs = DN_HEADS // 2

    @pl.when(pl.program_id(0) == 0)
    def _():
        s_ref[...] = jnp.zeros(s_ref.shape, F32)

    first = lax.broadcasted_iota(jnp.int32, (DN_DK, PW), 1) < DN_DV
    first_c = lax.broadcasted_iota(jnp.int32, (C, PW), 1) < DN_DV
    refs = ((a1f_ref, a2f_ref, auf_ref, aef_ref, of_ref), (a1b_ref, a2b_ref, aub_ref, aeb_ref, ob_ref))
    units = [(b, d, p) for b in range(batch) for d in range(2) for p in range(n_pairs)]
    sidx = lambda u: (u[0] * 2 + u[1]) * n_pairs + u[2]

    states = {u: s_ref[sidx(u)] for u in units}
    for step in range(cps):
        r1, r2 = {}, {}
        cc = (step, cps - 1 - step)
        for u in units:
            b, d, p = u
            r1[u] = _dot(refs[d][0][b, cc[d], :, p * PW:(p + 1) * PW], states[u].astype(BF16))
        for u in units:
            b, d, p = u
            vb = (refs[d][2][b, cc[d], :, p * PW:(p + 1) * PW].astype(F32) - r1[u][:C]).astype(BF16)
            zero = jnp.zeros_like(vb)
            v_bd = jnp.concatenate([jnp.where(first_c, vb, zero), jnp.where(first_c, zero, vb)],
                                   axis=0)
            r2[u] = _dot(refs[d][1][b, cc[d], :, p * 2 * C:(p + 1) * 2 * C], v_bd)
        for u in units:
            b, d, p = u
            refs[d][4][b, cc[d], :, p * PW:(p + 1) * PW] = (r1[u][C:] + r2[u][:C]).astype(BF16)
            e = refs[d][3][b, cc[d], 0:1, p * PW:(p + 1) * PW]
            upd = r2[u][C:]
            s = states[u]
            states[u] = jnp.concatenate(
                [s[:DN_DK] * e + jnp.where(first, upd, 0.0),
                 s[DN_DK:] * e + jnp.where(first, 0.0, upd)], axis=0)
    for u in units:
        s_ref[sidx(u)] = states[u]


def _dscan(a1, a2, au, ae, batch, seq, cps=4):
    nc = seq // CHUNK
    steps = nc // cps
    kern = functools.partial(_dscan_kernel, batch=batch, cps=cps)
    fwd = lambda i: (0, i, 0, 0, 0)
    bwd = lambda i: (0, steps - 1 - i, 1, 0, 0)

    def spec(arr, imap):
        return pl.BlockSpec((batch, cps, None) + arr.shape[3:], imap)

    o_spec = lambda imap: pl.BlockSpec((batch, cps, CHUNK, DN_VW), imap)
    o_shape = jax.ShapeDtypeStruct((batch, nc, CHUNK, DN_VW), BF16)
    return pl.pallas_call(
        kern,
        grid=(steps,),
        in_specs=[spec(a1, fwd), spec(a1, bwd), spec(a2, fwd), spec(a2, bwd),
                  spec(au, fwd), spec(au, bwd), spec(ae, fwd), spec(ae, bwd)],
        out_specs=[o_spec(lambda i: (0, i, 0, 0)), o_spec(lambda i: (0, steps - 1 - i, 0, 0))],
        out_shape=[o_shape, o_shape],
        scratch_shapes=[pltpu.VMEM((batch * 2 * (DN_HEADS // 2), 2 * DN_DK, 2 * DN_DV), F32)],
        compiler_params=pltpu.CompilerParams(dimension_semantics=("arbitrary",)),
        name="dscan",
    )(a1, a1, a2, a2, au, au, ae, ae)


def _post_kernel(oa_ref, of_ref, ob_ref, dz_ref, ga_ref, gd_ref, x_ref, p_ref,
                 wa_ref, wd_ref, wo_ref, wg_ref, wp_ref, nd_ref, npost_ref, nple_ref, o_ref):
    def rms(t, w):
        return t * lax.rsqrt(jnp.mean(t * t, axis=-1, keepdims=True) + EPS) * w

    nd = nd_ref[...]
    parts = []
    for h in range(DN_HEADS):
        sl = slice(h * DN_DV, (h + 1) * DN_DV)
        od = of_ref[:, sl].astype(F32) + ob_ref[:, sl].astype(F32)
        z = dz_ref[:, sl].astype(F32)
        parts.append((rms(od, nd) * (z * jax.nn.sigmoid(z))).astype(BF16))
    y_dn = jnp.dot(jnp.concatenate(parts, axis=1), wd_ref[...], preferred_element_type=F32)
    y_att = jnp.dot(oa_ref[...], wa_ref[...], preferred_element_type=F32)
    merged = (jax.nn.sigmoid(ga_ref[...].astype(F32)) * y_att
              + jax.nn.sigmoid(gd_ref[...].astype(F32)) * y_dn)
    mix = jnp.dot(merged.astype(BF16), wo_ref[...], preferred_element_type=F32)
    x1 = x_ref[...] + rms(mix, npost_ref[...])
    gate = jax.nn.sigmoid(jnp.dot(x1.astype(BF16), wg_ref[...], preferred_element_type=F32))
    e = jnp.dot(p_ref[...].astype(BF16), wp_ref[...], preferred_element_type=F32)
    o_ref[...] = x1 + rms(gate * e, nple_ref[...])


def _post(o_att, o_f, o_b, proj, x2, p2, wa, wd, wo, wg, wp, nd, npost, nple, offs, tm=512):
    n, d = x2.shape
    ple = p2.shape[1]
    row = lambda i: (i, 0)
    const = lambda i: (0, 0)
    return pl.pallas_call(
        _post_kernel,
        grid=(n // tm,),
        in_specs=[
            pl.BlockSpec((tm, ATT_W), row),
            pl.BlockSpec((tm, DN_VW), row),
            pl.BlockSpec((tm, DN_VW), row),
            pl.BlockSpec((tm, DN_VW), lambda i: (i, offs["dz"] // DN_VW)),
            pl.BlockSpec((tm, d), lambda i: (i, offs["ga"] // d)),
            pl.BlockSpec((tm, d), lambda i: (i, offs["gd"] // d)),
            pl.BlockSpec((tm, d), row),
            pl.BlockSpec((tm, ple), row),
            pl.BlockSpec((ATT_W, d), const),
            pl.BlockSpec((DN_VW, d), const),
            pl.BlockSpec((d, d), const),
            pl.BlockSpec((d, d), const),
            pl.BlockSpec((ple, d), const),
            pl.BlockSpec((1, DN_DV), const),
            pl.BlockSpec((1, d), const),
            pl.BlockSpec((1, d), const),
        ],
        out_specs=pl.BlockSpec((tm, d), row),
        out_shape=jax.ShapeDtypeStruct((n, d), F32),
        compiler_params=pltpu.CompilerParams(dimension_semantics=("parallel",)),
        name="post",
    )(o_att, o_f, o_b, proj, proj, proj, x2, p2, wa, wd, wo, wg, wp, nd, npost, nple)


def _rope_tables(seq):
    rows = seq // GRID_W
    row = np.broadcast_to(np.arange(rows)[:, None], (rows, GRID_W)).reshape(seq)
    col = np.broadcast_to(np.arange(GRID_W)[None, :], (rows, GRID_W)).reshape(seq)
    n_freq = HEAD_DIM // 4
    inv_freq = np.float32(ROPE_THETA) ** (-np.arange(n_freq, dtype=np.float32) / np.float32(n_freq))
    ang = np.concatenate([row.astype(np.float32)[:, None] * inv_freq,
                          col.astype(np.float32)[:, None] * inv_freq], axis=-1)
    cos, sin = np.cos(ang), np.sin(ang)
    return (jnp.asarray(np.concatenate([cos, cos], axis=-1), F32),
            jnp.asarray(np.concatenate([-sin, sin], axis=-1), F32))


def kernel(x, p, norm_pre, w_in, q_norm, k_norm, conv_w, a_log, dt_bias, dn_norm, w_br_att,
           w_br_dn, w_out, norm_post, w_ple_proj, w_ple_gate, ple_norm):
    batch, seq, d = x.shape
    depth = w_in.shape[0]
    n = batch * seq
    assert seq % GRID_W == 0 and seq % CHUNK == 0 and d % LANES == 0
    cos, sin = _rope_tables(seq)
    x2 = x.reshape(n, d)
    for i in range(depth):
        w_main, w_small, offs, pieces = _regroup_columns(w_in[i])
        conv_w8 = jnp.pad(conv_w[i], ((0, 8 - CONV_K), (0, 0)))
        proj, small = _proj(x2, norm_pre[i][None, :], w_main, w_small, conv_w8, pieces, seq)

        tq, tk = min(ATT_TQ, seq), min(ATT_TK, seq // 2)
        qt, k_rot, vt = _aprep(proj, cos, sin, _deinterleave(q_norm[i])[None, :],
                               _deinterleave(k_norm[i])[None, :], offs, batch, seq, tq, tk)
        o_att = _attention(proj, qt, k_rot, vt, offs, batch, seq, tq, tk)

        nsm = 4 * DN_HEADS
        nck = n // CHUNK
        small_r = small[:, 2 * DN_HEADS:nsm].reshape(nck, CHUNK, 2, N_TILES, HEADS_PER_TILE)
        small_r = small_r.transpose(0, 2, 3, 4, 1).reshape(nck, 2 * N_TILES, CAT_W)
        small_r = jnp.pad(small_r, ((0, 0), (0, 8 - 2 * N_TILES), (0, 0)))
        par = jnp.stack([a_log[i].reshape(-1), dt_bias[i].reshape(-1)])
        par_col = jnp.pad(jnp.concatenate([jnp.zeros_like(par), par], axis=1),
                          ((0, 6), (0, LANES - nsm)))
        par_row = jnp.repeat(par.reshape(2, 2 * N_TILES, HEADS_PER_TILE), CHUNK, axis=2)
        par_row = jnp.pad(par_row, ((0, 0), (0, 8 - 2 * N_TILES), (0, 0)))
        a1, a2, au, ae = _dchunk(proj, small, small_r, par_col, par_row, offs, batch, seq)
        o_f, o_b = _dscan(a1, a2, au, ae, batch, seq)
        o_f, o_b = o_f.reshape(n, DN_VW), o_b.reshape(n, DN_VW)

        x2 = _post(o_att, o_f, o_b, proj, x2, p[i].reshape(n, -1),
                   w_br_att[i].astype(BF16), w_br_dn[i].astype(BF16), w_out[i].astype(BF16),
                   w_ple_gate[i].astype(BF16), w_ple_proj[i].astype(BF16),
                   dn_norm[i][None, :], norm_post[i][None, :], ple_norm[i][None, :], offs)
    return x2.reshape(batch, seq, d)
```

```python
import functools

import numpy as np
import jax
import jax.numpy as jnp
from jax import lax
from jax.experimental import pallas as pl
from jax.experimental.pallas import tpu as pltpu

F32 = jnp.float32
BF16 = jnp.bfloat16

GRID_W = 64
ATT_HEADS = 8
ATT_KV_HEADS = 2
HEAD_DIM = 128
ROPE_THETA = 10000.0
DN_HEADS = 8
DN_DK = 128
DN_DV = 128
CONV_K = 5
CHUNK = 64
EPS = 1e-6

ATT_W = ATT_HEADS * HEAD_DIM
KV_W = ATT_KV_HEADS * HEAD_DIM
DN_KW = DN_HEADS * DN_DK
DN_VW = DN_HEADS * DN_DV
GROUP = ATT_HEADS // ATT_KV_HEADS
LANES = 128


def _deinterleave(t):
    lead = t.shape[:-1]
    t = t.reshape(lead + (-1, HEAD_DIM // 2, 2))
    return jnp.swapaxes(t, -1, -2).reshape(lead + (-1,))


PROJ_TN = 4352
CONV_HALO = 16
CONV_PIECE = 512


def _regroup_columns(w):
    d_model = w.shape[0]
    sizes = (ATT_W, KV_W, KV_W, ATT_W, DN_KW, DN_KW, DN_VW, 2 * DN_HEADS, 2 * DN_HEADS, DN_VW,
             d_model, d_model)
    assert w.shape[1] == sum(sizes)
    aq, ak, av, az, dq, dk, dv, db, da, dz, ga, gd = jnp.split(w, np.cumsum(sizes)[:-1], axis=1)
    aq0, aq1 = jnp.split(_deinterleave(aq), 2, axis=1)
    az0, az1 = jnp.split(az, 2, axis=1)
    dk0, dk1 = jnp.split(dk, 2, axis=1)
    groups = [("ga", ga, "plain", 0), ("gd", gd, "plain", 0), ("dq", dq, "conv_q", 0),
              ("dk0", dk0, "conv_k", DN_KW), ("aq0", aq0, "plain", 0),
              ("ak", _deinterleave(ak), "plain", 0),
              ("av", av, "plain", 0), ("dk1", dk1, "conv_k", DN_KW + DN_KW // 2),
              ("dz", dz, "plain", 0), ("dv", dv, "conv_v", 2 * DN_KW), ("aq1", aq1, "plain", 0),
              ("az0", az0, "plain", 0), ("az1", az1, "plain", 0)]
    offsets, off = {}, 0
    pieces = [[] for _ in range(sum(g.shape[1] for _, g, _, _ in groups) // PROJ_TN)]
    for name, g, kind, ch in groups:
        offsets[name] = off
        width = g.shape[1]
        tile, local = divmod(off, PROJ_TN)
        assert local + width <= PROJ_TN
        if kind == "plain":
            last = pieces[tile][-1] if pieces[tile] else None
            if last is not None and last[2] == "plain" and last[0] + last[1] == local:
                pieces[tile][-1] = (last[0], last[1] + width, "plain", 0)
            else:
                pieces[tile].append((local, width, "plain", 0))
        else:
            for k in range(0, width, CONV_PIECE):
                pieces[tile].append((local + k, CONV_PIECE, kind, ch + k))
        off += width
    main = jnp.concatenate([g.astype(BF16) for _, g, _, _ in groups], axis=1)
    small = jnp.pad(jnp.concatenate([db, da], axis=1), ((0, 0), (0, LANES - 4 * DN_HEADS)))
    return main, small.astype(BF16), offsets, pieces


def _proj_kernel(x_ref, xp_ref, xn_ref, g_ref, w_ref, ws_ref, cw_ref, o_ref, os_ref, h_ref,
                 *, tiles_per_seq, pieces):
    i, j = pl.program_id(0), pl.program_id(1)
    tm = x_ref.shape[0]
    ext = tm + 2 * CONV_HALO
    half = CONV_K // 2

    def normed(x):
        r = lax.rsqrt(jnp.mean(x * x, axis=-1, keepdims=True) + EPS)
        return (x * r * g_ref[...]).astype(BF16)

    @pl.when(j == 0)
    def _():
        t = i % tiles_per_seq
        h = normed(x_ref[...])
        zero = jnp.zeros((CONV_HALO, x_ref.shape[1]), BF16)
        h_ref[0:CONV_HALO, :] = jnp.where(t == 0, zero, normed(xp_ref[...]))
        h_ref[CONV_HALO:CONV_HALO + tm, :] = h
        h_ref[CONV_HALO + tm:, :] = jnp.where(t == tiles_per_seq - 1, zero, normed(xn_ref[...]))
        os_ref[...] = jnp.dot(h, ws_ref[...], preferred_element_type=F32)

    def tile(tile_pieces):
        for start, width, kind, ch in tile_pieces:
            cols = slice(start, start + width)
            if kind == "plain":
                o_ref[:, cols] = jnp.dot(h_ref[CONV_HALO:CONV_HALO + tm, :], w_ref[:, cols],
                                         preferred_element_type=F32).astype(BF16)
                continue
            r = jnp.dot(h_ref[...], w_ref[:, cols], preferred_element_type=F32)
            taps = cw_ref[:, ch:ch + width]
            acc = r[CONV_HALO:CONV_HALO + tm] * taps[half:half + 1]
            for t in range(CONV_K):
                if t != half:
                    shifted = pltpu.roll(r, (half - t) % ext, axis=0)
                    acc = acc + shifted[CONV_HALO:CONV_HALO + tm] * taps[t:t + 1]
            y = acc * jax.nn.sigmoid(acc)
            for hh in range(width // DN_DK):
                yh = y[:, hh * DN_DK:(hh + 1) * DN_DK]
                if kind != "conv_v":
                    inv = lax.rsqrt(jnp.sum(yh * yh, axis=-1, keepdims=True) + EPS)
                    yh = yh * (inv * DN_DK ** -0.5 if kind == "conv_q" else inv)
                o_ref[:, start + hh * DN_DK:start + (hh + 1) * DN_DK] = yh.astype(BF16)

    for jj, tile_pieces in enumerate(pieces):
        pl.when(j == jj)(functools.partial(tile, tile_pieces))


def _proj(x2, g, w_main, w_small, conv_w8, pieces, seq, tm=1024):
    n, d = x2.shape
    main_w = w_main.shape[1]
    tm = min(tm, seq)
    hb = tm // CONV_HALO
    last_hb = n // CONV_HALO - 1
    kern = functools.partial(_proj_kernel, tiles_per_seq=seq // tm, pieces=pieces)
    return pl.pallas_call(
        kern,
        grid=(n // tm, main_w // PROJ_TN),
        in_specs=[
            pl.BlockSpec((tm, d), lambda i, j: (i, 0)),
            pl.BlockSpec((CONV_HALO, d), lambda i, j: (jnp.maximum(i * hb - 1, 0), 0)),
            pl.BlockSpec((CONV_HALO, d), lambda i, j: (jnp.minimum((i + 1) * hb, last_hb), 0)),
            pl.BlockSpec((1, d), lambda i, j: (0, 0)),
            pl.BlockSpec((d, PROJ_TN), lambda i, j: (0, j)),
            pl.BlockSpec((d, LANES), lambda i, j: (0, 0)),
            pl.BlockSpec(conv_w8.shape, lambda i, j: (0, 0)),
        ],
        out_specs=[
            pl.BlockSpec((tm, PROJ_TN), lambda i, j: (i, j)),
            pl.BlockSpec((tm, LANES), lambda i, j: (i, 0)),
        ],
        out_shape=[
            jax.ShapeDtypeStruct((n, main_w), BF16),
            jax.ShapeDtypeStruct((n, LANES), F32),
        ],
        scratch_shapes=[pltpu.VMEM((tm + 2 * CONV_HALO, d), BF16)],
        compiler_params=pltpu.CompilerParams(dimension_semantics=("parallel", "arbitrary")),
        name="proj",
    )(x2, x2, x2, g, w_main, w_small, conv_w8)


def _norm_rope(xh, w, cos, sin):
    r = lax.rsqrt(jnp.mean(xh * xh, axis=-1, keepdims=True) + EPS)
    xn = xh * r * w
    return xn * cos + pltpu.roll(xn, HEAD_DIM // 2, axis=1) * sin


ATT_TQ = 1024
ATT_TK = 512
VT_ROWS = HEAD_DIM + 16


def _aprep_kernel(q0_ref, q1_ref, k_ref, v_ref, cos_ref, sin_ref, wq_ref, wk_ref,
                  qt_ref, kr_ref, vt_ref):
    cos, sin = cos_ref[...], sin_ref[...]
    scale = HEAD_DIM ** -0.5
    tm = k_ref.shape[0]
    tk = vt_ref.shape[-1]
    for h, q_ref in enumerate((q0_ref, q1_ref)):
        sl = slice(h * HEAD_DIM, (h + 1) * HEAD_DIM)
        kr_ref[:, sl] = _norm_rope(k_ref[:, sl].astype(F32), wk_ref[...], cos, sin).astype(BF16)
        for c in range(tm // tk):
            vt_ref[h, c, 0:HEAD_DIM, :] = v_ref[c * tk:(c + 1) * tk, sl].astype(F32).T.astype(BF16)
            vt_ref[h, c, HEAD_DIM:, :] = jnp.ones((VT_ROWS - HEAD_DIM, tk), BF16)
        for g in range(GROUP):
            qs = slice(g * HEAD_DIM, (g + 1) * HEAD_DIM)
            qr = _norm_rope(q_ref[:, qs].astype(F32), wq_ref[...], cos, sin) * scale
            qt_ref[h, :, g * tm:(g + 1) * tm] = qr.T.astype(BF16)


def _aprep(proj, cos, sin, wq, wk, offs, batch, seq, tm, tk):
    assert ATT_KV_HEADS == 2
    n = proj.shape[0]
    t = seq // tm
    cpt = tm // tk
    gw = GROUP * HEAD_DIM
    return pl.pallas_call(
        _aprep_kernel,
        grid=(batch, t),
        in_specs=[
            pl.BlockSpec((tm, gw), lambda b, i: (b * t + i, offs["aq0"] // gw)),
            pl.BlockSpec((tm, gw), lambda b, i: (b * t + i, offs["aq1"] // gw)),
            pl.BlockSpec((tm, KV_W), lambda b, i: (b * t + i, offs["ak"] // KV_W)),
            pl.BlockSpec((tm, KV_W), lambda b, i: (b * t + i, offs["av"] // KV_W)),
            pl.BlockSpec((tm, HEAD_DIM), lambda b, i: (i, 0)),
            pl.BlockSpec((tm, HEAD_DIM), lambda b, i: (i, 0)),
            pl.BlockSpec((1, HEAD_DIM), lambda b, i: (0, 0)),
            pl.BlockSpec((1, HEAD_DIM), lambda b, i: (0, 0)),
        ],
        out_specs=[
            pl.BlockSpec((None, None, ATT_KV_HEADS, HEAD_DIM, GROUP * tm),
                         lambda b, i: (b, i, 0, 0, 0)),
            pl.BlockSpec((tm, KV_W), lambda b, i: (b * t + i, 0)),
            pl.BlockSpec((None, ATT_KV_HEADS, cpt, VT_ROWS, tk), lambda b, i: (b, 0, i, 0, 0)),
        ],
        out_shape=[
            jax.ShapeDtypeStruct((batch, t, ATT_KV_HEADS, HEAD_DIM, GROUP * tm), BF16),
            jax.ShapeDtypeStruct((n, KV_W), BF16),
            jax.ShapeDtypeStruct((batch, ATT_KV_HEADS, seq // tk, VT_ROWS, tk), BF16),
        ],
        compiler_params=pltpu.CompilerParams(dimension_semantics=("parallel", "parallel")),
        name="aprep",
    )(proj, proj, proj, proj, cos, sin, wq, wk)


def _attn_kernel(qt_ref, z_ref, k_ref, vt_ref, o_ref, acc_ref, s0_ref, s1_ref, p0_ref, p1_ref,
                 *, tq, tk, seq):
    cols = GROUP * tq
    n_chunks = seq // tk
    qt = qt_ref[...]

    def scores(c, s_ref):
        start = pl.multiple_of(c * tk, tk)
        s = jnp.dot(k_ref[pl.ds(start, tk), :], qt, preferred_element_type=F32)
        s_ref[...] = s
        return jnp.max(s, axis=0, keepdims=True)

    def softmax(s_ref, p_ref, m, smax):
        m_new = jnp.maximum(m, smax)
        p_ref[...] = jnp.exp((s_ref[...] - m_new).astype(BF16))
        return m_new, jnp.exp(m - m_new)

    def accumulate(c, p_ref, alpha, init=False):
        pv = jnp.dot(vt_ref[c], p_ref[...], preferred_element_type=F32)
        acc_ref[...] = pv if init else alpha * acc_ref[...] + pv

    def pair(c, carry, first, last):
        m, alpha_prev, smax0 = carry
        m, alpha0 = softmax(s0_ref, p0_ref, m, smax0)
        smax1 = scores(c + 1, s1_ref)
        if not first:
            accumulate(c - 1, p1_ref, alpha_prev)
        m, alpha1 = softmax(s1_ref, p1_ref, m, smax1)
        if not last:
            smax0 = scores(c + 2, s0_ref)
        accumulate(c, p0_ref, alpha0, init=first)
        return m, alpha1, smax0

    n_pairs = n_chunks // 2
    smax = scores(0, s0_ref)
    carry = (jnp.full((1, cols), -jnp.inf, F32), jnp.ones((1, cols), F32), smax)
    carry = pair(0, carry, True, n_pairs == 1)
    if n_pairs > 1:
        carry = lax.fori_loop(1, n_pairs - 1, lambda j, cr: pair(2 * j, cr, False, False), carry)
        carry = pair(n_chunks - 2, carry, False, True)
    accumulate(n_chunks - 1, p1_ref, carry[1])
    acc = acc_ref[...]
    o_t = acc[:HEAD_DIM] / acc[HEAD_DIM:HEAD_DIM + 1]
    for g in range(GROUP):
        sl = slice(g * HEAD_DIM, (g + 1) * HEAD_DIM)
        z = z_ref[:, sl].astype(F32)
        o_ref[:, sl] = (o_t[:, g * tq:(g + 1) * tq].T * (z * jax.nn.sigmoid(z))).astype(BF16)


def _attention(proj, qt, k_rot, vt, offs, batch, seq, tq, tk):
    n = proj.shape[0]
    gw = GROUP * HEAD_DIM
    t = seq // tq
    cols = GROUP * tq
    assert (seq // tk) % 2 == 0 and offs["az1"] == offs["az0"] + gw
    kern = functools.partial(_attn_kernel, tq=tq, tk=tk, seq=seq)
    return pl.pallas_call(
        kern,
        grid=(batch, ATT_KV_HEADS, t),
        in_specs=[
            pl.BlockSpec((None, None, None, HEAD_DIM, cols), lambda b, h, i: (b, i, h, 0, 0)),
            pl.BlockSpec((tq, gw), lambda b, h, i: (b * t + i, offs["az0"] // gw + h)),
            pl.BlockSpec((seq, HEAD_DIM), lambda b, h, i: (b, h)),
            pl.BlockSpec((None, None, seq // tk, VT_ROWS, tk), lambda b, h, i: (b, h, 0, 0, 0)),
        ],
        out_specs=pl.BlockSpec((tq, gw), lambda b, h, i: (b * t + i, h)),
        out_shape=jax.ShapeDtypeStruct((n, ATT_W), BF16),
        scratch_shapes=[pltpu.VMEM((VT_ROWS, cols), F32),
                        pltpu.VMEM((tk, cols), F32), pltpu.VMEM((tk, cols), F32),
                        pltpu.VMEM((tk, cols), BF16), pltpu.VMEM((tk, cols), BF16)],
        compiler_params=pltpu.CompilerParams(
            dimension_semantics=("parallel", "parallel", "arbitrary")),
        name="attn",
    )(qt, proj, k_rot, vt)


HEADS_PER_TILE = 4
CHUNK_SHIFT = CHUNK.bit_length() - 1
assert CHUNK == 1 << CHUNK_SHIFT and DN_DK & (DN_DK - 1) == 0
CAT_W = HEADS_PER_TILE * CHUNK
N_TILES = DN_HEADS // HEADS_PER_TILE


def _split3(x):
    p1 = x.astype(BF16)
    r1 = x - p1.astype(F32)
    p2 = r1.astype(BF16)
    p3 = (r1 - p2.astype(F32)).astype(BF16)
    return p1, p2, p3


def _softplus(x):
    return jnp.maximum(x, 0.0) + jnp.log(1.0 + jnp.exp(-jnp.abs(x)))


def _dot(a, b):
    return jnp.dot(a, b, preferred_element_type=F32)


def _dchunk_kernel(q0_ref, q1_ref, k0_ref, k1_ref, v0_ref, v1_ref, c_ref, r_ref, pc_ref, pr_ref,
                   a1_ref, a2_ref, au_ref, ae_ref, *, cpb):
    C, W, T = CHUNK, CAT_W, HEADS_PER_TILE
    rows = cpb * C

    def cum_matrix(n, lower):
        i = lax.broadcasted_iota(jnp.int32, (n, n), 0)
        j = lax.broadcasted_iota(jnp.int32, (n, n), 1)
        same = (i >> CHUNK_SHIFT) == (j >> CHUNK_SHIFT)
        return (same & ((i >= j) if lower else (i <= j))).astype(BF16)

    col_lower, col_upper = cum_matrix(rows, True), cum_matrix(rows, False)
    row_lower, row_upper = cum_matrix(W, True), cum_matrix(W, False)

    blk = c_ref[...]
    beta_all = jax.nn.sigmoid(blk)
    g_parts = _split3(-jnp.exp(pc_ref[0:1, :]) * _softplus(blk + pc_ref[1:2, :]))
    gcol = (sum(_dot(col_lower, p) for p in g_parts),
            sum(_dot(col_upper, p) for p in g_parts))
    blk_r = r_ref[...]
    g_r = (-jnp.exp(pr_ref[0][None]) * _softplus(blk_r + pr_ref[1][None])).reshape(cpb * 8, W)
    gr_parts = _split3(g_r)
    grow = (sum(_dot(p, row_upper) for p in gr_parts),
            sum(_dot(p, row_lower) for p in gr_parts))

    ri = lax.broadcasted_iota(jnp.int32, (C, W), 0)
    li = lax.broadcasted_iota(jnp.int32, (C, W), 1)
    ci = li & (C - 1)
    lb = li >> CHUNK_SHIFT
    incl = (ri >= ci, ri <= ci)
    offdiag = ri != ci
    eye = (ri == ci).astype(F32)
    same = {1 << lg: (ri >> lg) == (ci >> lg) for lg in range(1, CHUNK_SHIFT)}
    level_masks = [same[2 * s] & ~same[s] for s in sorted(same)[:-1]] + [~same[C // 2]]
    tile_sel = [lb == t for t in range(T)]
    head_of_lane = lax.broadcasted_iota(jnp.int32, (C, T * DN_DK), 1) >> (DN_DK.bit_length() - 1)

    def block_diag(y):
        zero = jnp.zeros_like(y)
        return jnp.concatenate([jnp.where(tile_sel[t], y, zero) for t in range(T)], axis=0)

    def cat_bcast(cols):
        out = jnp.broadcast_to(cols[T - 1], (C, W))
        for t in range(T - 2, -1, -1):
            out = jnp.where(tile_sel[t], jnp.broadcast_to(cols[t], (C, W)), out)
        return out

    pairs = [(c, t) for c in range(cpb) for t in range(N_TILES)]
    units = [(c, t, d) for c in range(cpb) for t in range(N_TILES) for d in range(2)]

    qk, kk, q4s, k4s, v4s = {}, {}, {}, {}, {}
    for c, t in pairs:
        rs = slice(c * C, (c + 1) * C)
        q4 = (q0_ref, q1_ref)[t][rs, :]
        k4 = (k0_ref, k1_ref)[t][rs, :]
        v4 = (v0_ref, v1_ref)[t][rs, :]
        zero = jnp.zeros_like(k4)
        k_bd = jnp.concatenate([jnp.where(head_of_lane == h, k4, zero) for h in range(T)], axis=0)
        r = lax.dot_general(jnp.concatenate([q4, k4], axis=0), k_bd, (((1,), (1,)), ((), ())),
                            preferred_element_type=F32)
        qk[c, t], kk[c, t] = r[:C], r[C:]
        q4s[c, t], k4s[c, t], v4s[c, t] = q4, k4, v4

    lmat, amat, xinv = {}, {}, {}
    for c, t, d in units:
        rs = slice(c * C, (c + 1) * C)
        base = d * DN_HEADS + t * T
        gc = cat_bcast([gcol[d][rs, 2 * DN_HEADS + base + h:2 * DN_HEADS + base + h + 1]
                        for h in range(T)])
        beta = cat_bcast([beta_all[rs, base + h:base + h + 1] for h in range(T)])
        row = c * 8 + d * N_TILES + t
        gr = jnp.broadcast_to(grow[d][row:row + 1, :], (C, W))
        decay = jnp.exp(jnp.where(incl[d], gc - gr, -1e30))
        l = jnp.where(offdiag, beta * kk[c, t] * decay, 0.0)
        lmat[c, t, d] = l
        amat[c, t, d] = qk[c, t] * decay
        xinv[c, t, d] = eye - jnp.where(same[2], l, 0.0)

    for msk in level_masks:
        ys, xbs = {}, {}
        for u in units:
            e = jnp.where(msk, lmat[u], 0.0).astype(BF16)
            xbs[u] = xinv[u].astype(BF16)
            ys[u] = _dot(e, block_diag(xbs[u]))
        for u in units:
            xinv[u] = xinv[u] - _dot(xbs[u], block_diag(ys[u].astype(BF16)))

    for c, t, d in units:
        rs = slice(c * C, (c + 1) * C)
        kd_rows, rhs_rows = [], []
        for h in range(T):
            head = t * T + h
            col = d * DN_HEADS + head
            hs = slice(h * DN_DK, (h + 1) * DN_DK)
            out_s = slice(head * DN_DK, (head + 1) * DN_DK)
            gc = jnp.broadcast_to(gcol[d][rs, 2 * DN_HEADS + col:2 * DN_HEADS + col + 1],
                                  (C, DN_DK))
            beta = jnp.broadcast_to(beta_all[rs, col:col + 1], (C, DN_DK))
            g_last = gc[C - 1:C, :] if d == 0 else gc[0:1, :]
            eg = jnp.exp(gc)
            kf = k4s[c, t][:, hs].astype(F32)
            rhs = jnp.concatenate([kf * (beta * eg), v4s[c, t][:, hs].astype(F32) * beta],
                                  axis=1).astype(BF16)
            zero = jnp.zeros_like(rhs)
            rhs_rows.append(jnp.concatenate([rhs if hh == h else zero for hh in range(T)], axis=1))
            kd_rows.append(kf * jnp.exp(g_last - gc))
            a1_ref[c, d, C:2 * C, out_s] = (q4s[c, t][:, hs].astype(F32) * eg).astype(BF16)
            ae_ref[c, d, :, out_s] = jnp.exp(g_last)
        wu = _dot(xinv[c, t, d].astype(BF16), jnp.concatenate(rhs_rows, axis=0))
        for h in range(T):
            head = t * T + h
            out_s = slice(head * DN_DK, (head + 1) * DN_DK)
            a1_ref[c, d, 0:C, out_s] = wu[:, 2 * h * DN_DK:(2 * h + 1) * DN_DK].astype(BF16)
            au_ref[c, d, :, out_s] = wu[:, (2 * h + 1) * DN_DK:(2 * h + 2) * DN_DK].astype(BF16)
        cat_s = slice(t * W, (t + 1) * W)
        a2_ref[c, d, 0:C, cat_s] = amat[c, t, d].astype(BF16)
        a2_ref[c, d, C:C + DN_DK, cat_s] = jnp.concatenate(kd_rows, axis=0).T.astype(BF16)


def _dchunk(proj, small, small_r, par_col, par_row, offs, batch, seq, cpb=4):
    assert N_TILES == 2
    nc = seq // CHUNK
    steps = nc // cpb
    tw = HEADS_PER_TILE * DN_DK
    kern = functools.partial(_dchunk_kernel, cpb=cpb)
    out5 = lambda b, i: (b, i, 0, 0, 0)
    tile_cols = [offs["dq"], offs["dq"] + tw, offs["dk0"], offs["dk1"], offs["dv"], offs["dv"] + tw]
    qkv_specs = [pl.BlockSpec((cpb * CHUNK, tw), functools.partial(
        lambda b, i, col: (b * steps + i, col), col=c // tw)) for c in tile_cols]
    return pl.pallas_call(
        kern,
        grid=(batch, steps),
        in_specs=qkv_specs + [
            pl.BlockSpec((cpb * CHUNK, LANES), lambda b, i: (b * steps + i, 0)),
            pl.BlockSpec((cpb, 8, CAT_W), lambda b, i: (b * steps + i, 0, 0)),
            pl.BlockSpec((8, LANES), lambda b, i: (0, 0)),
            pl.BlockSpec((2, 8, CAT_W), lambda b, i: (0, 0, 0)),
        ],
        out_specs=[
            pl.BlockSpec((None, cpb, 2, 2 * CHUNK, DN_KW), out5),
            pl.BlockSpec((None, cpb, 2, CHUNK + DN_DK, DN_HEADS * CHUNK), out5),
            pl.BlockSpec((None, cpb, 2, CHUNK, DN_VW), out5),
            pl.BlockSpec((None, cpb, 2, 1, DN_VW), out5),
        ],
        out_shape=[
            jax.ShapeDtypeStruct((batch, nc, 2, 2 * CHUNK, DN_KW), BF16),
            jax.ShapeDtypeStruct((batch, nc, 2, CHUNK + DN_DK, DN_HEADS * CHUNK), BF16),
            jax.ShapeDtypeStruct((batch, nc, 2, CHUNK, DN_VW), BF16),
            jax.ShapeDtypeStruct((batch, nc, 2, 1, DN_VW), F32),
        ],
        compiler_params=pltpu.CompilerParams(dimension_semantics=("parallel", "parallel")),
        name="dchunk",
    )(proj, proj, proj, proj, proj, proj, small, small_r, par_col, par_row)


def _dscan_kernel(a1f_ref, a1b_ref, a2f_ref, a2b_ref, auf_ref, aub_ref, aef_ref, aeb_ref,
                  of_ref, ob_ref, s_ref, *, batch, cps):
    C = CHUNK
    PW = 2 * DN_DV
    n_pairs = DN_HEADS // 2

    @pl.when(pl.program_id(0) == 0)
    def _():
        s_ref[...] = jnp.zeros(s_ref.shape, F32)

    first = lax.broadcasted_iota(jnp.int32, (DN_DK, PW), 1) < DN_DV
    first_c = lax.broadcasted_iota(jnp.int32, (C, PW), 1) < DN_DV
    refs = ((a1f_ref, a2f_ref, auf_ref, aef_ref, of_ref), (a1b_ref, a2b_ref, aub_ref, aeb_ref, ob_ref))
    units = [(b, d, p) for b in range(batch) for d in range(2) for p in range(n_pairs)]
    sidx = lambda u: (u[0] * 2 + u[1]) * n_pairs + u[2]

    states = {u: s_ref[sidx(u)] for u in units}
    for step in range(cps):
        r1, r2 = {}, {}
        cc = (step, cps - 1 - step)
        for u in units:
            b, d, p = u
            r1[u] = _dot(refs[d][0][b, cc[d], :, p * PW:(p + 1) * PW], states[u].astype(BF16))
        for u in units:
            b, d, p = u
            vb = (refs[d][2][b, cc[d], :, p * PW:(p + 1) * PW].astype(F32) - r1[u][:C]).astype(BF16)
            zero = jnp.zeros_like(vb)
            v_bd = jnp.concatenate([jnp.where(first_c, vb, zero), jnp.where(first_c, zero, vb)],
                                   axis=0)
            r2[u] = _dot(refs[d][1][b, cc[d], :, p * 2 * C:(p + 1) * 2 * C], v_bd)
        for u in units:
            b, d, p = u
            refs[d][4][b, cc[d], :, p * PW:(p + 1) * PW] = (r1[u][C:] + r2[u][:C]).astype(BF16)
            e = refs[d][3][b, cc[d], 0:1, p * PW:(p + 1) * PW]
            upd = r2[u][C:]
            s = states[u]
            states[u] = jnp.concatenate(
                [s[:DN_DK] * e + jnp.where(first, upd, 0.0),
                 s[DN_DK:] * e + jnp.where(first, 0.0, upd)], axis=0)
    for u in units:
        s_ref[sidx(u)] = states[u]


def _dscan(a1, a2, au, ae, batch, seq, cps=4):
    nc = seq // CHUNK
    steps = nc // cps
    kern = functools.partial(_dscan_kernel, batch=batch, cps=cps)
    fwd = lambda i: (0, i, 0, 0, 0)
    bwd = lambda i: (0, steps - 1 - i, 1, 0, 0)

    def spec(arr, imap):
        return pl.BlockSpec((batch, cps, None) + arr.shape[3:], imap)

    o_spec = lambda imap: pl.BlockSpec((batch, cps, CHUNK, DN_VW), imap)
    o_shape = jax.ShapeDtypeStruct((batch, nc, CHUNK, DN_VW), BF16)
    return pl.pallas_call(
        kern,
        grid=(steps,),
        in_specs=[spec(a1, fwd), spec(a1, bwd), spec(a2, fwd), spec(a2, bwd),
                  spec(au, fwd), spec(au, bwd), spec(ae, fwd), spec(ae, bwd)],
        out_specs=[o_spec(lambda i: (0, i, 0, 0)), o_spec(lambda i: (0, steps - 1 - i, 0, 0))],
        out_shape=[o_shape, o_shape],
        scratch_shapes=[pltpu.VMEM((batch * 2 * (DN_HEADS // 2), 2 * DN_DK, 2 * DN_DV), F32)],
        compiler_params=pltpu.CompilerParams(dimension_semantics=("arbitrary",)),
        name="dscan",
    )(a1, a1, a2, a2, au, au, ae, ae)


def _post_kernel(oa_ref, of_ref, ob_ref, dz_ref, ga_ref, gd_ref, x_ref, p_ref,
                 wa_ref, wd_ref, wo_ref, wg_ref, wp_ref, nd_ref, npost_ref, nple_ref, o_ref):
    def rms(t, w):
        return t * lax.rsqrt(jnp.mean(t * t, axis=-1, keepdims=True) + EPS) * w

    nd = nd_ref[...]
    parts = []
    for h in range(DN_HEADS):
        sl = slice(h * DN_DV, (h + 1) * DN_DV)
        od = of_ref[:, sl].astype(F32) + ob_ref[:, sl].astype(F32)
        z = dz_ref[:, sl].astype(F32)
        parts.append((rms(od, nd) * (z * jax.nn.sigmoid(z))).astype(BF16))
    y_dn = jnp.dot(jnp.concatenate(parts, axis=1), wd_ref[...], preferred_element_type=F32)
    y_att = jnp.dot(oa_ref[...], wa_ref[...], preferred_element_type=F32)
    merged = (jax.nn.sigmoid(ga_ref[...].astype(F32)) * y_att
              + jax.nn.sigmoid(gd_ref[...].astype(F32)) * y_dn)
    mix = jnp.dot(merged.astype(BF16), wo_ref[...], preferred_element_type=F32)
    x1 = x_ref[...] + rms(mix, npost_ref[...])
    gate = jax.nn.sigmoid(jnp.dot(x1.astype(BF16), wg_ref[...], preferred_element_type=F32))
    e = jnp.dot(p_ref[...].astype(BF16), wp_ref[...], preferred_element_type=F32)
    o_ref[...] = x1 + rms(gate * e, nple_ref[...])


def _post(o_att, o_f, o_b, proj, x2, p2, wa, wd, wo, wg, wp, nd, npost, nple, offs, tm=512):
    n, d = x2.shape
    ple = p2.shape[1]
    row = lambda i: (i, 0)
    const = lambda i: (0, 0)
    return pl.pallas_call(
        _post_kernel,
        grid=(n // tm,),
        in_specs=[
            pl.BlockSpec((tm, ATT_W), row),
            pl.BlockSpec((tm, DN_VW), row),
            pl.BlockSpec((tm, DN_VW), row),
            pl.BlockSpec((tm, DN_VW), lambda i: (i, offs["dz"] // DN_VW)),
            pl.BlockSpec((tm, d), lambda i: (i, offs["ga"] // d)),
            pl.BlockSpec((tm, d), lambda i: (i, offs["gd"] // d)),
            pl.BlockSpec((tm, d), row),
            pl.BlockSpec((tm, ple), row),
            pl.BlockSpec((ATT_W, d), const),
            pl.BlockSpec((DN_VW, d), const),
            pl.BlockSpec((d, d), const),
            pl.BlockSpec((d, d), const),
            pl.BlockSpec((ple, d), const),
            pl.BlockSpec((1, DN_DV), const),
            pl.BlockSpec((1, d), const),
            pl.BlockSpec((1, d), const),
        ],
        out_specs=pl.BlockSpec((tm, d), row),
        out_shape=jax.ShapeDtypeStruct((n, d), F32),
        compiler_params=pltpu.CompilerParams(dimension_semantics=("parallel",)),
        name="post",
    )(o_att, o_f, o_b, proj, proj, proj, x2, p2, wa, wd, wo, wg, wp, nd, npost, nple)


def _rope_tables(seq):
    rows = seq // GRID_W
    row = np.broadcast_to(np.arange(rows)[:, None], (rows, GRID_W)).reshape(seq)
    col = np.broadcast_to(np.arange(GRID_W)[None, :], (rows, GRID_W)).reshape(seq)
    n_freq = HEAD_DIM // 4
    inv_freq = np.float32(ROPE_THETA) ** (-np.arange(n_freq, dtype=np.float32) / np.float32(n_freq))
    ang = np.concatenate([row.astype(np.float32)[:, None] * inv_freq,
                          col.astype(np.float32)[:, None] * inv_freq], axis=-1)
    cos, sin = np.cos(ang), np.sin(ang)
    return (jnp.asarray(np.concatenate([cos, cos], axis=-1), F32),
            jnp.asarray(np.concatenate([-sin, sin], axis=-1), F32))


def kernel(x, p, norm_pre, w_in, q_norm, k_norm, conv_w, a_log, dt_bias, dn_norm, w_br_att,
           w_br_dn, w_out, norm_post, w_ple_proj, w_ple_gate, ple_norm):
    batch, seq, d = x.shape
    depth = w_in.shape[0]
    n = batch * seq
    assert seq % GRID_W == 0 and seq % CHUNK == 0 and d % LANES == 0
    cos, sin = _rope_tables(seq)
    x2 = x.reshape(n, d)
    for i in range(depth):
        w_main, w_small, offs, pieces = _regroup_columns(w_in[i])
        conv_w8 = jnp.pad(conv_w[i], ((0, 8 - CONV_K), (0, 0)))
        proj, small = _proj(x2, norm_pre[i][None, :], w_main, w_small, conv_w8, pieces, seq)

        tq, tk = min(ATT_TQ, seq), min(ATT_TK, seq // 2)
        qt, k_rot, vt = _aprep(proj, cos, sin, _deinterleave(q_norm[i])[None, :],
                               _deinterleave(k_norm[i])[None, :], offs, batch, seq, tq, tk)
        o_att = _attention(proj, qt, k_rot, vt, offs, batch, seq, tq, tk)

        nsm = 4 * DN_HEADS
        nck = n // CHUNK
        small_r = small[:, 2 * DN_HEADS:nsm].reshape(nck, CHUNK, 2, N_TILES, HEADS_PER_TILE)
        small_r = small_r.transpose(0, 2, 3, 4, 1).reshape(nck, 2 * N_TILES, CAT_W)
        small_r = jnp.pad(small_r, ((0, 0), (0, 8 - 2 * N_TILES), (0, 0)))
        par = jnp.stack([a_log[i].reshape(-1), dt_bias[i].reshape(-1)])
        par_col = jnp.pad(jnp.concatenate([jnp.zeros_like(par), par], axis=1),
                          ((0, 6), (0, LANES - nsm)))
        par_row = jnp.repeat(par.reshape(2, 2 * N_TILES, HEADS_PER_TILE), CHUNK, axis=2)
        par_row = jnp.pad(par_row, ((0, 0), (0, 8 - 2 * N_TILES), (0, 0)))
        a1, a2, au, ae = _dchunk(proj, small, small_r, par_col, par_row, offs, batch, seq)
        o_f, o_b = _dscan(a1, a2, au, ae, batch, seq)
        o_f, o_b = o_f.reshape(n, DN_VW), o_b.reshape(n, DN_VW)

        x2 = _post(o_att, o_f, o_b, proj, x2, p[i].reshape(n, -1),
                   w_br_att[i].astype(BF16), w_br_dn[i].astype(BF16), w_out[i].astype(BF16),
                   w_ple_gate[i].astype(BF16), w_ple_proj[i].astype(BF16),
                   dn_norm[i][None, :], norm_post[i][None, :], ple_norm[i][None, :], offs)
    return x2.reshape(batch, seq, d)
```

```python
import functools

import numpy as np
import jax
import jax.numpy as jnp
from jax import lax
from jax.experimental import pallas as pl
from jax.experimental.pallas import tpu as pltpu

F32 = jnp.float32
BF16 = jnp.bfloat16

GRID_W = 64
ATT_HEADS = 8
ATT_KV_HEADS = 2
HEAD_DIM = 128
ROPE_THETA = 10000.0
DN_HEADS = 8
DN_DK = 128
DN_DV = 128
CONV_K = 5
CHUNK = 64
EPS = 1e-6

ATT_W = ATT_HEADS * HEAD_DIM
KV_W = ATT_KV_HEADS * HEAD_DIM
DN_KW = DN_HEADS * DN_DK
DN_VW = DN_HEADS * DN_DV
GROUP = ATT_HEADS // ATT_KV_HEADS
LANES = 128


def _deinterleave(t):
    lead = t.shape[:-1]
    t = t.reshape(lead + (-1, HEAD_DIM // 2, 2))
    return jnp.swapaxes(t, -1, -2).reshape(lead + (-1,))


PROJ_TN = 4352
CONV_HALO = 16
CONV_PIECE = 512


def _regroup_columns(w):
    d_model = w.shape[0]
    sizes = (ATT_W, KV_W, KV_W, ATT_W, DN_KW, DN_KW, DN_VW, 2 * DN_HEADS, 2 * DN_HEADS, DN_VW,
             d_model, d_model)
    assert w.shape[1] == sum(sizes)
    aq, ak, av, az, dq, dk, dv, db, da, dz, ga, gd = jnp.split(w, np.cumsum(sizes)[:-1], axis=1)
    aq0, aq1 = jnp.split(_deinterleave(aq), 2, axis=1)
    az0, az1 = jnp.split(az, 2, axis=1)
    dk0, dk1 = jnp.split(dk, 2, axis=1)
    groups = [("ga", ga, "plain", 0), ("gd", gd, "plain", 0), ("dq", dq, "conv_q", 0),
              ("dk0", dk0, "conv_k", DN_KW), ("aq0", aq0, "plain", 0),
              ("ak", _deinterleave(ak), "plain", 0),
              ("av", av, "plain", 0), ("dk1", dk1, "conv_k", DN_KW + DN_KW // 2),
              ("dz", dz, "plain", 0), ("dv", dv, "conv_v", 2 * DN_KW), ("aq1", aq1, "plain", 0),
              ("az0", az0, "plain", 0), ("az1", az1, "plain", 0)]
    offsets, off = {}, 0
    pieces = [[] for _ in range(sum(g.shape[1] for _, g, _, _ in groups) // PROJ_TN)]
    for name, g, kind, ch in groups:
        offsets[name] = off
        width = g.shape[1]
        tile, local = divmod(off, PROJ_TN)
        assert local + width <= PROJ_TN
        if kind == "plain":
            last = pieces[tile][-1] if pieces[tile] else None
            if last is not None and last[2] == "plain" and last[0] + last[1] == local:
                pieces[tile][-1] = (last[0], last[1] + width, "plain", 0)
            else:
                pieces[tile].append((local, width, "plain", 0))
        else:
            for k in range(0, width, CONV_PIECE):
                pieces[tile].append((local + k, CONV_PIECE, kind, ch + k))
        off += width
    main = jnp.concatenate([g.astype(BF16) for _, g, _, _ in groups], axis=1)
    small = jnp.pad(jnp.concatenate([db, da], axis=1), ((0, 0), (0, LANES - 4 * DN_HEADS)))
    return main, small.astype(BF16), offsets, pieces


def _proj_kernel(x_ref, xp_ref, xn_ref, g_ref, w_ref, ws_ref, cw_ref, o_ref, os_ref, h_ref,
                 *, tiles_per_seq, pieces):
    i, j = pl.program_id(0), pl.program_id(1)
    tm = x_ref.shape[0]
    ext = tm + 2 * CONV_HALO
    half = CONV_K // 2

    def normed(x):
        r = lax.rsqrt(jnp.mean(x * x, axis=-1, keepdims=True) + EPS)
        return (x * r * g_ref[...]).astype(BF16)

    @pl.when(j == 0)
    def _():
        t = i % tiles_per_seq
        h = normed(x_ref[...])
        zero = jnp.zeros((CONV_HALO, x_ref.shape[1]), BF16)
        h_ref[0:CONV_HALO, :] = jnp.where(t == 0, zero, normed(xp_ref[...]))
        h_ref[CONV_HALO:CONV_HALO + tm, :] = h
        h_ref[CONV_HALO + tm:, :] = jnp.where(t == tiles_per_seq - 1, zero, normed(xn_ref[...]))
        os_ref[...] = jnp.dot(h, ws_ref[...], preferred_element_type=F32)

    def tile(tile_pieces):
        for start, width, kind, ch in tile_pieces:
            cols = slice(start, start + width)
            if kind == "plain":
                o_ref[:, cols] = jnp.dot(h_ref[CONV_HALO:CONV_HALO + tm, :], w_ref[:, cols],
                                         preferred_element_type=F32).astype(BF16)
                continue
            r = jnp.dot(h_ref[...], w_ref[:, cols], preferred_element_type=F32)
            taps = cw_ref[:, ch:ch + width]
            acc = r[CONV_HALO:CONV_HALO + tm] * taps[half:half + 1]
            for t in range(CONV_K):
                if t != half:
                    shifted = pltpu.roll(r, (half - t) % ext, axis=0)
                    acc = acc + shifted[CONV_HALO:CONV_HALO + tm] * taps[t:t + 1]
            y = acc * jax.nn.sigmoid(acc)
            for hh in range(width // DN_DK):
                yh = y[:, hh * DN_DK:(hh + 1) * DN_DK]
                if kind != "conv_v":
                    inv = lax.rsqrt(jnp.sum(yh * yh, axis=-1, keepdims=True) + EPS)
                    yh = yh * (inv * DN_DK ** -0.5 if kind == "conv_q" else inv)
                o_ref[:, start + hh * DN_DK:start + (hh + 1) * DN_DK] = yh.astype(BF16)

    for jj, tile_pieces in enumerate(pieces):
        pl.when(j == jj)(functools.partial(tile, tile_pieces))


def _proj(x2, g, w_main, w_small, conv_w8, pieces, seq, tm=1024):
    n, d = x2.shape
    main_w = w_main.shape[1]
    tm = min(tm, seq)
    hb = tm // CONV_HALO
    last_hb = n // CONV_HALO - 1
    kern = functools.partial(_proj_kernel, tiles_per_seq=seq // tm, pieces=pieces)
    return pl.pallas_call(
        kern,
        grid=(n // tm, main_w // PROJ_TN),
        in_specs=[
            pl.BlockSpec((tm, d), lambda i, j: (i, 0)),
            pl.BlockSpec((CONV_HALO, d), lambda i, j: (jnp.maximum(i * hb - 1, 0), 0)),
            pl.BlockSpec((CONV_HALO, d), lambda i, j: (jnp.minimum((i + 1) * hb, last_hb), 0)),
            pl.BlockSpec((1, d), lambda i, j: (0, 0)),
            pl.BlockSpec((d, PROJ_TN), lambda i, j: (0, j)),
            pl.BlockSpec((d, LANES), lambda i, j: (0, 0)),
            pl.BlockSpec(conv_w8.shape, lambda i, j: (0, 0)),
        ],
        out_specs=[
            pl.BlockSpec((tm, PROJ_TN), lambda i, j: (i, j)),
            pl.BlockSpec((tm, LANES), lambda i, j: (i, 0)),
        ],
        out_shape=[
            jax.ShapeDtypeStruct((n, main_w), BF16),
            jax.ShapeDtypeStruct((n, LANES), F32),
        ],
        scratch_shapes=[pltpu.VMEM((tm + 2 * CONV_HALO, d), BF16)],
        compiler_params=pltpu.CompilerParams(dimension_semantics=("parallel", "arbitrary")),
        name="proj",
    )(x2, x2, x2, g, w_main, w_small, conv_w8)


def _norm_rope(xh, w, cos, sin):
    r = lax.rsqrt(jnp.mean(xh * xh, axis=-1, keepdims=True) + EPS)
    xn = xh * r * w
    return xn * cos + pltpu.roll(xn, HEAD_DIM // 2, axis=1) * sin


ATT_TQ = 1024
ATT_TK = 512
VT_ROWS = HEAD_DIM + 16


def _aprep_kernel(q0_ref, q1_ref, k_ref, v_ref, cos_ref, sin_ref, wq_ref, wk_ref,
                  qt_ref, kr_ref, vt_ref):
    cos, sin = cos_ref[...], sin_ref[...]
    scale = HEAD_DIM ** -0.5
    tm = k_ref.shape[0]
    tk = vt_ref.shape[-1]
    for h, q_ref in enumerate((q0_ref, q1_ref)):
        sl = slice(h * HEAD_DIM, (h + 1) * HEAD_DIM)
        kr_ref[:, sl] = _norm_rope(k_ref[:, sl].astype(F32), wk_ref[...], cos, sin).astype(BF16)
        for c in range(tm // tk):
            vt_ref[h, c, 0:HEAD_DIM, :] = v_ref[c * tk:(c + 1) * tk, sl].astype(F32).T.astype(BF16)
            vt_ref[h, c, HEAD_DIM:, :] = jnp.ones((VT_ROWS - HEAD_DIM, tk), BF16)
        for g in range(GROUP):
            qs = slice(g * HEAD_DIM, (g + 1) * HEAD_DIM)
            qr = _norm_rope(q_ref[:, qs].astype(F32), wq_ref[...], cos, sin) * scale
            qt_ref[h, :, g * tm:(g + 1) * tm] = qr.T.astype(BF16)


def _aprep(proj, cos, sin, wq, wk, offs, batch, seq, tm, tk):
    assert ATT_KV_HEADS == 2
    n = proj.shape[0]
    t = seq // tm
    cpt = tm // tk
    gw = GROUP * HEAD_DIM
    return pl.pallas_call(
        _aprep_kernel,
        grid=(batch, t),
        in_specs=[
            pl.BlockSpec((tm, gw), lambda b, i: (b * t + i, offs["aq0"] // gw)),
            pl.BlockSpec((tm, gw), lambda b, i: (b * t + i, offs["aq1"] // gw)),
            pl.BlockSpec((tm, KV_W), lambda b, i: (b * t + i, offs["ak"] // KV_W)),
            pl.BlockSpec((tm, KV_W), lambda b, i: (b * t + i, offs["av"] // KV_W)),
            pl.BlockSpec((tm, HEAD_DIM), lambda b, i: (i, 0)),
            pl.BlockSpec((tm, HEAD_DIM), lambda b, i: (i, 0)),
            pl.BlockSpec((1, HEAD_DIM), lambda b, i: (0, 0)),
            pl.BlockSpec((1, HEAD_DIM), lambda b, i: (0, 0)),
        ],
        out_specs=[
            pl.BlockSpec((None, None, ATT_KV_HEADS, HEAD_DIM, GROUP * tm),
                         lambda b, i: (b, i, 0, 0, 0)),
            pl.BlockSpec((tm, KV_W), lambda b, i: (b * t + i, 0)),
            pl.BlockSpec((None, ATT_KV_HEADS, cpt, VT_ROWS, tk), lambda b, i: (b, 0, i, 0, 0)),
        ],
        out_shape=[
            jax.ShapeDtypeStruct((batch, t, ATT_KV_HEADS, HEAD_DIM, GROUP * tm), BF16),
            jax.ShapeDtypeStruct((n, KV_W), BF16),
            jax.ShapeDtypeStruct((batch, ATT_KV_HEADS, seq // tk, VT_ROWS, tk), BF16),
        ],
        compiler_params=pltpu.CompilerParams(dimension_semantics=("parallel", "parallel")),
        name="aprep",
    )(proj, proj, proj, proj, cos, sin, wq, wk)


def _attn_kernel(qt_ref, z_ref, k_ref, vt_ref, o_ref, acc_ref, s0_ref, s1_ref, p0_ref, p1_ref,
                 *, tq, tk, seq):
    cols = GROUP * tq
    n_chunks = seq // tk
    qt = qt_ref[...]

    def scores(c, s_ref):
        start = pl.multiple_of(c * tk, tk)
        s = jnp.dot(k_ref[pl.ds(start, tk), :], qt, preferred_element_type=F32)
        s_ref[...] = s
        return jnp.max(s, axis=0, keepdims=True)

    def softmax(s_ref, p_ref, m, smax):
        m_new = jnp.maximum(m, smax)
        p_ref[...] = jnp.exp((s_ref[...] - m_new).astype(BF16))
        return m_new, jnp.exp(m - m_new)

    def accumulate(c, p_ref, alpha, init=False):
        pv = jnp.dot(vt_ref[c], p_ref[...], preferred_element_type=F32)
        acc_ref[...] = pv if init else alpha * acc_ref[...] + pv

    def pair(c, carry, first, last):
        m, alpha_prev, smax0 = carry
        m, alpha0 = softmax(s0_ref, p0_ref, m, smax0)
        smax1 = scores(c + 1, s1_ref)
        if not first:
            accumulate(c - 1, p1_ref, alpha_prev)
        m, alpha1 = softmax(s1_ref, p1_ref, m, smax1)
        if not last:
            smax0 = scores(c + 2, s0_ref)
        accumulate(c, p0_ref, alpha0, init=first)
        return m, alpha1, smax0

    n_pairs = n_chunks // 2
    smax = scores(0, s0_ref)
    carry = (jnp.full((1, cols), -jnp.inf, F32), jnp.ones((1, cols), F32), smax)
    carry = pair(0, carry, True, n_pairs == 1)
    if n_pairs > 1:
        carry = lax.fori_loop(1, n_pairs - 1, lambda j, cr: pair(2 * j, cr, False, False), carry)
        carry = pair(n_chunks - 2, carry, False, True)
    accumulate(n_chunks - 1, p1_ref, carry[1])
    acc = acc_ref[...]
    o_t = acc[:HEAD_DIM] / acc[HEAD_DIM:HEAD_DIM + 1]
    for g in range(GROUP):
        sl = slice(g * HEAD_DIM, (g + 1) * HEAD_DIM)
        z = z_ref[:, sl].astype(F32)
        o_ref[:, sl] = (o_t[:, g * tq:(g + 1) * tq].T * (z * jax.nn.sigmoid(z))).astype(BF16)


def _attention(proj, qt, k_rot, vt, offs, batch, seq, tq, tk):
    n = proj.shape[0]
    gw = GROUP * HEAD_DIM
    t = seq // tq
    cols = GROUP * tq
    assert (seq // tk) % 2 == 0 and offs["az1"] == offs["az0"] + gw
    kern = functools.partial(_attn_kernel, tq=tq, tk=tk, seq=seq)
    return pl.pallas_call(
        kern,
        grid=(batch, ATT_KV_HEADS, t),
        in_specs=[
            pl.BlockSpec((None, None, None, HEAD_DIM, cols), lambda b, h, i: (b, i, h, 0, 0)),
            pl.BlockSpec((tq, gw), lambda b, h, i: (b * t + i, offs["az0"] // gw + h)),
            pl.BlockSpec((seq, HEAD_DIM), lambda b, h, i: (b, h)),
            pl.BlockSpec((None, None, seq // tk, VT_ROWS, tk), lambda b, h, i: (b, h, 0, 0, 0)),
        ],
        out_specs=pl.BlockSpec((tq, gw), lambda b, h, i: (b * t + i, h)),
        out_shape=jax.ShapeDtypeStruct((n, ATT_W), BF16),
        scratch_shapes=[pltpu.VMEM((VT_ROWS, cols), F32),
                        pltpu.VMEM((tk, cols), F32), pltpu.VMEM((tk, cols), F32),
                        pltpu.VMEM((tk, cols), BF16), pltpu.VMEM((tk, cols), BF16)],
        compiler_params=pltpu.CompilerParams(
            dimension_semantics=("parallel", "parallel", "arbitrary")),
        name="attn",
    )(qt, proj, k_rot, vt)


HEADS_PER_TILE = 4
CHUNK_SHIFT = CHUNK.bit_length() - 1
assert CHUNK == 1 << CHUNK_SHIFT and DN_DK & (DN_DK - 1) == 0
CAT_W = HEADS_PER_TILE * CHUNK
N_TILES = DN_HEADS // HEADS_PER_TILE


def _split3(x):
    p1 = x.astype(BF16)
    r1 = x - p1.astype(F32)
    p2 = r1.astype(BF16)
    p3 = (r1 - p2.astype(F32)).astype(BF16)
    return p1, p2, p3


def _softplus(x):
    return jnp.maximum(x, 0.0) + jnp.log(1.0 + jnp.exp(-jnp.abs(x)))


def _dot(a, b):
    return jnp.dot(a, b, preferred_element_type=F32)


def _dchunk_kernel(q0_ref, q1_ref, k0_ref, k1_ref, v0_ref, v1_ref, c_ref, r_ref, pc_ref, pr_ref,
                   a1_ref, a2_ref, au_ref, ae_ref, *, cpb):
    C, W, T = CHUNK, CAT_W, HEADS_PER_TILE
    rows = cpb * C

    def cum_matrix(n, lower):
        i = lax.broadcasted_iota(jnp.int32, (n, n), 0)
        j = lax.broadcasted_iota(jnp.int32, (n, n), 1)
        same = (i >> CHUNK_SHIFT) == (j >> CHUNK_SHIFT)
        return (same & ((i >= j) if lower else (i <= j))).astype(BF16)

    col_lower, col_upper = cum_matrix(rows, True), cum_matrix(rows, False)
    row_lower, row_upper = cum_matrix(W, True), cum_matrix(W, False)

    blk = c_ref[...]
    beta_all = jax.nn.sigmoid(blk)
    g_parts = _split3(-jnp.exp(pc_ref[0:1, :]) * _softplus(blk + pc_ref[1:2, :]))
    gcol = (sum(_dot(col_lower, p) for p in g_parts),
            sum(_dot(col_upper, p) for p in g_parts))
    blk_r = r_ref[...]
    g_r = (-jnp.exp(pr_ref[0][None]) * _softplus(blk_r + pr_ref[1][None])).reshape(cpb * 8, W)
    gr_parts = _split3(g_r)
    grow = (sum(_dot(p, row_upper) for p in gr_parts),
            sum(_dot(p, row_lower) for p in gr_parts))

    ri = lax.broadcasted_iota(jnp.int32, (C, W), 0)
    li = lax.broadcasted_iota(jnp.int32, (C, W), 1)
    ci = li & (C - 1)
    lb = li >> CHUNK_SHIFT
    incl = (ri >= ci, ri <= ci)
    offdiag = ri != ci
    eye = (ri == ci).astype(F32)
    same = {1 << lg: (ri >> lg) == (ci >> lg) for lg in range(1, CHUNK_SHIFT)}
    level_masks = [same[2 * s] & ~same[s] for s in sorted(same)[:-1]] + [~same[C // 2]]
    tile_sel = [lb == t for t in range(T)]
    head_of_lane = lax.broadcasted_iota(jnp.int32, (C, T * DN_DK), 1) >> (DN_DK.bit_length() - 1)

    def block_diag(y):
        zero = jnp.zeros_like(y)
        return jnp.concatenate([jnp.where(tile_sel[t], y, zero) for t in range(T)], axis=0)

    def cat_bcast(cols):
        out = jnp.broadcast_to(cols[T - 1], (C, W))
        for t in range(T - 2, -1, -1):
            out = jnp.where(tile_sel[t], jnp.broadcast_to(cols[t], (C, W)), out)
        return out

    pairs = [(c, t) for c in range(cpb) for t in range(N_TILES)]
    units = [(c, t, d) for c in range(cpb) for t in range(N_TILES) for d in range(2)]

    qk, kk, q4s, k4s, v4s = {}, {}, {}, {}, {}
    for c, t in pairs:
        rs = slice(c * C, (c + 1) * C)
        q4 = (q0_ref, q1_ref)[t][rs, :]
        k4 = (k0_ref, k1_ref)[t][rs, :]
        v4 = (v0_ref, v1_ref)[t][rs, :]
        zero = jnp.zeros_like(k4)
        k_bd = jnp.concatenate([jnp.where(head_of_lane == h, k4, zero) for h in range(T)], axis=0)
        r = lax.dot_general(jnp.concatenate([q4, k4], axis=0), k_bd, (((1,), (1,)), ((), ())),
                            preferred_element_type=F32)
        qk[c, t], kk[c, t] = r[:C], r[C:]
        q4s[c, t], k4s[c, t], v4s[c, t] = q4, k4, v4

    lmat, amat, xinv = {}, {}, {}
    for c, t, d in units:
        rs = slice(c * C, (c + 1) * C)
        base = d * DN_HEADS + t * T
        gc = cat_bcast([gcol[d][rs, 2 * DN_HEADS + base + h:2 * DN_HEADS + base + h + 1]
                        for h in range(T)])
        beta = cat_bcast([beta_all[rs, base + h:base + h + 1] for h in range(T)])
        row = c * 8 + d * N_TILES + t
        gr = jnp.broadcast_to(grow[d][row:row + 1, :], (C, W))
        decay = jnp.exp(jnp.where(incl[d], gc - gr, -1e30))
        l = jnp.where(offdiag, beta * kk[c, t] * decay, 0.0)
        lmat[c, t, d] = l
        amat[c, t, d] = qk[c, t] * decay
        xinv[c, t, d] = eye - jnp.where(same[2], l, 0.0)

    for msk in level_masks:
        ys, xbs = {}, {}
        for u in units:
            e = jnp.where(msk, lmat[u], 0.0).astype(BF16)
            xbs[u] = xinv[u].astype(BF16)
            ys[u] = _dot(e, block_diag(xbs[u]))
        for u in units:
            xinv[u] = xinv[u] - _dot(xbs[u], block_diag(ys[u].astype(BF16)))

    for c, t, d in units:
        rs = slice(c * C, (c + 1) * C)
        kd_rows, rhs_rows = [], []
        for h in range(T):
            head = t * T + h
            col = d * DN_HEADS + head
            hs = slice(h * DN_DK, (h + 1) * DN_DK)
            out_s = slice(head * DN_DK, (head + 1) * DN_DK)
            gc = jnp.broadcast_to(gcol[d][rs, 2 * DN_HEADS + col:2 * DN_HEADS + col + 1],
                                  (C, DN_DK))
            beta = jnp.broadcast_to(beta_all[rs, col:col + 1], (C, DN_DK))
            g_last = gc[C - 1:C, :] if d == 0 else gc[0:1, :]
            eg = jnp.exp(gc)
            kf = k4s[c, t][:, hs].astype(F32)
            rhs = jnp.concatenate([kf * (beta * eg), v4s[c, t][:, hs].astype(F32) * beta],
                                  axis=1).astype(BF16)
            zero = jnp.zeros_like(rhs)
            rhs_rows.append(jnp.concatenate([rhs if hh == h else zero for hh in range(T)], axis=1))
            kd_rows.append(kf * jnp.exp(g_last - gc))
            a1_ref[c, d, C:2 * C, out_s] = (q4s[c, t][:, hs].astype(F32) * eg).astype(BF16)
            ae_ref[c, d, :, out_s] = jnp.exp(g_last)
        wu = _dot(xinv[c, t, d].astype(BF16), jnp.concatenate(rhs_rows, axis=0))
        for h in range(T):
            head = t * T + h
            out_s = slice(head * DN_DK, (head + 1) * DN_DK)
            a1_ref[c, d, 0:C, out_s] = wu[:, 2 * h * DN_DK:(2 * h + 1) * DN_DK].astype(BF16)
            au_ref[c, d, :, out_s] = wu[:, (2 * h + 1) * DN_DK:(2 * h + 2) * DN_DK].astype(BF16)
        cat_s = slice(t * W, (t + 1) * W)
        a2_ref[c, d, 0:C, cat_s] = amat[c, t, d].astype(BF16)
        a2_ref[c, d, C:C + DN_DK, cat_s] = jnp.concatenate(kd_rows, axis=0).T.astype(BF16)


def _dchunk(proj, small, small_r, par_col, par_row, offs, batch, seq, cpb=4):
    assert N_TILES == 2
    nc = seq // CHUNK
    steps = nc // cpb
    tw = HEADS_PER_TILE * DN_DK
    kern = functools.partial(_dchunk_kernel, cpb=cpb)
    out5 = lambda b, i: (b, i, 0, 0, 0)
    tile_cols = [offs["dq"], offs["dq"] + tw, offs["dk0"], offs["dk1"], offs["dv"], offs["dv"] + tw]
    qkv_specs = [pl.BlockSpec((cpb * CHUNK, tw), functools.partial(
        lambda b, i, col: (b * steps + i, col), col=c // tw)) for c in tile_cols]
    return pl.pallas_call(
        kern,
        grid=(batch, steps),
        in_specs=qkv_specs + [
            pl.BlockSpec((cpb * CHUNK, LANES), lambda b, i: (b * steps + i, 0)),
            pl.BlockSpec((cpb, 8, CAT_W), lambda b, i: (b * steps + i, 0, 0)),
            pl.BlockSpec((8, LANES), lambda b, i: (0, 0)),
            pl.BlockSpec((2, 8, CAT_W), lambda b, i: (0, 0, 0)),
        ],
        out_specs=[
            pl.BlockSpec((None, cpb, 2, 2 * CHUNK, DN_KW), out5),
            pl.BlockSpec((None, cpb, 2, CHUNK + DN_DK, DN_HEADS * CHUNK), out5),
            pl.BlockSpec((None, cpb, 2, CHUNK, DN_VW), out5),
            pl.BlockSpec((None, cpb, 2, 1, DN_VW), out5),
        ],
        out_shape=[
            jax.ShapeDtypeStruct((batch, nc, 2, 2 * CHUNK, DN_KW), BF16),
            jax.ShapeDtypeStruct((batch, nc, 2, CHUNK + DN_DK, DN_HEADS * CHUNK), BF16),
            jax.ShapeDtypeStruct((batch, nc, 2, CHUNK, DN_VW), BF16),
            jax.ShapeDtypeStruct((batch, nc, 2, 1, DN_VW), F32),
        ],
        compiler_params=pltpu.CompilerParams(dimension_semantics=("parallel", "parallel")),
        name="dchunk",
    )(proj, proj, proj, proj, proj, proj, small, small_r, par_col, par_row)


def _dscan_kernel(a1f_ref, a1b_ref, a2f_ref, a2b_ref, auf_ref, aub_ref, aef_ref, aeb_ref,
                  of_ref, ob_ref, s_ref, *, batch, cps):
    C = CHUNK
    PW = 2 * DN_DV
    n_pairs = DN_HEADS // 2

    @pl.when(pl.program_id(0) == 0)
    def _():
        s_ref[...] = jnp.zeros(s_ref.shape, F32)

    first = lax.broadcasted_iota(jnp.int32, (DN_DK, PW), 1) < DN_DV
    first_c = lax.broadcasted_iota(jnp.int32, (C, PW), 1) < DN_DV
    refs = ((a1f_ref, a2f_ref, auf_ref, aef_ref, of_ref), (a1b_ref, a2b_ref, aub_ref, aeb_ref, ob_ref))
    units = [(b, d, p) for b in range(batch) for d in range(2) for p in range(n_pairs)]
    sidx = lambda u: (u[0] * 2 + u[1]) * n_pairs + u[2]

    states = {u: s_ref[sidx(u)] for u in units}
    for step in range(cps):
        r1, r2 = {}, {}
        cc = (step, cps - 1 - step)
        for u in units:
            b, d, p = u
            r1[u] = _dot(refs[d][0][b, cc[d], :, p * PW:(p + 1) * PW], states[u].astype(BF16))
        for u in units:
            b, d, p = u
            vb = (refs[d][2][b, cc[d], :, p * PW:(p + 1) * PW].astype(F32) - r1[u][:C]).astype(BF16)
            zero = jnp.zeros_like(vb)
            v_bd = jnp.concatenate([jnp.where(first_c, vb, zero), jnp.where(first_c, zero, vb)],
                                   axis=0)
            r2[u] = _dot(refs[d][1][b, cc[d], :, p * 2 * C:(p + 1) * 2 * C], v_bd)
        for u in units:
            b, d, p = u
            refs[d][4][b, cc[d], :, p * PW:(p + 1) * PW] = (r1[u][C:] + r2[u][:C]).astype(BF16)
            e = refs[d][3][b, cc[d], 0:1, p * PW:(p + 1) * PW]
            upd = r2[u][C:]
            s = states[u]
            states[u] = jnp.concatenate(
                [s[:DN_DK] * e + jnp.where(first, upd, 0.0),
                 s[DN_DK:] * e + jnp.where(first, 0.0, upd)], axis=0)
    for u in units:
        s_ref[sidx(u)] = states[u]


def _dscan(a1, a2, au, ae, batch, seq, cps=8):
    nc = seq // CHUNK
    steps = nc // cps
    kern = functools.partial(_dscan_kernel, batch=batch, cps=cps)
    fwd = lambda i: (0, i, 0, 0, 0)
    bwd = lambda i: (0, steps - 1 - i, 1, 0, 0)

    def spec(arr, imap):
        return pl.BlockSpec((batch, cps, None) + arr.shape[3:], imap)

    o_spec = lambda imap: pl.BlockSpec((batch, cps, CHUNK, DN_VW), imap)
    o_shape = jax.ShapeDtypeStruct((batch, nc, CHUNK, DN_VW), BF16)
    return pl.pallas_call(
        kern,
        grid=(steps,),
        in_specs=[spec(a1, fwd), spec(a1, bwd), spec(a2, fwd), spec(a2, bwd),
                  spec(au, fwd), spec(au, bwd), spec(ae, fwd), spec(ae, bwd)],
        out_specs=[o_spec(lambda i: (0, i, 0, 0)), o_spec(lambda i: (0, steps - 1 - i, 0, 0))],
        out_shape=[o_shape, o_shape],
        scratch_shapes=[pltpu.VMEM((batch * 2 * (DN_HEADS // 2), 2 * DN_DK, 2 * DN_DV), F32)],
        compiler_params=pltpu.CompilerParams(dimension_semantics=("arbitrary",)),
        name="dscan",
    )(a1, a1, a2, a2, au, au, ae, ae)


def _post_kernel(oa_ref, of_ref, ob_ref, dz_ref, ga_ref, gd_ref, x_ref, p_ref,
                 wa_ref, wd_ref, wo_ref, wg_ref, wp_ref, nd_ref, npost_ref, nple_ref, o_ref):
    def rms(t, w):
        return t * lax.rsqrt(jnp.mean(t * t, axis=-1, keepdims=True) + EPS) * w

    nd = nd_ref[...]
    parts = []
    for h in range(DN_HEADS):
        sl = slice(h * DN_DV, (h + 1) * DN_DV)
        od = of_ref[:, sl].astype(F32) + ob_ref[:, sl].astype(F32)
        z = dz_ref[:, sl].astype(F32)
        parts.append((rms(od, nd) * (z * jax.nn.sigmoid(z))).astype(BF16))
    y_dn = jnp.dot(jnp.concatenate(parts, axis=1), wd_ref[...], preferred_element_type=F32)
    y_att = jnp.dot(oa_ref[...], wa_ref[...], preferred_element_type=F32)
    merged = (jax.nn.sigmoid(ga_ref[...].astype(F32)) * y_att
              + jax.nn.sigmoid(gd_ref[...].astype(F32)) * y_dn)
    mix = jnp.dot(merged.astype(BF16), wo_ref[...], preferred_element_type=F32)
    x1 = x_ref[...] + rms(mix, npost_ref[...])
    gate = jax.nn.sigmoid(jnp.dot(x1.astype(BF16), wg_ref[...], preferred_element_type=F32))
    e = jnp.dot(p_ref[...].astype(BF16), wp_ref[...], preferred_element_type=F32)
    o_ref[...] = x1 + rms(gate * e, nple_ref[...])


def _post(o_att, o_f, o_b, proj, x2, p2, wa, wd, wo, wg, wp, nd, npost, nple, offs, tm=512):
    n, d = x2.shape
    ple = p2.shape[1]
    row = lambda i: (i, 0)
    const = lambda i: (0, 0)
    return pl.pallas_call(
        _post_kernel,
        grid=(n // tm,),
        in_specs=[
            pl.BlockSpec((tm, ATT_W), row),
            pl.BlockSpec((tm, DN_VW), row),
            pl.BlockSpec((tm, DN_VW), row),
            pl.BlockSpec((tm, DN_VW), lambda i: (i, offs["dz"] // DN_VW)),
            pl.BlockSpec((tm, d), lambda i: (i, offs["ga"] // d)),
            pl.BlockSpec((tm, d), lambda i: (i, offs["gd"] // d)),
            pl.BlockSpec((tm, d), row),
            pl.BlockSpec((tm, ple), row),
            pl.BlockSpec((ATT_W, d), const),
            pl.BlockSpec((DN_VW, d), const),
            pl.BlockSpec((d, d), const),
            pl.BlockSpec((d, d), const),
            pl.BlockSpec((ple, d), const),
            pl.BlockSpec((1, DN_DV), const),
            pl.BlockSpec((1, d), const),
            pl.BlockSpec((1, d), const),
        ],
        out_specs=pl.BlockSpec((tm, d), row),
        out_shape=jax.ShapeDtypeStruct((n, d), F32),
        compiler_params=pltpu.CompilerParams(dimension_semantics=("parallel",)),
        name="post",
    )(o_att, o_f, o_b, proj, proj, proj, x2, p2, wa, wd, wo, wg, wp, nd, npost, nple)


def _rope_tables(seq):
    rows = seq // GRID_W
    row = np.broadcast_to(np.arange(rows)[:, None], (rows, GRID_W)).reshape(seq)
    col = np.broadcast_to(np.arange(GRID_W)[None, :], (rows, GRID_W)).reshape(seq)
    n_freq = HEAD_DIM // 4
    inv_freq = np.float32(ROPE_THETA) ** (-np.arange(n_freq, dtype=np.float32) / np.float32(n_freq))
    ang = np.concatenate([row.astype(np.float32)[:, None] * inv_freq,
                          col.astype(np.float32)[:, None] * inv_freq], axis=-1)
    cos, sin = np.cos(ang), np.sin(ang)
    return (jnp.asarray(np.concatenate([cos, cos], axis=-1), F32),
            jnp.asarray(np.concatenate([-sin, sin], axis=-1), F32))


def kernel(x, p, norm_pre, w_in, q_norm, k_norm, conv_w, a_log, dt_bias, dn_norm, w_br_att,
           w_br_dn, w_out, norm_post, w_ple_proj, w_ple_gate, ple_norm):
    batch, seq, d = x.shape
    depth = w_in.shape[0]
    n = batch * seq
    assert seq % GRID_W == 0 and seq % CHUNK == 0 and d % LANES == 0
    cos, sin = _rope_tables(seq)
    x2 = x.reshape(n, d)
    for i in range(depth):
        w_main, w_small, offs, pieces = _regroup_columns(w_in[i])
        conv_w8 = jnp.pad(conv_w[i], ((0, 8 - CONV_K), (0, 0)))
        proj, small = _proj(x2, norm_pre[i][None, :], w_main, w_small, conv_w8, pieces, seq)

        tq, tk = min(ATT_TQ, seq), min(ATT_TK, seq // 2)
        qt, k_rot, vt = _aprep(proj, cos, sin, _deinterleave(q_norm[i])[None, :],
                               _deinterleave(k_norm[i])[None, :], offs, batch, seq, tq, tk)
        o_att = _attention(proj, qt, k_rot, vt, offs, batch, seq, tq, tk)

        nsm = 4 * DN_HEADS
        nck = n // CHUNK
        small_r = small[:, 2 * DN_HEADS:nsm].reshape(nck, CHUNK, 2, N_TILES, HEADS_PER_TILE)
        small_r = small_r.transpose(0, 2, 3, 4, 1).reshape(nck, 2 * N_TILES, CAT_W)
        small_r = jnp.pad(small_r, ((0, 0), (0, 8 - 2 * N_TILES), (0, 0)))
        par = jnp.stack([a_log[i].reshape(-1), dt_bias[i].reshape(-1)])
        par_col = jnp.pad(jnp.concatenate([jnp.zeros_like(par), par], axis=1),
                          ((0, 6), (0, LANES - nsm)))
        par_row = jnp.repeat(par.reshape(2, 2 * N_TILES, HEADS_PER_TILE), CHUNK, axis=2)
        par_row = jnp.pad(par_row, ((0, 0), (0, 8 - 2 * N_TILES), (0, 0)))
        a1, a2, au, ae = _dchunk(proj, small, small_r, par_col, par_row, offs, batch, seq)
        o_f, o_b = _dscan(a1, a2, au, ae, batch, seq)
        o_f, o_b = o_f.reshape(n, DN_VW), o_b.reshape(n, DN_VW)

        x2 = _post(o_att, o_f, o_b, proj, x2, p[i].reshape(n, -1),
                   w_br_att[i].astype(BF16), w_br_dn[i].astype(BF16), w_out[i].astype(BF16),
                   w_ple_gate[i].astype(BF16), w_ple_proj[i].astype(BF16),
                   dn_norm[i][None, :], norm_post[i][None, :], ple_norm[i][None, :], offs)
    return x2.reshape(batch, seq, d)
```

```python
import functools

import numpy as np
import jax
import jax.numpy as jnp
from jax import lax
from jax.experimental import pallas as pl
from jax.experimental.pallas import tpu as pltpu

F32 = jnp.float32
BF16 = jnp.bfloat16

GRID_W = 64
ATT_HEADS = 8
ATT_KV_HEADS = 2
HEAD_DIM = 128
ROPE_THETA = 10000.0
DN_HEADS = 8
DN_DK = 128
DN_DV = 128
CONV_K = 5
CHUNK = 64
EPS = 1e-6

ATT_W = ATT_HEADS * HEAD_DIM
KV_W = ATT_KV_HEADS * HEAD_DIM
DN_KW = DN_HEADS * DN_DK
DN_VW = DN_HEADS * DN_DV
GROUP = ATT_HEADS // ATT_KV_HEADS
LANES = 128


def _deinterleave(t):
    lead = t.shape[:-1]
    t = t.reshape(lead + (-1, HEAD_DIM // 2, 2))
    return jnp.swapaxes(t, -1, -2).reshape(lead + (-1,))


PROJ_TN = 4352
CONV_HALO = 16
CONV_PIECE = 512


def _regroup_columns(w):
    d_model = w.shape[0]
    sizes = (ATT_W, KV_W, KV_W, ATT_W, DN_KW, DN_KW, DN_VW, 2 * DN_HEADS, 2 * DN_HEADS, DN_VW,
             d_model, d_model)
    assert w.shape[1] == sum(sizes)
    aq, ak, av, az, dq, dk, dv, db, da, dz, ga, gd = jnp.split(w, np.cumsum(sizes)[:-1], axis=1)
    aq0, aq1 = jnp.split(_deinterleave(aq), 2, axis=1)
    az0, az1 = jnp.split(az, 2, axis=1)
    dk0, dk1 = jnp.split(dk, 2, axis=1)
    groups = [("ga", ga, "plain", 0), ("gd", gd, "plain", 0), ("dq", dq, "conv_q", 0),
              ("dk0", dk0, "conv_k", DN_KW), ("aq0", aq0, "plain", 0),
              ("ak", _deinterleave(ak), "plain", 0),
              ("av", av, "plain", 0), ("dk1", dk1, "conv_k", DN_KW + DN_KW // 2),
              ("dz", dz, "plain", 0), ("dv", dv, "conv_v", 2 * DN_KW), ("aq1", aq1, "plain", 0),
              ("az0", az0, "plain", 0), ("az1", az1, "plain", 0)]
    offsets, off = {}, 0
    pieces = [[] for _ in range(sum(g.shape[1] for _, g, _, _ in groups) // PROJ_TN)]
    for name, g, kind, ch in groups:
        offsets[name] = off
        width = g.shape[1]
        tile, local = divmod(off, PROJ_TN)
        assert local + width <= PROJ_TN
        if kind == "plain":
            last = pieces[tile][-1] if pieces[tile] else None
            if last is not None and last[2] == "plain" and last[0] + last[1] == local:
                pieces[tile][-1] = (last[0], last[1] + width, "plain", 0)
            else:
                pieces[tile].append((local, width, "plain", 0))
        else:
            for k in range(0, width, CONV_PIECE):
                pieces[tile].append((local + k, CONV_PIECE, kind, ch + k))
        off += width
    main = jnp.concatenate([g.astype(BF16) for _, g, _, _ in groups], axis=1)
    small = jnp.pad(jnp.concatenate([db, da], axis=1), ((0, 0), (0, LANES - 4 * DN_HEADS)))
    return main, small.astype(BF16), offsets, pieces


def _proj_kernel(x_ref, xp_ref, xn_ref, g_ref, w_ref, ws_ref, cw_ref, o_ref, os_ref, h_ref,
                 *, tiles_per_seq, pieces):
    i, j = pl.program_id(0), pl.program_id(1)
    tm = x_ref.shape[0]
    ext = tm + 2 * CONV_HALO
    half = CONV_K // 2

    def normed(x):
        r = lax.rsqrt(jnp.mean(x * x, axis=-1, keepdims=True) + EPS)
        return (x * r * g_ref[...]).astype(BF16)

    @pl.when(j == 0)
    def _():
        t = i % tiles_per_seq
        h = normed(x_ref[...])
        zero = jnp.zeros((CONV_HALO, x_ref.shape[1]), BF16)
        h_ref[0:CONV_HALO, :] = jnp.where(t == 0, zero, normed(xp_ref[...]))
        h_ref[CONV_HALO:CONV_HALO + tm, :] = h
        h_ref[CONV_HALO + tm:, :] = jnp.where(t == tiles_per_seq - 1, zero, normed(xn_ref[...]))
        os_ref[...] = jnp.dot(h, ws_ref[...], preferred_element_type=F32)

    def tile(tile_pieces):
        for start, width, kind, ch in tile_pieces:
            cols = slice(start, start + width)
            if kind == "plain":
                o_ref[:, cols] = jnp.dot(h_ref[CONV_HALO:CONV_HALO + tm, :], w_ref[:, cols],
                                         preferred_element_type=F32).astype(BF16)
                continue
            r = jnp.dot(h_ref[...], w_ref[:, cols], preferred_element_type=F32)
            taps = cw_ref[:, ch:ch + width]
            acc = r[CONV_HALO:CONV_HALO + tm] * taps[half:half + 1]
            for t in range(CONV_K):
                if t != half:
                    shifted = pltpu.roll(r, (half - t) % ext, axis=0)
                    acc = acc + shifted[CONV_HALO:CONV_HALO + tm] * taps[t:t + 1]
            y = acc * jax.nn.sigmoid(acc)
            for hh in range(width // DN_DK):
                yh = y[:, hh * DN_DK:(hh + 1) * DN_DK]
                if kind != "conv_v":
                    inv = lax.rsqrt(jnp.sum(yh * yh, axis=-1, keepdims=True) + EPS)
                    yh = yh * (inv * DN_DK ** -0.5 if kind == "conv_q" else inv)
                o_ref[:, start + hh * DN_DK:start + (hh + 1) * DN_DK] = yh.astype(BF16)

    for jj, tile_pieces in enumerate(pieces):
        pl.when(j == jj)(functools.partial(tile, tile_pieces))


def _proj(x2, g, w_main, w_small, conv_w8, pieces, seq, tm=1024):
    n, d = x2.shape
    main_w = w_main.shape[1]
    tm = min(tm, seq)
    hb = tm // CONV_HALO
    last_hb = n // CONV_HALO - 1
    kern = functools.partial(_proj_kernel, tiles_per_seq=seq // tm, pieces=pieces)
    return pl.pallas_call(
        kern,
        grid=(n // tm, main_w // PROJ_TN),
        in_specs=[
            pl.BlockSpec((tm, d), lambda i, j: (i, 0)),
            pl.BlockSpec((CONV_HALO, d), lambda i, j: (jnp.maximum(i * hb - 1, 0), 0)),
            pl.BlockSpec((CONV_HALO, d), lambda i, j: (jnp.minimum((i + 1) * hb, last_hb), 0)),
            pl.BlockSpec((1, d), lambda i, j: (0, 0)),
            pl.BlockSpec((d, PROJ_TN), lambda i, j: (0, j)),
            pl.BlockSpec((d, LANES), lambda i, j: (0, 0)),
            pl.BlockSpec(conv_w8.shape, lambda i, j: (0, 0)),
        ],
        out_specs=[
            pl.BlockSpec((tm, PROJ_TN), lambda i, j: (i, j)),
            pl.BlockSpec((tm, LANES), lambda i, j: (i, 0)),
        ],
        out_shape=[
            jax.ShapeDtypeStruct((n, main_w), BF16),
            jax.ShapeDtypeStruct((n, LANES), F32),
        ],
        scratch_shapes=[pltpu.VMEM((tm + 2 * CONV_HALO, d), BF16)],
        compiler_params=pltpu.CompilerParams(dimension_semantics=("parallel", "arbitrary")),
        name="proj",
    )(x2, x2, x2, g, w_main, w_small, conv_w8)


def _norm_rope(xh, w, cos, sin):
    r = lax.rsqrt(jnp.mean(xh * xh, axis=-1, keepdims=True) + EPS)
    xn = xh * r * w
    return xn * cos + pltpu.roll(xn, HEAD_DIM // 2, axis=1) * sin


ATT_TQ = 1024
ATT_TK = 512
VT_ROWS = HEAD_DIM + 16


def _aprep_kernel(q0_ref, q1_ref, k_ref, v_ref, cos_ref, sin_ref, wq_ref, wk_ref,
                  qt_ref, kr_ref, vt_ref):
    cos, sin = cos_ref[...], sin_ref[...]
    scale = HEAD_DIM ** -0.5
    tm = k_ref.shape[0]
    tk = vt_ref.shape[-1]
    for h, q_ref in enumerate((q0_ref, q1_ref)):
        sl = slice(h * HEAD_DIM, (h + 1) * HEAD_DIM)
        kr_ref[:, sl] = _norm_rope(k_ref[:, sl].astype(F32), wk_ref[...], cos, sin).astype(BF16)
        for c in range(tm // tk):
            vt_ref[h, c, 0:HEAD_DIM, :] = v_ref[c * tk:(c + 1) * tk, sl].astype(F32).T.astype(BF16)
            vt_ref[h, c, HEAD_DIM:, :] = jnp.ones((VT_ROWS - HEAD_DIM, tk), BF16)
        for g in range(GROUP):
            qs = slice(g * HEAD_DIM, (g + 1) * HEAD_DIM)
            qr = _norm_rope(q_ref[:, qs].astype(F32), wq_ref[...], cos, sin) * scale
            qt_ref[h, :, g * tm:(g + 1) * tm] = qr.T.astype(BF16)


def _aprep(proj, cos, sin, wq, wk, offs, batch, seq, tm, tk):
    assert ATT_KV_HEADS == 2
    n = proj.shape[0]
    t = seq // tm
    cpt = tm // tk
    gw = GROUP * HEAD_DIM
    return pl.pallas_call(
        _aprep_kernel,
        grid=(batch, t),
        in_specs=[
            pl.BlockSpec((tm, gw), lambda b, i: (b * t + i, offs["aq0"] // gw)),
            pl.BlockSpec((tm, gw), lambda b, i: (b * t + i, offs["aq1"] // gw)),
            pl.BlockSpec((tm, KV_W), lambda b, i: (b * t + i, offs["ak"] // KV_W)),
            pl.BlockSpec((tm, KV_W), lambda b, i: (b * t + i, offs["av"] // KV_W)),
            pl.BlockSpec((tm, HEAD_DIM), lambda b, i: (i, 0)),
            pl.BlockSpec((tm, HEAD_DIM), lambda b, i: (i, 0)),
            pl.BlockSpec((1, HEAD_DIM), lambda b, i: (0, 0)),
            pl.BlockSpec((1, HEAD_DIM), lambda b, i: (0, 0)),
        ],
        out_specs=[
            pl.BlockSpec((None, None, ATT_KV_HEADS, HEAD_DIM, GROUP * tm),
                         lambda b, i: (b, i, 0, 0, 0)),
            pl.BlockSpec((tm, KV_W), lambda b, i: (b * t + i, 0)),
            pl.BlockSpec((None, ATT_KV_HEADS, cpt, VT_ROWS, tk), lambda b, i: (b, 0, i, 0, 0)),
        ],
        out_shape=[
            jax.ShapeDtypeStruct((batch, t, ATT_KV_HEADS, HEAD_DIM, GROUP * tm), BF16),
            jax.ShapeDtypeStruct((n, KV_W), BF16),
            jax.ShapeDtypeStruct((batch, ATT_KV_HEADS, seq // tk, VT_ROWS, tk), BF16),
        ],
        compiler_params=pltpu.CompilerParams(dimension_semantics=("parallel", "parallel")),
        name="aprep",
    )(proj, proj, proj, proj, cos, sin, wq, wk)


def _attn_kernel(qt_ref, z_ref, k_ref, vt_ref, o_ref, acc_ref, s0_ref, s1_ref, p0_ref, p1_ref,
                 *, tq, tk, seq):
    cols = GROUP * tq
    n_chunks = seq // tk
    qt = qt_ref[...]

    def scores(c, s_ref):
        start = pl.multiple_of(c * tk, tk)
        s = jnp.dot(k_ref[pl.ds(start, tk), :], qt, preferred_element_type=F32)
        s_ref[...] = s
        return jnp.max(s, axis=0, keepdims=True)

    def softmax(s_ref, p_ref, m, smax):
        m_new = jnp.maximum(m, smax)
        p_ref[...] = jnp.exp((s_ref[...] - m_new).astype(BF16))
        return m_new, jnp.exp(m - m_new)

    def accumulate(c, p_ref, alpha, init=False):
        pv = jnp.dot(vt_ref[c], p_ref[...], preferred_element_type=F32)
        acc_ref[...] = pv if init else alpha * acc_ref[...] + pv

    def pair(c, carry, first, last):
        m, alpha_prev, smax0 = carry
        m, alpha0 = softmax(s0_ref, p0_ref, m, smax0)
        smax1 = scores(c + 1, s1_ref)
        if not first:
            accumulate(c - 1, p1_ref, alpha_prev)
        m, alpha1 = softmax(s1_ref, p1_ref, m, smax1)
        if not last:
            smax0 = scores(c + 2, s0_ref)
        accumulate(c, p0_ref, alpha0, init=first)
        return m, alpha1, smax0

    n_pairs = n_chunks // 2
    smax = scores(0, s0_ref)
    carry = (jnp.full((1, cols), -jnp.inf, F32), jnp.ones((1, cols), F32), smax)
    carry = pair(0, carry, True, n_pairs == 1)
    if n_pairs > 1:
        carry = lax.fori_loop(1, n_pairs - 1, lambda j, cr: pair(2 * j, cr, False, False), carry)
        carry = pair(n_chunks - 2, carry, False, True)
    accumulate(n_chunks - 1, p1_ref, carry[1])
    acc = acc_ref[...]
    o_t = acc[:HEAD_DIM] / acc[HEAD_DIM:HEAD_DIM + 1]
    for g in range(GROUP):
        sl = slice(g * HEAD_DIM, (g + 1) * HEAD_DIM)
        z = z_ref[:, sl].astype(F32)
        o_ref[:, sl] = (o_t[:, g * tq:(g + 1) * tq].T * (z * jax.nn.sigmoid(z))).astype(BF16)


def _attention(proj, qt, k_rot, vt, offs, batch, seq, tq, tk):
    n = proj.shape[0]
    gw = GROUP * HEAD_DIM
    t = seq // tq
    cols = GROUP * tq
    assert (seq // tk) % 2 == 0 and offs["az1"] == offs["az0"] + gw
    kern = functools.partial(_attn_kernel, tq=tq, tk=tk, seq=seq)
    return pl.pallas_call(
        kern,
        grid=(batch, ATT_KV_HEADS, t),
        in_specs=[
            pl.BlockSpec((None, None, None, HEAD_DIM, cols), lambda b, h, i: (b, i, h, 0, 0)),
            pl.BlockSpec((tq, gw), lambda b, h, i: (b * t + i, offs["az0"] // gw + h)),
            pl.BlockSpec((seq, HEAD_DIM), lambda b, h, i: (b, h)),
            pl.BlockSpec((None, None, seq // tk, VT_ROWS, tk), lambda b, h, i: (b, h, 0, 0, 0)),
        ],
        out_specs=pl.BlockSpec((tq, gw), lambda b, h, i: (b * t + i, h)),
        out_shape=jax.ShapeDtypeStruct((n, ATT_W), BF16),
        scratch_shapes=[pltpu.VMEM((VT_ROWS, cols), F32),
                        pltpu.VMEM((tk, cols), F32), pltpu.VMEM((tk, cols), F32),
                        pltpu.VMEM((tk, cols), BF16), pltpu.VMEM((tk, cols), BF16)],
        compiler_params=pltpu.CompilerParams(
            dimension_semantics=("parallel", "parallel", "arbitrary")),
        name="attn",
    )(qt, proj, k_rot, vt)


HEADS_PER_TILE = 4
CHUNK_SHIFT = CHUNK.bit_length() - 1
assert CHUNK == 1 << CHUNK_SHIFT and DN_DK & (DN_DK - 1) == 0
CAT_W = HEADS_PER_TILE * CHUNK
N_TILES = DN_HEADS // HEADS_PER_TILE


def _split3(x):
    p1 = x.astype(BF16)
    r1 = x - p1.astype(F32)
    p2 = r1.astype(BF16)
    p3 = (r1 - p2.astype(F32)).astype(BF16)
    return p1, p2, p3


def _softplus(x):
    return jnp.maximum(x, 0.0) + jnp.log(1.0 + jnp.exp(-jnp.abs(x)))


def _dot(a, b):
    return jnp.dot(a, b, preferred_element_type=F32)


def _dchunk_kernel(q0_ref, q1_ref, k0_ref, k1_ref, v0_ref, v1_ref, c_ref, r_ref, pc_ref, pr_ref,
                   a1_ref, a2_ref, au_ref, ae_ref, *, cpb):
    C, W, T = CHUNK, CAT_W, HEADS_PER_TILE
    rows = cpb * C

    def cum_matrix(n, lower):
        i = lax.broadcasted_iota(jnp.int32, (n, n), 0)
        j = lax.broadcasted_iota(jnp.int32, (n, n), 1)
        same = (i >> CHUNK_SHIFT) == (j >> CHUNK_SHIFT)
        return (same & ((i >= j) if lower else (i <= j))).astype(BF16)

    col_lower, col_upper = cum_matrix(rows, True), cum_matrix(rows, False)
    row_lower, row_upper = cum_matrix(W, True), cum_matrix(W, False)

    blk = c_ref[...]
    beta_all = jax.nn.sigmoid(blk)
    g_parts = _split3(-jnp.exp(pc_ref[0:1, :]) * _softplus(blk + pc_ref[1:2, :]))
    gcol = (sum(_dot(col_lower, p) for p in g_parts),
            sum(_dot(col_upper, p) for p in g_parts))
    blk_r = r_ref[...]
    g_r = (-jnp.exp(pr_ref[0][None]) * _softplus(blk_r + pr_ref[1][None])).reshape(cpb * 8, W)
    gr_parts = _split3(g_r)
    grow = (sum(_dot(p, row_upper) for p in gr_parts),
            sum(_dot(p, row_lower) for p in gr_parts))

    ri = lax.broadcasted_iota(jnp.int32, (C, W), 0)
    li = lax.broadcasted_iota(jnp.int32, (C, W), 1)
    ci = li & (C - 1)
    lb = li >> CHUNK_SHIFT
    incl = (ri >= ci, ri <= ci)
    offdiag = ri != ci
    eye = (ri == ci).astype(F32)
    same = {1 << lg: (ri >> lg) == (ci >> lg) for lg in range(1, CHUNK_SHIFT)}
    level_masks = [same[2 * s] & ~same[s] for s in sorted(same)[:-1]] + [~same[C // 2]]
    tile_sel = [lb == t for t in range(T)]
    head_of_lane = lax.broadcasted_iota(jnp.int32, (C, T * DN_DK), 1) >> (DN_DK.bit_length() - 1)

    def block_diag(y):
        zero = jnp.zeros_like(y)
        return jnp.concatenate([jnp.where(tile_sel[t], y, zero) for t in range(T)], axis=0)

    def cat_bcast(cols):
        out = jnp.broadcast_to(cols[T - 1], (C, W))
        for t in range(T - 2, -1, -1):
            out = jnp.where(tile_sel[t], jnp.broadcast_to(cols[t], (C, W)), out)
        return out

    pairs = [(c, t) for c in range(cpb) for t in range(N_TILES)]
    units = [(c, t, d) for c in range(cpb) for t in range(N_TILES) for d in range(2)]

    qk, kk, q4s, k4s, v4s = {}, {}, {}, {}, {}
    for c, t in pairs:
        rs = slice(c * C, (c + 1) * C)
        q4 = (q0_ref, q1_ref)[t][rs, :]
        k4 = (k0_ref, k1_ref)[t][rs, :]
        v4 = (v0_ref, v1_ref)[t][rs, :]
        zero = jnp.zeros_like(k4)
        k_bd = jnp.concatenate([jnp.where(head_of_lane == h, k4, zero) for h in range(T)], axis=0)
        r = lax.dot_general(jnp.concatenate([q4, k4], axis=0), k_bd, (((1,), (1,)), ((), ())),
                            preferred_element_type=F32)
        qk[c, t], kk[c, t] = r[:C], r[C:]
        q4s[c, t], k4s[c, t], v4s[c, t] = q4, k4, v4

    lmat, amat, xinv = {}, {}, {}
    for c, t, d in units:
        rs = slice(c * C, (c + 1) * C)
        base = d * DN_HEADS + t * T
        gc = cat_bcast([gcol[d][rs, 2 * DN_HEADS + base + h:2 * DN_HEADS + base + h + 1]
                        for h in range(T)])
        beta = cat_bcast([beta_all[rs, base + h:base + h + 1] for h in range(T)])
        row = c * 8 + d * N_TILES + t
        gr = jnp.broadcast_to(grow[d][row:row + 1, :], (C, W))
        decay = jnp.exp(jnp.where(incl[d], gc - gr, -1e30))
        l = jnp.where(offdiag, beta * kk[c, t] * decay, 0.0)
        lmat[c, t, d] = l
        amat[c, t, d] = qk[c, t] * decay
        xinv[c, t, d] = eye - jnp.where(same[2], l, 0.0)

    for msk in level_masks:
        ys, xbs = {}, {}
        for u in units:
            e = jnp.where(msk, lmat[u], 0.0).astype(BF16)
            xbs[u] = xinv[u].astype(BF16)
            ys[u] = _dot(e, block_diag(xbs[u]))
        for u in units:
            xinv[u] = xinv[u] - _dot(xbs[u], block_diag(ys[u].astype(BF16)))

    for c, t, d in units:
        rs = slice(c * C, (c + 1) * C)
        kd_rows, rhs_rows = [], []
        for h in range(T):
            head = t * T + h
            col = d * DN_HEADS + head
            hs = slice(h * DN_DK, (h + 1) * DN_DK)
            out_s = slice(head * DN_DK, (head + 1) * DN_DK)
            gc = jnp.broadcast_to(gcol[d][rs, 2 * DN_HEADS + col:2 * DN_HEADS + col + 1],
                                  (C, DN_DK))
            beta = jnp.broadcast_to(beta_all[rs, col:col + 1], (C, DN_DK))
            g_last = gc[C - 1:C, :] if d == 0 else gc[0:1, :]
            eg = jnp.exp(gc)
            kf = k4s[c, t][:, hs].astype(F32)
            rhs = jnp.concatenate([kf * (beta * eg), v4s[c, t][:, hs].astype(F32) * beta],
                                  axis=1).astype(BF16)
            zero = jnp.zeros_like(rhs)
            rhs_rows.append(jnp.concatenate([rhs if hh == h else zero for hh in range(T)], axis=1))
            kd_rows.append(kf * jnp.exp(g_last - gc))
            a1_ref[c, d, C:2 * C, out_s] = (q4s[c, t][:, hs].astype(F32) * eg).astype(BF16)
            ae_ref[c, d, :, out_s] = jnp.exp(g_last)
        wu = _dot(xinv[c, t, d].astype(BF16), jnp.concatenate(rhs_rows, axis=0))
        for h in range(T):
            head = t * T + h
            out_s = slice(head * DN_DK, (head + 1) * DN_DK)
            a1_ref[c, d, 0:C, out_s] = wu[:, 2 * h * DN_DK:(2 * h + 1) * DN_DK].astype(BF16)
            au_ref[c, d, :, out_s] = wu[:, (2 * h + 1) * DN_DK:(2 * h + 2) * DN_DK].astype(BF16)
        cat_s = slice(t * W, (t + 1) * W)
        a2_ref[c, d, 0:C, cat_s] = amat[c, t, d].astype(BF16)
        a2_ref[c, d, C:C + DN_DK, cat_s] = jnp.concatenate(kd_rows, axis=0).T.astype(BF16)


def _dchunk(proj, small, small_r, par_col, par_row, offs, batch, seq, cpb=8):
    assert N_TILES == 2
    nc = seq // CHUNK
    steps = nc // cpb
    tw = HEADS_PER_TILE * DN_DK
    kern = functools.partial(_dchunk_kernel, cpb=cpb)
    out5 = lambda b, i: (b, i, 0, 0, 0)
    tile_cols = [offs["dq"], offs["dq"] + tw, offs["dk0"], offs["dk1"], offs["dv"], offs["dv"] + tw]
    qkv_specs = [pl.BlockSpec((cpb * CHUNK, tw), functools.partial(
        lambda b, i, col: (b * steps + i, col), col=c // tw)) for c in tile_cols]
    return pl.pallas_call(
        kern,
        grid=(batch, steps),
        in_specs=qkv_specs + [
            pl.BlockSpec((cpb * CHUNK, LANES), lambda b, i: (b * steps + i, 0)),
            pl.BlockSpec((cpb, 8, CAT_W), lambda b, i: (b * steps + i, 0, 0)),
            pl.BlockSpec((8, LANES), lambda b, i: (0, 0)),
            pl.BlockSpec((2, 8, CAT_W), lambda b, i: (0, 0, 0)),
        ],
        out_specs=[
            pl.BlockSpec((None, cpb, 2, 2 * CHUNK, DN_KW), out5),
            pl.BlockSpec((None, cpb, 2, CHUNK + DN_DK, DN_HEADS * CHUNK), out5),
            pl.BlockSpec((None, cpb, 2, CHUNK, DN_VW), out5),
            pl.BlockSpec((None, cpb, 2, 1, DN_VW), out5),
        ],
        out_shape=[
            jax.ShapeDtypeStruct((batch, nc, 2, 2 * CHUNK, DN_KW), BF16),
            jax.ShapeDtypeStruct((batch, nc, 2, CHUNK + DN_DK, DN_HEADS * CHUNK), BF16),
            jax.ShapeDtypeStruct((batch, nc, 2, CHUNK, DN_VW), BF16),
            jax.ShapeDtypeStruct((batch, nc, 2, 1, DN_VW), F32),
        ],
        compiler_params=pltpu.CompilerParams(dimension_semantics=("parallel", "parallel")),
        name="dchunk",
    )(proj, proj, proj, proj, proj, proj, small, small_r, par_col, par_row)


def _dscan_kernel(a1f_ref, a1b_ref, a2f_ref, a2b_ref, auf_ref, aub_ref, aef_ref, aeb_ref,
                  of_ref, ob_ref, s_ref, *, batch, cps):
    C = CHUNK
    PW = 2 * DN_DV
    n_pairs = DN_HEADS // 2

    @pl.when(pl.program_id(0) == 0)
    def _():
        s_ref[...] = jnp.zeros(s_ref.shape, F32)

    first = lax.broadcasted_iota(jnp.int32, (DN_DK, PW), 1) < DN_DV
    first_c = lax.broadcasted_iota(jnp.int32, (C, PW), 1) < DN_DV
    refs = ((a1f_ref, a2f_ref, auf_ref, aef_ref, of_ref), (a1b_ref, a2b_ref, aub_ref, aeb_ref, ob_ref))
    units = [(b, d, p) for b in range(batch) for d in range(2) for p in range(n_pairs)]
    sidx = lambda u: (u[0] * 2 + u[1]) * n_pairs + u[2]

    states = {u: s_ref[sidx(u)] for u in units}
    for step in range(cps):
        r1, r2 = {}, {}
        cc = (step, cps - 1 - step)
        for u in units:
            b, d, p = u
            r1[u] = _dot(refs[d][0][b, cc[d], :, p * PW:(p + 1) * PW], states[u].astype(BF16))
        for u in units:
            b, d, p = u
            vb = (refs[d][2][b, cc[d], :, p * PW:(p + 1) * PW].astype(F32) - r1[u][:C]).astype(BF16)
            zero = jnp.zeros_like(vb)
            v_bd = jnp.concatenate([jnp.where(first_c, vb, zero), jnp.where(first_c, zero, vb)],
                                   axis=0)
            r2[u] = _dot(refs[d][1][b, cc[d], :, p * 2 * C:(p + 1) * 2 * C], v_bd)
        for u in units:
            b, d, p = u
            refs[d][4][b, cc[d], :, p * PW:(p + 1) * PW] = (r1[u][C:] + r2[u][:C]).astype(BF16)
            e = refs[d][3][b, cc[d], 0:1, p * PW:(p + 1) * PW]
            upd = r2[u][C:]
            s = states[u]
            states[u] = jnp.concatenate(
                [s[:DN_DK] * e + jnp.where(first, upd, 0.0),
                 s[DN_DK:] * e + jnp.where(first, 0.0, upd)], axis=0)
    for u in units:
        s_ref[sidx(u)] = states[u]


def _dscan(a1, a2, au, ae, batch, seq, cps=8):
    nc = seq // CHUNK
    steps = nc // cps
    kern = functools.partial(_dscan_kernel, batch=batch, cps=cps)
    fwd = lambda i: (0, i, 0, 0, 0)
    bwd = lambda i: (0, steps - 1 - i, 1, 0, 0)

    def spec(arr, imap):
        return pl.BlockSpec((batch, cps, None) + arr.shape[3:], imap)

    o_spec = lambda imap: pl.BlockSpec((batch, cps, CHUNK, DN_VW), imap)
    o_shape = jax.ShapeDtypeStruct((batch, nc, CHUNK, DN_VW), BF16)
    return pl.pallas_call(
        kern,
        grid=(steps,),
        in_specs=[spec(a1, fwd), spec(a1, bwd), spec(a2, fwd), spec(a2, bwd),
                  spec(au, fwd), spec(au, bwd), spec(ae, fwd), spec(ae, bwd)],
        out_specs=[o_spec(lambda i: (0, i, 0, 0)), o_spec(lambda i: (0, steps - 1 - i, 0, 0))],
        out_shape=[o_shape, o_shape],
        scratch_shapes=[pltpu.VMEM((batch * 2 * (DN_HEADS // 2), 2 * DN_DK, 2 * DN_DV), F32)],
        compiler_params=pltpu.CompilerParams(dimension_semantics=("arbitrary",)),
        name="dscan",
    )(a1, a1, a2, a2, au, au, ae, ae)


def _post_kernel(oa_ref, of_ref, ob_ref, dz_ref, ga_ref, gd_ref, x_ref, p_ref,
                 wa_ref, wd_ref, wo_ref, wg_ref, wp_ref, nd_ref, npost_ref, nple_ref, o_ref):
    def rms(t, w):
        return t * lax.rsqrt(jnp.mean(t * t, axis=-1, keepdims=True) + EPS) * w

    nd = nd_ref[...]
    parts = []
    for h in range(DN_HEADS):
        sl = slice(h * DN_DV, (h + 1) * DN_DV)
        od = of_ref[:, sl].astype(F32) + ob_ref[:, sl].astype(F32)
        z = dz_ref[:, sl].astype(F32)
        parts.append((rms(od, nd) * (z * jax.nn.sigmoid(z))).astype(BF16))
    y_dn = jnp.dot(jnp.concatenate(parts, axis=1), wd_ref[...], preferred_element_type=F32)
    y_att = jnp.dot(oa_ref[...], wa_ref[...], preferred_element_type=F32)
    merged = (jax.nn.sigmoid(ga_ref[...].astype(F32)) * y_att
              + jax.nn.sigmoid(gd_ref[...].astype(F32)) * y_dn)
    mix = jnp.dot(merged.astype(BF16), wo_ref[...], preferred_element_type=F32)
    x1 = x_ref[...] + rms(mix, npost_ref[...])
    gate = jax.nn.sigmoid(jnp.dot(x1.astype(BF16), wg_ref[...], preferred_element_type=F32))
    e = jnp.dot(p_ref[...].astype(BF16), wp_ref[...], preferred_element_type=F32)
    o_ref[...] = x1 + rms(gate * e, nple_ref[...])


def _post(o_att, o_f, o_b, proj, x2, p2, wa, wd, wo, wg, wp, nd, npost, nple, offs, tm=512):
    n, d = x2.shape
    ple = p2.shape[1]
    row = lambda i: (i, 0)
    const = lambda i: (0, 0)
    return pl.pallas_call(
        _post_kernel,
        grid=(n // tm,),
        in_specs=[
            pl.BlockSpec((tm, ATT_W), row),
            pl.BlockSpec((tm, DN_VW), row),
            pl.BlockSpec((tm, DN_VW), row),
            pl.BlockSpec((tm, DN_VW), lambda i: (i, offs["dz"] // DN_VW)),
            pl.BlockSpec((tm, d), lambda i: (i, offs["ga"] // d)),
            pl.BlockSpec((tm, d), lambda i: (i, offs["gd"] // d)),
            pl.BlockSpec((tm, d), row),
            pl.BlockSpec((tm, ple), row),
            pl.BlockSpec((ATT_W, d), const),
            pl.BlockSpec((DN_VW, d), const),
            pl.BlockSpec((d, d), const),
            pl.BlockSpec((d, d), const),
            pl.BlockSpec((ple, d), const),
            pl.BlockSpec((1, DN_DV), const),
            pl.BlockSpec((1, d), const),
            pl.BlockSpec((1, d), const),
        ],
        out_specs=pl.BlockSpec((tm, d), row),
        out_shape=jax.ShapeDtypeStruct((n, d), F32),
        compiler_params=pltpu.CompilerParams(dimension_semantics=("parallel",)),
        name="post",
    )(o_att, o_f, o_b, proj, proj, proj, x2, p2, wa, wd, wo, wg, wp, nd, npost, nple)


def _rope_tables(seq):
    rows = seq // GRID_W
    row = np.broadcast_to(np.arange(rows)[:, None], (rows, GRID_W)).reshape(seq)
    col = np.broadcast_to(np.arange(GRID_W)[None, :], (rows, GRID_W)).reshape(seq)
    n_freq = HEAD_DIM // 4
    inv_freq = np.float32(ROPE_THETA) ** (-np.arange(n_freq, dtype=np.float32) / np.float32(n_freq))
    ang = np.concatenate([row.astype(np.float32)[:, None] * inv_freq,
                          col.astype(np.float32)[:, None] * inv_freq], axis=-1)
    cos, sin = np.cos(ang), np.sin(ang)
    return (jnp.asarray(np.concatenate([cos, cos], axis=-1), F32),
            jnp.asarray(np.concatenate([-sin, sin], axis=-1), F32))


def kernel(x, p, norm_pre, w_in, q_norm, k_norm, conv_w, a_log, dt_bias, dn_norm, w_br_att,
           w_br_dn, w_out, norm_post, w_ple_proj, w_ple_gate, ple_norm):
    batch, seq, d = x.shape
    depth = w_in.shape[0]
    n = batch * seq
    assert seq % GRID_W == 0 and seq % CHUNK == 0 and d % LANES == 0
    cos, sin = _rope_tables(seq)
    x2 = x.reshape(n, d)
    for i in range(depth):
        w_main, w_small, offs, pieces = _regroup_columns(w_in[i])
        conv_w8 = jnp.pad(conv_w[i], ((0, 8 - CONV_K), (0, 0)))
        proj, small = _proj(x2, norm_pre[i][None, :], w_main, w_small, conv_w8, pieces, seq)

        tq, tk = min(ATT_TQ, seq), min(ATT_TK, seq // 2)
        qt, k_rot, vt = _aprep(proj, cos, sin, _deinterleave(q_norm[i])[None, :],
                               _deinterleave(k_norm[i])[None, :], offs, batch, seq, tq, tk)
        o_att = _attention(proj, qt, k_rot, vt, offs, batch, seq, tq, tk)

        nsm = 4 * DN_HEADS
        nck = n // CHUNK
        small_r = small[:, 2 * DN_HEADS:nsm].reshape(nck, CHUNK, 2, N_TILES, HEADS_PER_TILE)
        small_r = small_r.transpose(0, 2, 3, 4, 1).reshape(nck, 2 * N_TILES, CAT_W)
        small_r = jnp.pad(small_r, ((0, 0), (0, 8 - 2 * N_TILES), (0, 0)))
        par = jnp.stack([a_log[i].reshape(-1), dt_bias[i].reshape(-1)])
        par_col = jnp.pad(jnp.concatenate([jnp.zeros_like(par), par], axis=1),
                          ((0, 6), (0, LANES - nsm)))
        par_row = jnp.repeat(par.reshape(2, 2 * N_TILES, HEADS_PER_TILE), CHUNK, axis=2)
        par_row = jnp.pad(par_row, ((0, 0), (0, 8 - 2 * N_TILES), (0, 0)))
        a1, a2, au, ae = _dchunk(proj, small, small_r, par_col, par_row, offs, batch, seq)
        o_f, o_b = _dscan(a1, a2, au, ae, batch, seq)
        o_f, o_b = o_f.reshape(n, DN_VW), o_b.reshape(n, DN_VW)

        x2 = _post(o_att, o_f, o_b, proj, x2, p[i].reshape(n, -1),
                   w_br_att[i].astype(BF16), w_br_dn[i].astype(BF16), w_out[i].astype(BF16),
                   w_ple_gate[i].astype(BF16), w_ple_proj[i].astype(BF16),
                   dn_norm[i][None, :], norm_post[i][None, :], ple_norm[i][None, :], offs)
    return x2.reshape(batch, seq, d)
```

```python
import functools

import numpy as np
import jax
import jax.numpy as jnp
from jax import lax
from jax.experimental import pallas as pl
from jax.experimental.pallas import tpu as pltpu

F32 = jnp.float32
BF16 = jnp.bfloat16

GRID_W = 64
ATT_HEADS = 8
ATT_KV_HEADS = 2
HEAD_DIM = 128
ROPE_THETA = 10000.0
DN_HEADS = 8
DN_DK = 128
DN_DV = 128
CONV_K = 5
CHUNK = 64
EPS = 1e-6

ATT_W = ATT_HEADS * HEAD_DIM
KV_W = ATT_KV_HEADS * HEAD_DIM
DN_KW = DN_HEADS * DN_DK
DN_VW = DN_HEADS * DN_DV
GROUP = ATT_HEADS // ATT_KV_HEADS
LANES = 128


def _deinterleave(t):
    lead = t.shape[:-1]
    t = t.reshape(lead + (-1, HEAD_DIM // 2, 2))
    return jnp.swapaxes(t, -1, -2).reshape(lead + (-1,))


PROJ_TN = 4352
CONV_HALO = 16
CONV_PIECE = 512


def _regroup_columns(w):
    d_model = w.shape[0]
    sizes = (ATT_W, KV_W, KV_W, ATT_W, DN_KW, DN_KW, DN_VW, 2 * DN_HEADS, 2 * DN_HEADS, DN_VW,
             d_model, d_model)
    assert w.shape[1] == sum(sizes)
    aq, ak, av, az, dq, dk, dv, db, da, dz, ga, gd = jnp.split(w, np.cumsum(sizes)[:-1], axis=1)
    aq0, aq1 = jnp.split(_deinterleave(aq), 2, axis=1)
    az0, az1 = jnp.split(az, 2, axis=1)
    dk0, dk1 = jnp.split(dk, 2, axis=1)
    groups = [("ga", ga, "plain", 0), ("gd", gd, "plain", 0), ("dq", dq, "conv_q", 0),
              ("dk0", dk0, "conv_k", DN_KW), ("aq0", aq0, "plain", 0),
              ("ak", _deinterleave(ak), "plain", 0),
              ("av", av, "plain", 0), ("dk1", dk1, "conv_k", DN_KW + DN_KW // 2),
              ("dz", dz, "plain", 0), ("dv", dv, "conv_v", 2 * DN_KW), ("aq1", aq1, "plain", 0),
              ("az0", az0, "plain", 0), ("az1", az1, "plain", 0)]
    offsets, off = {}, 0
    pieces = [[] for _ in range(sum(g.shape[1] for _, g, _, _ in groups) // PROJ_TN)]
    for name, g, kind, ch in groups:
        offsets[name] = off
        width = g.shape[1]
        tile, local = divmod(off, PROJ_TN)
        assert local + width <= PROJ_TN
        if kind == "plain":
            last = pieces[tile][-1] if pieces[tile] else None
            if last is not None and last[2] == "plain" and last[0] + last[1] == local:
                pieces[tile][-1] = (last[0], last[1] + width, "plain", 0)
            else:
                pieces[tile].append((local, width, "plain", 0))
        else:
            for k in range(0, width, CONV_PIECE):
                pieces[tile].append((local + k, CONV_PIECE, kind, ch + k))
        off += width
    main = jnp.concatenate([g.astype(BF16) for _, g, _, _ in groups], axis=1)
    small = jnp.pad(jnp.concatenate([db, da], axis=1), ((0, 0), (0, LANES - 4 * DN_HEADS)))
    return main, small.astype(BF16), offsets, pieces


def _proj_kernel(x_ref, xp_ref, xn_ref, g_ref, w_ref, ws_ref, cw_ref, o_ref, os_ref, h_ref,
                 *, tiles_per_seq, pieces):
    i, j = pl.program_id(0), pl.program_id(1)
    tm = x_ref.shape[0]
    ext = tm + 2 * CONV_HALO
    half = CONV_K // 2

    def normed(x):
        r = lax.rsqrt(jnp.mean(x * x, axis=-1, keepdims=True) + EPS)
        return (x * r * g_ref[...]).astype(BF16)

    @pl.when(j == 0)
    def _():
        t = i % tiles_per_seq
        h = normed(x_ref[...])
        zero = jnp.zeros((CONV_HALO, x_ref.shape[1]), BF16)
        h_ref[0:CONV_HALO, :] = jnp.where(t == 0, zero, normed(xp_ref[...]))
        h_ref[CONV_HALO:CONV_HALO + tm, :] = h
        h_ref[CONV_HALO + tm:, :] = jnp.where(t == tiles_per_seq - 1, zero, normed(xn_ref[...]))
        os_ref[...] = jnp.dot(h, ws_ref[...], preferred_element_type=F32)

    def tile(tile_pieces):
        for start, width, kind, ch in tile_pieces:
            cols = slice(start, start + width)
            if kind == "plain":
                o_ref[:, cols] = jnp.dot(h_ref[CONV_HALO:CONV_HALO + tm, :], w_ref[:, cols],
                                         preferred_element_type=F32).astype(BF16)
                continue
            r = jnp.dot(h_ref[...], w_ref[:, cols], preferred_element_type=F32)
            taps = cw_ref[:, ch:ch + width]
            acc = r[CONV_HALO:CONV_HALO + tm] * taps[half:half + 1]
            for t in range(CONV_K):
                if t != half:
                    shifted = pltpu.roll(r, (half - t) % ext, axis=0)
                    acc = acc + shifted[CONV_HALO:CONV_HALO + tm] * taps[t:t + 1]
            y = acc * jax.nn.sigmoid(acc)
            for hh in range(width // DN_DK):
                yh = y[:, hh * DN_DK:(hh + 1) * DN_DK]
                if kind != "conv_v":
                    inv = lax.rsqrt(jnp.sum(yh * yh, axis=-1, keepdims=True) + EPS)
                    yh = yh * (inv * DN_DK ** -0.5 if kind == "conv_q" else inv)
                o_ref[:, start + hh * DN_DK:start + (hh + 1) * DN_DK] = yh.astype(BF16)

    for jj, tile_pieces in enumerate(pieces):
        pl.when(j == jj)(functools.partial(tile, tile_pieces))


def _proj(x2, g, w_main, w_small, conv_w8, pieces, seq, tm=1024):
    n, d = x2.shape
    main_w = w_main.shape[1]
    tm = min(tm, seq)
    hb = tm // CONV_HALO
    last_hb = n // CONV_HALO - 1
    kern = functools.partial(_proj_kernel, tiles_per_seq=seq // tm, pieces=pieces)
    return pl.pallas_call(
        kern,
        grid=(n // tm, main_w // PROJ_TN),
        in_specs=[
            pl.BlockSpec((tm, d), lambda i, j: (i, 0)),
            pl.BlockSpec((CONV_HALO, d), lambda i, j: (jnp.maximum(i * hb - 1, 0), 0)),
            pl.BlockSpec((CONV_HALO, d), lambda i, j: (jnp.minimum((i + 1) * hb, last_hb), 0)),
            pl.BlockSpec((1, d), lambda i, j: (0, 0)),
            pl.BlockSpec((d, PROJ_TN), lambda i, j: (0, j)),
            pl.BlockSpec((d, LANES), lambda i, j: (0, 0)),
            pl.BlockSpec(conv_w8.shape, lambda i, j: (0, 0)),
        ],
        out_specs=[
            pl.BlockSpec((tm, PROJ_TN), lambda i, j: (i, j)),
            pl.BlockSpec((tm, LANES), lambda i, j: (i, 0)),
        ],
        out_shape=[
            jax.ShapeDtypeStruct((n, main_w), BF16),
            jax.ShapeDtypeStruct((n, LANES), F32),
        ],
        scratch_shapes=[pltpu.VMEM((tm + 2 * CONV_HALO, d), BF16)],
        compiler_params=pltpu.CompilerParams(dimension_semantics=("parallel", "arbitrary")),
        name="proj",
    )(x2, x2, x2, g, w_main, w_small, conv_w8)


def _norm_rope(xh, w, cos, sin):
    r = lax.rsqrt(jnp.mean(xh * xh, axis=-1, keepdims=True) + EPS)
    xn = xh * r * w
    return xn * cos + pltpu.roll(xn, HEAD_DIM // 2, axis=1) * sin


ATT_TQ = 1024
ATT_TK = 512
VT_ROWS = HEAD_DIM + 16


def _aprep_kernel(q0_ref, q1_ref, k_ref, v_ref, cos_ref, sin_ref, wq_ref, wk_ref,
                  qt_ref, kr_ref, vt_ref):
    cos, sin = cos_ref[...], sin_ref[...]
    scale = HEAD_DIM ** -0.5
    tm = k_ref.shape[0]
    tk = vt_ref.shape[-1]
    ident = (lax.broadcasted_iota(jnp.int32, (HEAD_DIM, HEAD_DIM), 0)
             == lax.broadcasted_iota(jnp.int32, (HEAD_DIM, HEAD_DIM), 1)).astype(BF16)

    def transpose_bf16(a):
        return lax.dot_general(ident, a.astype(BF16), (((1,), (1,)), ((), ())),
                               preferred_element_type=F32).astype(BF16)
    for h, q_ref in enumerate((q0_ref, q1_ref)):
        sl = slice(h * HEAD_DIM, (h + 1) * HEAD_DIM)
        kr_ref[:, sl] = _norm_rope(k_ref[:, sl].astype(F32), wk_ref[...], cos, sin).astype(BF16)
        for c in range(tm // tk):
            vt_ref[h, c, 0:HEAD_DIM, :] = transpose_bf16(v_ref[c * tk:(c + 1) * tk, sl])
            vt_ref[h, c, HEAD_DIM:, :] = jnp.ones((VT_ROWS - HEAD_DIM, tk), BF16)
        for g in range(GROUP):
            qs = slice(g * HEAD_DIM, (g + 1) * HEAD_DIM)
            qr = _norm_rope(q_ref[:, qs].astype(F32), wq_ref[...], cos, sin) * scale
            qt_ref[h, :, g * tm:(g + 1) * tm] = transpose_bf16(qr)


def _aprep(proj, cos, sin, wq, wk, offs, batch, seq, tm, tk):
    assert ATT_KV_HEADS == 2
    n = proj.shape[0]
    t = seq // tm
    cpt = tm // tk
    gw = GROUP * HEAD_DIM
    return pl.pallas_call(
        _aprep_kernel,
        grid=(batch, t),
        in_specs=[
            pl.BlockSpec((tm, gw), lambda b, i: (b * t + i, offs["aq0"] // gw)),
            pl.BlockSpec((tm, gw), lambda b, i: (b * t + i, offs["aq1"] // gw)),
            pl.BlockSpec((tm, KV_W), lambda b, i: (b * t + i, offs["ak"] // KV_W)),
            pl.BlockSpec((tm, KV_W), lambda b, i: (b * t + i, offs["av"] // KV_W)),
            pl.BlockSpec((tm, HEAD_DIM), lambda b, i: (i, 0)),
            pl.BlockSpec((tm, HEAD_DIM), lambda b, i: (i, 0)),
            pl.BlockSpec((1, HEAD_DIM), lambda b, i: (0, 0)),
            pl.BlockSpec((1, HEAD_DIM), lambda b, i: (0, 0)),
        ],
        out_specs=[
            pl.BlockSpec((None, None, ATT_KV_HEADS, HEAD_DIM, GROUP * tm),
                         lambda b, i: (b, i, 0, 0, 0)),
            pl.BlockSpec((tm, KV_W), lambda b, i: (b * t + i, 0)),
            pl.BlockSpec((None, ATT_KV_HEADS, cpt, VT_ROWS, tk), lambda b, i: (b, 0, i, 0, 0)),
        ],
        out_shape=[
            jax.ShapeDtypeStruct((batch, t, ATT_KV_HEADS, HEAD_DIM, GROUP * tm), BF16),
            jax.ShapeDtypeStruct((n, KV_W), BF16),
            jax.ShapeDtypeStruct((batch, ATT_KV_HEADS, seq // tk, VT_ROWS, tk), BF16),
        ],
        compiler_params=pltpu.CompilerParams(dimension_semantics=("parallel", "parallel")),
        name="aprep",
    )(proj, proj, proj, proj, cos, sin, wq, wk)


def _attn_kernel(qt_ref, z_ref, k_ref, vt_ref, o_ref, acc_ref, s0_ref, s1_ref, p0_ref, p1_ref,
                 *, tq, tk, seq):
    cols = GROUP * tq
    n_chunks = seq // tk
    qt = qt_ref[...]

    def scores(c, s_ref):
        start = pl.multiple_of(c * tk, tk)
        s = jnp.dot(k_ref[pl.ds(start, tk), :], qt, preferred_element_type=F32)
        s_ref[...] = s
        return jnp.max(s, axis=0, keepdims=True)

    def softmax(s_ref, p_ref, m, smax):
        m_new = jnp.maximum(m, smax)
        p_ref[...] = jnp.exp((s_ref[...] - m_new).astype(BF16))
        return m_new, jnp.exp(m - m_new)

    def accumulate(c, p_ref, alpha, init=False):
        pv = jnp.dot(vt_ref[c], p_ref[...], preferred_element_type=F32)
        acc_ref[...] = pv if init else alpha * acc_ref[...] + pv

    def pair(c, carry, first, last):
        m, alpha_prev, smax0 = carry
        m, alpha0 = softmax(s0_ref, p0_ref, m, smax0)
        smax1 = scores(c + 1, s1_ref)
        if not first:
            accumulate(c - 1, p1_ref, alpha_prev)
        m, alpha1 = softmax(s1_ref, p1_ref, m, smax1)
        if not last:
            smax0 = scores(c + 2, s0_ref)
        accumulate(c, p0_ref, alpha0, init=first)
        return m, alpha1, smax0

    n_pairs = n_chunks // 2
    smax = scores(0, s0_ref)
    carry = (jnp.full((1, cols), -jnp.inf, F32), jnp.ones((1, cols), F32), smax)
    carry = pair(0, carry, True, n_pairs == 1)
    if n_pairs > 1:
        carry = lax.fori_loop(1, n_pairs - 1, lambda j, cr: pair(2 * j, cr, False, False), carry)
        carry = pair(n_chunks - 2, carry, False, True)
    accumulate(n_chunks - 1, p1_ref, carry[1])
    acc = acc_ref[...]
    o_t = acc[:HEAD_DIM] / acc[HEAD_DIM:HEAD_DIM + 1]
    for g in range(GROUP):
        sl = slice(g * HEAD_DIM, (g + 1) * HEAD_DIM)
        z = z_ref[:, sl].astype(F32)
        o_ref[:, sl] = (o_t[:, g * tq:(g + 1) * tq].T * (z * jax.nn.sigmoid(z))).astype(BF16)


def _attention(proj, qt, k_rot, vt, offs, batch, seq, tq, tk):
    n = proj.shape[0]
    gw = GROUP * HEAD_DIM
    t = seq // tq
    cols = GROUP * tq
    assert (seq // tk) % 2 == 0 and offs["az1"] == offs["az0"] + gw
    kern = functools.partial(_attn_kernel, tq=tq, tk=tk, seq=seq)
    return pl.pallas_call(
        kern,
        grid=(batch, ATT_KV_HEADS, t),
        in_specs=[
            pl.BlockSpec((None, None, None, HEAD_DIM, cols), lambda b, h, i: (b, i, h, 0, 0)),
            pl.BlockSpec((tq, gw), lambda b, h, i: (b * t + i, offs["az0"] // gw + h)),
            pl.BlockSpec((seq, HEAD_DIM), lambda b, h, i: (b, h)),
            pl.BlockSpec((None, None, seq // tk, VT_ROWS, tk), lambda b, h, i: (b, h, 0, 0, 0)),
        ],
        out_specs=pl.BlockSpec((tq, gw), lambda b, h, i: (b * t + i, h)),
        out_shape=jax.ShapeDtypeStruct((n, ATT_W), BF16),
        scratch_shapes=[pltpu.VMEM((VT_ROWS, cols), F32),
                        pltpu.VMEM((tk, cols), F32), pltpu.VMEM((tk, cols), F32),
                        pltpu.VMEM((tk, cols), BF16), pltpu.VMEM((tk, cols), BF16)],
        compiler_params=pltpu.CompilerParams(
            dimension_semantics=("parallel", "parallel", "arbitrary")),
        name="attn",
    )(qt, proj, k_rot, vt)


HEADS_PER_TILE = 4
CHUNK_SHIFT = CHUNK.bit_length() - 1
assert CHUNK == 1 << CHUNK_SHIFT and DN_DK & (DN_DK - 1) == 0
CAT_W = HEADS_PER_TILE * CHUNK
N_TILES = DN_HEADS // HEADS_PER_TILE


def _split3(x):
    p1 = x.astype(BF16)
    r1 = x - p1.astype(F32)
    p2 = r1.astype(BF16)
    p3 = (r1 - p2.astype(F32)).astype(BF16)
    return p1, p2, p3


def _softplus(x):
    return jnp.maximum(x, 0.0) + jnp.log(1.0 + jnp.exp(-jnp.abs(x)))


def _dot(a, b):
    return jnp.dot(a, b, preferred_element_type=F32)


def _dchunk_kernel(q0_ref, q1_ref, k0_ref, k1_ref, v0_ref, v1_ref, c_ref, r_ref, pc_ref, pr_ref,
                   a1_ref, a2_ref, au_ref, ae_ref, *, cpb):
    C, W, T = CHUNK, CAT_W, HEADS_PER_TILE
    rows = cpb * C

    def cum_matrix(n, lower):
        i = lax.broadcasted_iota(jnp.int32, (n, n), 0)
        j = lax.broadcasted_iota(jnp.int32, (n, n), 1)
        same = (i >> CHUNK_SHIFT) == (j >> CHUNK_SHIFT)
        return (same & ((i >= j) if lower else (i <= j))).astype(BF16)

    col_lower, col_upper = cum_matrix(rows, True), cum_matrix(rows, False)
    row_lower, row_upper = cum_matrix(W, True), cum_matrix(W, False)

    blk = c_ref[...]
    beta_all = jax.nn.sigmoid(blk)
    g_parts = _split3(-jnp.exp(pc_ref[0:1, :]) * _softplus(blk + pc_ref[1:2, :]))
    gcol = (sum(_dot(col_lower, p) for p in g_parts),
            sum(_dot(col_upper, p) for p in g_parts))
    blk_r = r_ref[...]
    g_r = (-jnp.exp(pr_ref[0][None]) * _softplus(blk_r + pr_ref[1][None])).reshape(cpb * 8, W)
    gr_parts = _split3(g_r)
    grow = (sum(_dot(p, row_upper) for p in gr_parts),
            sum(_dot(p, row_lower) for p in gr_parts))

    ri = lax.broadcasted_iota(jnp.int32, (C, W), 0)
    li = lax.broadcasted_iota(jnp.int32, (C, W), 1)
    ci = li & (C - 1)
    lb = li >> CHUNK_SHIFT
    incl = (ri >= ci, ri <= ci)
    offdiag = ri != ci
    eye = (ri == ci).astype(F32)
    same = {1 << lg: (ri >> lg) == (ci >> lg) for lg in range(1, CHUNK_SHIFT)}
    level_masks = [same[2 * s] & ~same[s] for s in sorted(same)[:-1]] + [~same[C // 2]]
    tile_sel = [lb == t for t in range(T)]
    head_of_lane = lax.broadcasted_iota(jnp.int32, (C, T * DN_DK), 1) >> (DN_DK.bit_length() - 1)

    def block_diag(y):
        zero = jnp.zeros_like(y)
        return jnp.concatenate([jnp.where(tile_sel[t], y, zero) for t in range(T)], axis=0)

    def cat_bcast(cols):
        out = jnp.broadcast_to(cols[T - 1], (C, W))
        for t in range(T - 2, -1, -1):
            out = jnp.where(tile_sel[t], jnp.broadcast_to(cols[t], (C, W)), out)
        return out

    pairs = [(c, t) for c in range(cpb) for t in range(N_TILES)]
    units = [(c, t, d) for c in range(cpb) for t in range(N_TILES) for d in range(2)]

    qk, kk, q4s, k4s, v4s = {}, {}, {}, {}, {}
    for c, t in pairs:
        rs = slice(c * C, (c + 1) * C)
        q4 = (q0_ref, q1_ref)[t][rs, :]
        k4 = (k0_ref, k1_ref)[t][rs, :]
        v4 = (v0_ref, v1_ref)[t][rs, :]
        zero = jnp.zeros_like(k4)
        k_bd = jnp.concatenate([jnp.where(head_of_lane == h, k4, zero) for h in range(T)], axis=0)
        r = lax.dot_general(jnp.concatenate([q4, k4], axis=0), k_bd, (((1,), (1,)), ((), ())),
                            preferred_element_type=F32)
        qk[c, t], kk[c, t] = r[:C], r[C:]
        q4s[c, t], k4s[c, t], v4s[c, t] = q4, k4, v4

    lmat, amat, xinv = {}, {}, {}
    for c, t, d in units:
        rs = slice(c * C, (c + 1) * C)
        base = d * DN_HEADS + t * T
        gc = cat_bcast([gcol[d][rs, 2 * DN_HEADS + base + h:2 * DN_HEADS + base + h + 1]
                        for h in range(T)])
        beta = cat_bcast([beta_all[rs, base + h:base + h + 1] for h in range(T)])
        row = c * 8 + d * N_TILES + t
        gr = jnp.broadcast_to(grow[d][row:row + 1, :], (C, W))
        decay = jnp.exp(jnp.where(incl[d], gc - gr, -1e30))
        l = jnp.where(offdiag, beta * kk[c, t] * decay, 0.0)
        lmat[c, t, d] = l
        amat[c, t, d] = qk[c, t] * decay
        xinv[c, t, d] = eye - jnp.where(same[2], l, 0.0)

    for msk in level_masks:
        ys, xbs = {}, {}
        for u in units:
            e = jnp.where(msk, lmat[u], 0.0).astype(BF16)
            xbs[u] = xinv[u].astype(BF16)
            ys[u] = _dot(e, block_diag(xbs[u]))
        for u in units:
            xinv[u] = xinv[u] - _dot(xbs[u], block_diag(ys[u].astype(BF16)))

    for c, t, d in units:
        rs = slice(c * C, (c + 1) * C)
        kd_rows, rhs_rows = [], []
        for h in range(T):
            head = t * T + h
            col = d * DN_HEADS + head
            hs = slice(h * DN_DK, (h + 1) * DN_DK)
            out_s = slice(head * DN_DK, (head + 1) * DN_DK)
            gc = jnp.broadcast_to(gcol[d][rs, 2 * DN_HEADS + col:2 * DN_HEADS + col + 1],
                                  (C, DN_DK))
            beta = jnp.broadcast_to(beta_all[rs, col:col + 1], (C, DN_DK))
            g_last = gc[C - 1:C, :] if d == 0 else gc[0:1, :]
            eg = jnp.exp(gc)
            kf = k4s[c, t][:, hs].astype(F32)
            rhs = jnp.concatenate([kf * (beta * eg), v4s[c, t][:, hs].astype(F32) * beta],
                                  axis=1).astype(BF16)
            zero = jnp.zeros_like(rhs)
            rhs_rows.append(jnp.concatenate([rhs if hh == h else zero for hh in range(T)], axis=1))
            kd_rows.append(kf * jnp.exp(g_last - gc))
            a1_ref[c, d, C:2 * C, out_s] = (q4s[c, t][:, hs].astype(F32) * eg).astype(BF16)
            ae_ref[c, d, :, out_s] = jnp.exp(g_last)
        wu = _dot(xinv[c, t, d].astype(BF16), jnp.concatenate(rhs_rows, axis=0))
        for h in range(T):
            head = t * T + h
            out_s = slice(head * DN_DK, (head + 1) * DN_DK)
            a1_ref[c, d, 0:C, out_s] = wu[:, 2 * h * DN_DK:(2 * h + 1) * DN_DK].astype(BF16)
            au_ref[c, d, :, out_s] = wu[:, (2 * h + 1) * DN_DK:(2 * h + 2) * DN_DK].astype(BF16)
        cat_s = slice(t * W, (t + 1) * W)
        a2_ref[c, d, 0:C, cat_s] = amat[c, t, d].astype(BF16)
        a2_ref[c, d, C:C + DN_DK, cat_s] = jnp.concatenate(kd_rows, axis=0).T.astype(BF16)


def _dchunk(proj, small, small_r, par_col, par_row, offs, batch, seq, cpb=4):
    assert N_TILES == 2
    nc = seq // CHUNK
    steps = nc // cpb
    tw = HEADS_PER_TILE * DN_DK
    kern = functools.partial(_dchunk_kernel, cpb=cpb)
    out5 = lambda b, i: (b, i, 0, 0, 0)
    tile_cols = [offs["dq"], offs["dq"] + tw, offs["dk0"], offs["dk1"], offs["dv"], offs["dv"] + tw]
    qkv_specs = [pl.BlockSpec((cpb * CHUNK, tw), functools.partial(
        lambda b, i, col: (b * steps + i, col), col=c // tw)) for c in tile_cols]
    return pl.pallas_call(
        kern,
        grid=(batch, steps),
        in_specs=qkv_specs + [
            pl.BlockSpec((cpb * CHUNK, LANES), lambda b, i: (b * steps + i, 0)),
            pl.BlockSpec((cpb, 8, CAT_W), lambda b, i: (b * steps + i, 0, 0)),
            pl.BlockSpec((8, LANES), lambda b, i: (0, 0)),
            pl.BlockSpec((2, 8, CAT_W), lambda b, i: (0, 0, 0)),
        ],
        out_specs=[
            pl.BlockSpec((None, cpb, 2, 2 * CHUNK, DN_KW), out5),
            pl.BlockSpec((None, cpb, 2, CHUNK + DN_DK, DN_HEADS * CHUNK), out5),
            pl.BlockSpec((None, cpb, 2, CHUNK, DN_VW), out5),
            pl.BlockSpec((None, cpb, 2, 1, DN_VW), out5),
        ],
        out_shape=[
            jax.ShapeDtypeStruct((batch, nc, 2, 2 * CHUNK, DN_KW), BF16),
            jax.ShapeDtypeStruct((batch, nc, 2, CHUNK + DN_DK, DN_HEADS * CHUNK), BF16),
            jax.ShapeDtypeStruct((batch, nc, 2, CHUNK, DN_VW), BF16),
            jax.ShapeDtypeStruct((batch, nc, 2, 1, DN_VW), F32),
        ],
        compiler_params=pltpu.CompilerParams(dimension_semantics=("parallel", "parallel")),
        name="dchunk",
    )(proj, proj, proj, proj, proj, proj, small, small_r, par_col, par_row)


def _dscan_kernel(a1f_ref, a1b_ref, a2f_ref, a2b_ref, auf_ref, aub_ref, aef_ref, aeb_ref,
                  of_ref, ob_ref, s_ref, *, batch, cps):
    C = CHUNK
    PW = 2 * DN_DV
    n_pairs = DN_HEADS // 2

    @pl.when(pl.program_id(0) == 0)
    def _():
        s_ref[...] = jnp.zeros(s_ref.shape, F32)

    first = lax.broadcasted_iota(jnp.int32, (DN_DK, PW), 1) < DN_DV
    first_c = lax.broadcasted_iota(jnp.int32, (C, PW), 1) < DN_DV
    refs = ((a1f_ref, a2f_ref, auf_ref, aef_ref, of_ref), (a1b_ref, a2b_ref, aub_ref, aeb_ref, ob_ref))
    units = [(b, d, p) for b in range(batch) for d in range(2) for p in range(n_pairs)]
    sidx = lambda u: (u[0] * 2 + u[1]) * n_pairs + u[2]

    states = {u: s_ref[sidx(u)] for u in units}
    for step in range(cps):
        r1, r2 = {}, {}
        cc = (step, cps - 1 - step)
        for u in units:
            b, d, p = u
            r1[u] = _dot(refs[d][0][b, cc[d], :, p * PW:(p + 1) * PW], states[u].astype(BF16))
        for u in units:
            b, d, p = u
            vb = (refs[d][2][b, cc[d], :, p * PW:(p + 1) * PW].astype(F32) - r1[u][:C]).astype(BF16)
            zero = jnp.zeros_like(vb)
            v_bd = jnp.concatenate([jnp.where(first_c, vb, zero), jnp.where(first_c, zero, vb)],
                                   axis=0)
            r2[u] = _dot(refs[d][1][b, cc[d], :, p * 2 * C:(p + 1) * 2 * C], v_bd)
        for u in units:
            b, d, p = u
            refs[d][4][b, cc[d], :, p * PW:(p + 1) * PW] = (r1[u][C:] + r2[u][:C]).astype(BF16)
            e = refs[d][3][b, cc[d], 0:1, p * PW:(p + 1) * PW]
            upd = r2[u][C:]
            s = states[u]
            states[u] = jnp.concatenate(
                [s[:DN_DK] * e + jnp.where(first, upd, 0.0),
                 s[DN_DK:] * e + jnp.where(first, 0.0, upd)], axis=0)
    for u in units:
        s_ref[sidx(u)] = states[u]


def _dscan(a1, a2, au, ae, batch, seq, cps=8):
    nc = seq // CHUNK
    steps = nc // cps
    kern = functools.partial(_dscan_kernel, batch=batch, cps=cps)
    fwd = lambda i: (0, i, 0, 0, 0)
    bwd = lambda i: (0, steps - 1 - i, 1, 0, 0)

    def spec(arr, imap):
        return pl.BlockSpec((batch, cps, None) + arr.shape[3:], imap)

    o_spec = lambda imap: pl.BlockSpec((batch, cps, CHUNK, DN_VW), imap)
    o_shape = jax.ShapeDtypeStruct((batch, nc, CHUNK, DN_VW), BF16)
    return pl.pallas_call(
        kern,
        grid=(steps,),
        in_specs=[spec(a1, fwd), spec(a1, bwd), spec(a2, fwd), spec(a2, bwd),
                  spec(au, fwd), spec(au, bwd), spec(ae, fwd), spec(ae, bwd)],
        out_specs=[o_spec(lambda i: (0, i, 0, 0)), o_spec(lambda i: (0, steps - 1 - i, 0, 0))],
        out_shape=[o_shape, o_shape],
        scratch_shapes=[pltpu.VMEM((batch * 2 * (DN_HEADS // 2), 2 * DN_DK, 2 * DN_DV), F32)],
        compiler_params=pltpu.CompilerParams(dimension_semantics=("arbitrary",)),
        name="dscan",
    )(a1, a1, a2, a2, au, au, ae, ae)


def _post_kernel(oa_ref, of_ref, ob_ref, dz_ref, ga_ref, gd_ref, x_ref, p_ref,
                 wa_ref, wd_ref, wo_ref, wg_ref, wp_ref, nd_ref, npost_ref, nple_ref, o_ref):
    def rms(t, w):
        return t * lax.rsqrt(jnp.mean(t * t, axis=-1, keepdims=True) + EPS) * w

    nd = nd_ref[...]
    parts = []
    for h in range(DN_HEADS):
        sl = slice(h * DN_DV, (h + 1) * DN_DV)
        od = of_ref[:, sl].astype(F32) + ob_ref[:, sl].astype(F32)
        z = dz_ref[:, sl].astype(F32)
        parts.append((rms(od, nd) * (z * jax.nn.sigmoid(z))).astype(BF16))
    y_dn = jnp.dot(jnp.concatenate(parts, axis=1), wd_ref[...], preferred_element_type=F32)
    y_att = jnp.dot(oa_ref[...], wa_ref[...], preferred_element_type=F32)
    merged = (jax.nn.sigmoid(ga_ref[...].astype(F32)) * y_att
              + jax.nn.sigmoid(gd_ref[...].astype(F32)) * y_dn)
    mix = jnp.dot(merged.astype(BF16), wo_ref[...], preferred_element_type=F32)
    x1 = x_ref[...] + rms(mix, npost_ref[...])
    gate = jax.nn.sigmoid(jnp.dot(x1.astype(BF16), wg_ref[...], preferred_element_type=F32))
    e = jnp.dot(p_ref[...].astype(BF16), wp_ref[...], preferred_element_type=F32)
    o_ref[...] = x1 + rms(gate * e, nple_ref[...])


def _post(o_att, o_f, o_b, proj, x2, p2, wa, wd, wo, wg, wp, nd, npost, nple, offs, tm=512):
    n, d = x2.shape
    ple = p2.shape[1]
    row = lambda i: (i, 0)
    const = lambda i: (0, 0)
    return pl.pallas_call(
        _post_kernel,
        grid=(n // tm,),
        in_specs=[
            pl.BlockSpec((tm, ATT_W), row),
            pl.BlockSpec((tm, DN_VW), row),
            pl.BlockSpec((tm, DN_VW), row),
            pl.BlockSpec((tm, DN_VW), lambda i: (i, offs["dz"] // DN_VW)),
            pl.BlockSpec((tm, d), lambda i: (i, offs["ga"] // d)),
            pl.BlockSpec((tm, d), lambda i: (i, offs["gd"] // d)),
            pl.BlockSpec((tm, d), row),
            pl.BlockSpec((tm, ple), row),
            pl.BlockSpec((ATT_W, d), const),
            pl.BlockSpec((DN_VW, d), const),
            pl.BlockSpec((d, d), const),
            pl.BlockSpec((d, d), const),
            pl.BlockSpec((ple, d), const),
            pl.BlockSpec((1, DN_DV), const),
            pl.BlockSpec((1, d), const),
            pl.BlockSpec((1, d), const),
        ],
        out_specs=pl.BlockSpec((tm, d), row),
        out_shape=jax.ShapeDtypeStruct((n, d), F32),
        compiler_params=pltpu.CompilerParams(dimension_semantics=("parallel",)),
        name="post",
    )(o_att, o_f, o_b, proj, proj, proj, x2, p2, wa, wd, wo, wg, wp, nd, npost, nple)


def _rope_tables(seq):
    rows = seq // GRID_W
    row = np.broadcast_to(np.arange(rows)[:, None], (rows, GRID_W)).reshape(seq)
    col = np.broadcast_to(np.arange(GRID_W)[None, :], (rows, GRID_W)).reshape(seq)
    n_freq = HEAD_DIM // 4
    inv_freq = np.float32(ROPE_THETA) ** (-np.arange(n_freq, dtype=np.float32) / np.float32(n_freq))
    ang = np.concatenate([row.astype(np.float32)[:, None] * inv_freq,
                          col.astype(np.float32)[:, None] * inv_freq], axis=-1)
    cos, sin = np.cos(ang), np.sin(ang)
    return (jnp.asarray(np.concatenate([cos, cos], axis=-1), F32),
            jnp.asarray(np.concatenate([-sin, sin], axis=-1), F32))


def kernel(x, p, norm_pre, w_in, q_norm, k_norm, conv_w, a_log, dt_bias, dn_norm, w_br_att,
           w_br_dn, w_out, norm_post, w_ple_proj, w_ple_gate, ple_norm):
    batch, seq, d = x.shape
    depth = w_in.shape[0]
    n = batch * seq
    assert seq % GRID_W == 0 and seq % CHUNK == 0 and d % LANES == 0
    cos, sin = _rope_tables(seq)
    x2 = x.reshape(n, d)
    for i in range(depth):
        w_main, w_small, offs, pieces = _regroup_columns(w_in[i])
        conv_w8 = jnp.pad(conv_w[i], ((0, 8 - CONV_K), (0, 0)))
        proj, small = _proj(x2, norm_pre[i][None, :], w_main, w_small, conv_w8, pieces, seq)

        tq, tk = min(ATT_TQ, seq), min(ATT_TK, seq // 2)
        qt, k_rot, vt = _aprep(proj, cos, sin, _deinterleave(q_norm[i])[None, :],
                               _deinterleave(k_norm[i])[None, :], offs, batch, seq, tq, tk)
        o_att = _attention(proj, qt, k_rot, vt, offs, batch, seq, tq, tk)

        nsm = 4 * DN_HEADS
        nck = n // CHUNK
        small_r = small[:, 2 * DN_HEADS:nsm].reshape(nck, CHUNK, 2, N_TILES, HEADS_PER_TILE)
        small_r = small_r.transpose(0, 2, 3, 4, 1).reshape(nck, 2 * N_TILES, CAT_W)
        small_r = jnp.pad(small_r, ((0, 0), (0, 8 - 2 * N_TILES), (0, 0)))
        par = jnp.stack([a_log[i].reshape(-1), dt_bias[i].reshape(-1)])
        par_col = jnp.pad(jnp.concatenate([jnp.zeros_like(par), par], axis=1),
                          ((0, 6), (0, LANES - nsm)))
        par_row = jnp.repeat(par.reshape(2, 2 * N_TILES, HEADS_PER_TILE), CHUNK, axis=2)
        par_row = jnp.pad(par_row, ((0, 0), (0, 8 - 2 * N_TILES), (0, 0)))
        a1, a2, au, ae = _dchunk(proj, small, small_r, par_col, par_row, offs, batch, seq)
        o_f, o_b = _dscan(a1, a2, au, ae, batch, seq)
        o_f, o_b = o_f.reshape(n, DN_VW), o_b.reshape(n, DN_VW)

        x2 = _post(o_att, o_f, o_b, proj, x2, p[i].reshape(n, -1),
                   w_br_att[i].astype(BF16), w_br_dn[i].astype(BF16), w_out[i].astype(BF16),
                   w_ple_gate[i].astype(BF16), w_ple_proj[i].astype(BF16),
                   dn_norm[i][None, :], norm_post[i][None, :], ple_norm[i][None, :], offs)
    return x2.reshape(batch, seq, d)
```
